```python
import jax
import jax.numpy as jnp
from jax import lax
import numpy as np

D_MODEL = 1024
BATCH = 4
SEQ = 4096
DEPTH = 4
DEC_BATCH = 8
DEC_SEQ = 64
PAST_LEN = 2048

CHUNK = 64
HEAD_DIM = 64
N_EVEN = (DEPTH + 1) // 2
N_ODD = DEPTH // 2
EPS = 1e-6
ROPE_THETA = 10000.0

A_HEADS = 8
A_KV_HEADS = 2
IDX_HEADS = 8
IDX_DIM = 64
IDX_W_SCALE = (IDX_HEADS * IDX_DIM) ** -0.5
TOPK_MAX = 256
QBLOCK = 128

B_HEADS = 8
B_LEFT_CHUNKS = 8
B_WINDOW = B_LEFT_CHUNKS * CHUNK
B_MAX_REL = 128

C_HEADS = 16
C_KV_HEADS = 2
C_WINDOW = 128
C_LEFT_CHUNKS = C_WINDOW // CHUNK

D_FF = -(-8 * D_MODEL // (3 * 256)) * 256

EVEN_SIZES = (A_HEADS * HEAD_DIM, A_KV_HEADS * HEAD_DIM, A_KV_HEADS * HEAD_DIM,
              IDX_HEADS * IDX_DIM, IDX_DIM, IDX_HEADS,
              B_HEADS * HEAD_DIM, B_HEADS * HEAD_DIM, B_HEADS * HEAD_DIM)
EVEN_IN = (A_HEADS + 2 * A_KV_HEADS + 3 * B_HEADS) * HEAD_DIM + (IDX_HEADS + 1) * IDX_DIM + IDX_HEADS
EVEN_OUT = (A_HEADS + B_HEADS) * HEAD_DIM
ODD_SIZES = (C_HEADS * HEAD_DIM, C_KV_HEADS * HEAD_DIM, C_KV_HEADS * HEAD_DIM)
ODD_IN = (C_HEADS + 2 * C_KV_HEADS) * HEAD_DIM
ODD_OUT = C_HEADS * HEAD_DIM

kernel_name = 'hybrid_dsa_chunkband_swa_stream_step'


def rmsnorm(x, g):
    x32 = x.astype(jnp.float32)
    y = x32 * lax.rsqrt(jnp.mean(x32 * x32, axis=-1, keepdims=True) + EPS)
    return (y * g.astype(jnp.float32)).astype(x.dtype)


def rope(x, pos):
    half = x.shape[-1] // 2
    inv_freq = ROPE_THETA ** (-jnp.arange(half, dtype=jnp.float32) / half)
    ang = pos.astype(jnp.float32)[:, None] * inv_freq[None, :]
    cos = jnp.cos(ang)[:, None, :]
    sin = jnp.sin(ang)[:, None, :]
    x32 = x.astype(jnp.float32)
    x1, x2 = x32[..., :half], x32[..., half:]
    return jnp.concatenate([x1 * cos - x2 * sin, x2 * cos + x1 * sin], axis=-1).astype(x.dtype)


def split_cols(h, sizes):
    points = []
    acc = 0
    for s in sizes[:-1]:
        acc += s
        points.append(acc)
    return jnp.split(h, points, axis=-1)


def heads(t, n, d):
    return t.reshape(t.shape[0], t.shape[1], n, d)


def even_project(xn, pos, w_in, a_qn, a_kn, i_kn, b_qn, b_kn):
    aq, ak, av, iq, ik, iw, bq, bk, bv = split_cols(xn @ w_in, EVEN_SIZES)
    aq = rope(rmsnorm(heads(aq, A_HEADS, HEAD_DIM), a_qn), pos)
    ak = rope(rmsnorm(heads(ak, A_KV_HEADS, HEAD_DIM), a_kn), pos)
    av = heads(av, A_KV_HEADS, HEAD_DIM)
    iq = rope(heads(iq, IDX_HEADS, IDX_DIM), pos)
    ik = rope(rmsnorm(ik, i_kn)[:, :, None, :], pos)[:, :, 0, :]
    iw = iw * IDX_W_SCALE
    bq = rmsnorm(heads(bq, B_HEADS, HEAD_DIM), b_qn)
    bk = rmsnorm(heads(bk, B_HEADS, HEAD_DIM), b_kn)
    bv = heads(bv, B_HEADS, HEAD_DIM)
    return aq, ak, av, iq, ik, iw, bq, bk, bv


def odd_project(xn, pos, w_in, c_qn, c_kn):
    q, k, v = split_cols(xn @ w_in, ODD_SIZES)
    q = rope(rmsnorm(heads(q, C_HEADS, HEAD_DIM), c_qn), pos)
    k = rope(rmsnorm(heads(k, C_KV_HEADS, HEAD_DIM), c_kn), pos)
    v = heads(v, C_KV_HEADS, HEAD_DIM)
    return q, k, v


def dsa_attention(q, iq, iw, k, v, ik, q_pos, k_pos, topk):
    rel = jax.nn.relu(jnp.einsum('btnd,bsd->btns', iq.astype(jnp.float32), ik.astype(jnp.float32)))
    score = jnp.einsum('btns,btn->bts', rel, iw.astype(jnp.float32))
    admissible = (k_pos[None, :] // CHUNK) <= (q_pos[:, None] // CHUNK)
    score = jnp.where(admissible[None], score, -jnp.inf)
    sel_score, sel = lax.top_k(score, topk)
    valid = jnp.isfinite(sel_score)
    kg = jax.vmap(lambda kb, ib: kb[ib])(k, sel)
    vg = jax.vmap(lambda vb, ib: vb[ib])(v, sel)
    bn, t = q.shape[0], q.shape[1]
    qg = q.reshape(bn, t, A_KV_HEADS, A_HEADS // A_KV_HEADS, HEAD_DIM)
    logits = jnp.einsum('btkgd,btskd->btkgs', qg, kg).astype(jnp.float32) * HEAD_DIM ** -0.5
    logits = jnp.where(valid[:, :, None, None, :], logits, -jnp.inf)
    p = jax.nn.softmax(logits, axis=-1)
    out = jnp.einsum('btkgs,btskd->btkgd', p.astype(v.dtype), vg)
    return out.reshape(bn, t, A_HEADS * HEAD_DIM)


def band_attention(q, k, v, q_pos, k_pos, n_left, rel_bias, sinks):
    bn, c, t, h = q.shape[0], q.shape[1], q.shape[2], q.shape[3]
    kv = k.shape[3]
    g = h // kv
    l = k.shape[2]
    qc = q_pos // CHUNK
    kc = k_pos // CHUNK
    valid = ((k_pos[:, None, :] >= 0) & (kc[:, None, :] <= qc[:, :, None])
             & (kc[:, None, :] >= qc[:, :, None] - n_left))
    qg = q.reshape(bn, c, t, kv, g, HEAD_DIM)
    logits = jnp.einsum('bctkgd,bcskd->bckgts', qg, k).astype(jnp.float32) * HEAD_DIM ** -0.5
    if rel_bias is not None:
        ridx = jnp.clip(k_pos[:, None, :] - q_pos[:, :, None], -B_MAX_REL, B_MAX_REL) + B_MAX_REL
        bias = rel_bias.astype(jnp.float32)[:, ridx]
        bias = jnp.moveaxis(bias.reshape(kv, g, c, t, l), 2, 0)
        logits = logits + bias[None]
    logits = jnp.where(valid[None, :, None, None], logits, -jnp.inf)
    if sinks is None:
        p = jax.nn.softmax(logits, axis=-1)
    else:
        s = sinks.astype(jnp.float32).reshape(1, 1, kv, g, 1, 1)
        m = jnp.maximum(jnp.max(logits, axis=-1, keepdims=True), s)
        e = jnp.exp(logits - m)
        p = e / (jnp.sum(e, axis=-1, keepdims=True) + jnp.exp(s - m))
    out = jnp.einsum('bckgts,bcskd->bctkgd', p.astype(v.dtype), v)
    return out.reshape(bn, c, t, h * HEAD_DIM)


def to_band(x, n_left):
    bn, s = x.shape[0], x.shape[1]
    nc = s // CHUNK
    pad = [(0, 0), (n_left * CHUNK, 0)] + [(0, 0)] * (x.ndim - 2)
    xp = jnp.pad(x, pad).reshape(bn, nc + n_left, CHUNK, *x.shape[2:])
    band = jnp.stack([xp[:, j:j + nc] for j in range(n_left + 1)], axis=2)
    return band.reshape(bn, nc, (n_left + 1) * CHUNK, *x.shape[2:])


def band_prompt(q, k, v, n_left, rel_bias, sinks):
    bn, s = q.shape[0], q.shape[1]
    nc = s // CHUNK
    q_pos = jnp.arange(s, dtype=jnp.int32).reshape(nc, CHUNK)
    k_pos = ((jnp.arange(nc, dtype=jnp.int32)[:, None] - n_left) * CHUNK
             + jnp.arange((n_left + 1) * CHUNK, dtype=jnp.int32)[None, :])
    out = band_attention(q.reshape(bn, nc, CHUNK, *q.shape[2:]), to_band(k, n_left), to_band(v, n_left),
                         q_pos, k_pos, n_left, rel_bias, sinks)
    return out.reshape(bn, s, -1)


def band_sample(q, k_new, v_new, k_cache, v_cache, n_left, rel_bias, sinks):
    t = q.shape[1]
    w = k_cache.shape[1]
    q_pos = PAST_LEN + jnp.arange(t, dtype=jnp.int32)
    k_pos = jnp.concatenate([PAST_LEN - w + jnp.arange(w, dtype=jnp.int32), q_pos])
    k = jnp.concatenate([k_cache, k_new], axis=1)[:, None]
    v = jnp.concatenate([v_cache, v_new], axis=1)[:, None]
    out = band_attention(q[:, None], k, v, q_pos[None], k_pos[None], n_left, rel_bias, sinks)
    return out[:, 0]


def to_qblocks(t, nb):
    return jnp.moveaxis(t.reshape(t.shape[0], nb, QBLOCK, *t.shape[2:]), 1, 0)


def swiglu(x, w_in, w_out):
    gate, up = jnp.split(x @ w_in, 2, axis=-1)
    return (jax.nn.silu(gate) * up) @ w_out


def setup_inputs(seed: int = 0) -> dict:
    key = jax.random.key(seed)
    ks = jax.random.split(key, 32)

    def nrm(k, shape, scale=1.0):
        return jax.random.normal(k, shape, jnp.float32) * scale

    def gain(k, shape):
        return 1.0 + 0.02 * jax.random.normal(k, shape, jnp.float32)

    wb = min(B_WINDOW, PAST_LEN)
    wc = min(C_WINDOW, PAST_LEN)
    return {
        'x_prompt': nrm(ks[0], (BATCH, SEQ, D_MODEL)),
        'x_sample': nrm(ks[1], (DEC_BATCH, DEC_SEQ, D_MODEL)),
        'cache_a_k': nrm(ks[2], (N_EVEN, DEC_BATCH, PAST_LEN, A_KV_HEADS, HEAD_DIM)),
        'cache_a_v': nrm(ks[3], (N_EVEN, DEC_BATCH, PAST_LEN, A_KV_HEADS, HEAD_DIM)),
        'cache_a_kidx': nrm(ks[4], (N_EVEN, DEC_BATCH, PAST_LEN, IDX_DIM)),
        'cache_b_k': nrm(ks[5], (N_EVEN, DEC_BATCH, wb, B_HEADS, HEAD_DIM)),
        'cache_b_v': nrm(ks[6], (N_EVEN, DEC_BATCH, wb, B_HEADS, HEAD_DIM)),
        'cache_c_k': nrm(ks[7], (N_ODD, DEC_BATCH, wc, C_KV_HEADS, HEAD_DIM)),
        'cache_c_v': nrm(ks[8], (N_ODD, DEC_BATCH, wc, C_KV_HEADS, HEAD_DIM)),
        'norm_mix': gain(ks[9], (DEPTH, D_MODEL)),
        'norm_ffn': gain(ks[10], (DEPTH, D_MODEL)),
        'w_in_even': nrm(ks[11], (N_EVEN, D_MODEL, EVEN_IN), D_MODEL ** -0.5),
        'w_out_even': nrm(ks[12], (N_EVEN, EVEN_OUT, D_MODEL), EVEN_OUT ** -0.5),
        'a_q_norm': gain(ks[13], (N_EVEN, HEAD_DIM)),
        'a_k_norm': gain(ks[14], (N_EVEN, HEAD_DIM)),
        'idx_k_norm': gain(ks[15], (N_EVEN, IDX_DIM)),
        'b_q_norm': gain(ks[16], (N_EVEN, HEAD_DIM)),
        'b_k_norm': gain(ks[17], (N_EVEN, HEAD_DIM)),
        'b_rel_bias': nrm(ks[18], (N_EVEN, B_HEADS, 2 * B_MAX_REL + 1), 0.1),
        'w_in_odd': nrm(ks[19], (N_ODD, D_MODEL, ODD_IN), D_MODEL ** -0.5),
        'w_out_odd': nrm(ks[20], (N_ODD, ODD_OUT, D_MODEL), ODD_OUT ** -0.5),
        'c_q_norm': gain(ks[21], (N_ODD, HEAD_DIM)),
        'c_k_norm': gain(ks[22], (N_ODD, HEAD_DIM)),
        'c_sinks': nrm(ks[23], (N_ODD, C_HEADS), 0.5),
        'w_ffn_in': nrm(ks[24], (DEPTH, D_MODEL, 2 * D_FF), D_MODEL ** -0.5),
        'w_ffn_out': nrm(ks[25], (DEPTH, D_FF, D_MODEL), D_FF ** -0.5),
    }


def reference(x_prompt, x_sample, cache_a_k, cache_a_v, cache_a_kidx, cache_b_k, cache_b_v,
              cache_c_k, cache_c_v, norm_mix, norm_ffn, w_in_even, w_out_even, a_q_norm, a_k_norm,
              idx_k_norm, b_q_norm, b_k_norm, b_rel_bias, w_in_odd, w_out_odd, c_q_norm, c_k_norm,
              c_sinks, w_ffn_in, w_ffn_out):
    seq = x_prompt.shape[1]
    t_new = x_sample.shape[1]
    pos_p = jnp.arange(seq, dtype=jnp.int32)
    pos_s = PAST_LEN + jnp.arange(t_new, dtype=jnp.int32)
    k_pos_s = jnp.arange(PAST_LEN + t_new, dtype=jnp.int32)
    topk_p = min(TOPK_MAX, seq // 4)
    topk_s = min(TOPK_MAX, (PAST_LEN + t_new) // 4)
    nb = seq // QBLOCK
    keep_b = min(B_WINDOW, seq)
    keep_c = min(C_WINDOW, seq)
    pa_k, pa_v, pa_i, pb_k, pb_v, pc_k, pc_v = [], [], [], [], [], [], []
    sa_k, sa_v, sa_i, sb_k, sb_v, sc_k, sc_v = [], [], [], [], [], [], []

    hp, hs = x_prompt, x_sample
    for layer in range(DEPTH):
        li = layer // 2
        xp = rmsnorm(hp, norm_mix[layer])
        xs = rmsnorm(hs, norm_mix[layer])
        if layer % 2 == 0:
            ew = (w_in_even[li], a_q_norm[li], a_k_norm[li], idx_k_norm[li], b_q_norm[li], b_k_norm[li])
            aq, ak, av, iq, ik, iw, bq, bk, bv = even_project(xp, pos_p, *ew)

            def attend_block(blk, ak=ak, av=av, ik=ik):
                qb, iqb, iwb, pb = blk
                return dsa_attention(qb, iqb, iwb, ak, av, ik, pb, pos_p, topk_p)

            out_a = lax.map(attend_block, (to_qblocks(aq, nb), to_qblocks(iq, nb), to_qblocks(iw, nb),
                                           pos_p.reshape(nb, QBLOCK)))
            out_a = jnp.moveaxis(out_a, 0, 1).reshape(hp.shape[0], seq, A_HEADS * HEAD_DIM)
            out_b = band_prompt(bq, bk, bv, B_LEFT_CHUNKS, b_rel_bias[li], None)
            mix_p = jnp.concatenate([out_a, out_b], axis=-1) @ w_out_even[li]
            pa_k.append(ak)
            pa_v.append(av)
            pa_i.append(ik)
            pb_k.append(bk[:, seq - keep_b:])
            pb_v.append(bv[:, seq - keep_b:])
            aq, ak, av, iq, ik, iw, bq, bk, bv = even_project(xs, pos_s, *ew)
            k_all = jnp.concatenate([cache_a_k[li], ak], axis=1)
            v_all = jnp.concatenate([cache_a_v[li], av], axis=1)
            i_all = jnp.concatenate([cache_a_kidx[li], ik], axis=1)
            out_a = dsa_attention(aq, iq, iw, k_all, v_all, i_all, pos_s, k_pos_s, topk_s)
            out_b = band_sample(bq, bk, bv, cache_b_k[li], cache_b_v[li], B_LEFT_CHUNKS, b_rel_bias[li], None)
            mix_s = jnp.concatenate([out_a, out_b], axis=-1) @ w_out_even[li]
            sa_k.append(ak)
            sa_v.append(av)
            sa_i.append(ik)
            sb_k.append(bk)
            sb_v.append(bv)
        else:
            q, k, v = odd_project(xp, pos_p, w_in_odd[li], c_q_norm[li], c_k_norm[li])
            mix_p = band_prompt(q, k, v, C_LEFT_CHUNKS, None, c_sinks[li]) @ w_out_odd[li]
            pc_k.append(k[:, seq - keep_c:])
            pc_v.append(v[:, seq - keep_c:])
            q, k, v = odd_project(xs, pos_s, w_in_odd[li], c_q_norm[li], c_k_norm[li])
            mix_s = band_sample(q, k, v, cache_c_k[li], cache_c_v[li], C_LEFT_CHUNKS, None, c_sinks[li]) @ w_out_odd[li]
            sc_k.append(k)
            sc_v.append(v)
        hp = hp + mix_p
        hs = hs + mix_s
        hp = hp + swiglu(rmsnorm(hp, norm_ffn[layer]), w_ffn_in[layer], w_ffn_out[layer])
        hs = hs + swiglu(rmsnorm(hs, norm_ffn[layer]), w_ffn_in[layer], w_ffn_out[layer])

    return (hp, hs,
            jnp.stack(pa_k), jnp.stack(pa_v), jnp.stack(pa_i), jnp.stack(pb_k), jnp.stack(pb_v),
            jnp.stack(pc_k), jnp.stack(pc_v),
            jnp.stack(sa_k), jnp.stack(sa_v), jnp.stack(sa_i), jnp.stack(sb_k), jnp.stack(sb_v),
            jnp.stack(sc_k), jnp.stack(sc_v))
```

```python
import functools

import numpy as np
import jax
import jax.numpy as jnp
from jax import lax
from jax.experimental import pallas as pl
from jax.experimental.pallas import tpu as pltpu

CHUNK = 64
HEAD_DIM = 64
EPS = 1e-6
ROPE_THETA = 10000.0
A_HEADS = 8
A_KV_HEADS = 2
IDX_HEADS = 8
IDX_DIM = 64
IDX_W_SCALE = (IDX_HEADS * IDX_DIM) ** -0.5
TOPK_MAX = 256
B_HEADS = 8
B_LEFT_CHUNKS = 8
B_MAX_REL = 128
C_HEADS = 16
C_KV_HEADS = 2
C_LEFT_CHUNKS = 2
QK_SCALE = HEAD_DIM ** -0.5

LANES = 128
MXU_COLS = 256
VMEM_LIMIT = 56 * 1024 * 1024

NEG = -1e30
INT_MIN = np.int32(-2 ** 31)
F32 = jnp.float32
BF16 = jnp.bfloat16

A_PERM = tuple(c + (A_HEADS // 2) * p for c in range(A_HEADS // 2) for p in range(2))
C_PERM = tuple(c + (C_HEADS // 2) * p for c in range(C_HEADS // 2) for p in range(2))


def _dot(a, b):
    return jnp.dot(a, b, preferred_element_type=F32)


def _dot_nt(a, b):
    return lax.dot_general(a, b, (((1,), (1,)), ((), ())), preferred_element_type=F32)


def _row_tile(n):
    for t in (512, 256, 128, 64):
        if n % t == 0:
            return t
    raise ValueError(f"row count {n} is not a multiple of {CHUNK}")


def _const_spec(shape):
    nd = len(shape)
    return pl.BlockSpec(shape, lambda *_: (0,) * nd)


def _params(n_axes):
    return pltpu.CompilerParams(dimension_semantics=("arbitrary",) * n_axes,
                                vmem_limit_bytes=VMEM_LIMIT)


def _group_ms(hb, bd):
    sq = (hb * hb).astype(BF16)
    w = hb.shape[1]
    parts = [_dot(sq[:, i:i + MXU_COLS], bd) for i in range(0, w, MXU_COLS)]
    return parts[0] if len(parts) == 1 else jnp.concatenate(parts, axis=1)


def _rope_blocks(y, cos, sin, first_half):
    out = []
    for i in range(0, y.shape[1], LANES):
        yb = y[:, i:i + LANES]
        sw = jnp.where(first_half, pltpu.roll(yb, LANES - 32, 1), pltpu.roll(yb, 32, 1))
        out.append(yb * cos + sw * sin)
    return out[0] if len(out) == 1 else jnp.concatenate(out, axis=1)


def _normed_input(x_ref, g_ref):
    x = x_ref[...]
    ms = jnp.mean(x * x, axis=-1, keepdims=True)
    return (x * lax.rsqrt(ms + EPS) * g_ref[...]).astype(BF16)


def _even_in_kernel(x_ref, g_ref, w_ref, gain_ref, bd_ref, cos_ref, sin_ref,
                    aq_ref, iq_ref, bq_ref, bk_ref, bv_ref, bk16_ref, bv16_ref,
                    ak_ref, av_ref, ak16_ref, av16_ref, ik_ref, ik16_ref, iw_ref):
    tm = x_ref.shape[0]
    xn = _normed_input(x_ref, g_ref)
    bd = bd_ref[...]
    cos = cos_ref[...]
    sin = sin_ref[...]
    first_half = (lax.broadcasted_iota(jnp.int32, (tm, LANES), 1) % HEAD_DIM) < HEAD_DIM // 2

    def proj(c0, width):
        return _dot(xn, w_ref[:, c0:c0 + width])

    def normed(h, c0):
        return h * lax.rsqrt(_group_ms(h, bd) + EPS) * gain_ref[:, c0:c0 + h.shape[1]]

    h = proj(0, 512)
    aq_ref[...] = (_rope_blocks(normed(h, 0), cos, sin, first_half) * QK_SCALE).astype(BF16)
    h = proj(512, 512)
    iq_ref[...] = _rope_blocks(h, cos, sin, first_half).astype(BF16)
    h = proj(1024, 512)
    bq_ref[...] = (normed(h, 1024) * QK_SCALE).astype(BF16)
    h = normed(proj(1536, 512), 1536)
    bk_ref[...] = h
    bk16_ref[...] = h.astype(BF16)
    h = proj(2048, 512)
    bv_ref[...] = h
    bv16_ref[...] = h.astype(BF16)
    h = proj(2560, 256)
    k = _rope_blocks(normed(h, 2560)[:, :LANES], cos, sin, first_half)
    ak_ref[...] = k
    ak16_ref[...] = k.astype(BF16)
    v = h[:, LANES:]
    av_ref[...] = v
    av16_ref[...] = v.astype(BF16)
    h = proj(2816, 256)
    k = _rope_blocks(normed(h, 2816)[:, :LANES], cos, sin, first_half)
    ik_ref[...] = k
    ik16_ref[...] = k.astype(BF16)
    iw_ref[...] = h[:, LANES:] * IDX_W_SCALE


def _even_in(x, g, w, gains, bd, cos, sin):
    n, d = x.shape
    tm = _row_tile(n)
    row = lambda width: pl.BlockSpec((tm, width), lambda i: (i, 0))
    widths_dtypes = [(512, BF16), (512, BF16), (512, BF16), (512, F32), (512, F32), (512, BF16), (512, BF16),
                     (128, F32), (128, F32), (128, BF16), (128, BF16), (128, F32), (128, BF16), (128, F32)]
    return pl.pallas_call(
        _even_in_kernel,
        grid=(n // tm,),
        in_specs=[row(d), _const_spec((1, d)), _const_spec(w.shape), _const_spec(gains.shape),
                  _const_spec(bd.shape), row(LANES), row(LANES)],
        out_specs=[row(wd) for wd, _ in widths_dtypes],
        out_shape=[jax.ShapeDtypeStruct((n, wd), dt) for wd, dt in widths_dtypes],
        compiler_params=_params(1),
        name="even_in_proj",
    )(x, g, w, gains, bd, cos, sin)


def _odd_in_kernel(x_ref, g_ref, w_ref, gain_ref, bd_ref, cos_ref, sin_ref,
                   q_ref, k_ref, v_ref, k16_ref, v16_ref):
    tm = x_ref.shape[0]
    xn = _normed_input(x_ref, g_ref)
    bd = bd_ref[...]
    cos = cos_ref[...]
    sin = sin_ref[...]
    first_half = (lax.broadcasted_iota(jnp.int32, (tm, LANES), 1) % HEAD_DIM) < HEAD_DIM // 2
    for c0 in (0, 512):
        h = _dot(xn, w_ref[:, c0:c0 + 512])
        h = h * lax.rsqrt(_group_ms(h, bd) + EPS) * gain_ref[:, c0:c0 + 512]
        q_ref[:, c0:c0 + 512] = (_rope_blocks(h, cos, sin, first_half) * QK_SCALE).astype(BF16)
    h = _dot(xn, w_ref[:, 1024:1280])
    hn = h * lax.rsqrt(_group_ms(h, bd) + EPS) * gain_ref[:, 1024:1280]
    k = _rope_blocks(hn[:, :LANES], cos, sin, first_half)
    k_ref[...] = k
    k16_ref[...] = k.astype(BF16)
    v = h[:, LANES:]
    v_ref[...] = v
    v16_ref[...] = v.astype(BF16)


def _odd_in(x, g, w, gains, bd, cos, sin):
    n, d = x.shape
    tm = _row_tile(n)
    row = lambda width: pl.BlockSpec((tm, width), lambda i: (i, 0))
    widths_dtypes = [(1024, BF16), (128, F32), (128, F32), (128, BF16), (128, BF16)]
    return pl.pallas_call(
        _odd_in_kernel,
        grid=(n // tm,),
        in_specs=[row(d), _const_spec((1, d)), _const_spec(w.shape), _const_spec(gains.shape),
                  _const_spec(bd.shape), row(LANES), row(LANES)],
        out_specs=[row(wd) for wd, _ in widths_dtypes],
        out_shape=[jax.ShapeDtypeStruct((n, wd), dt) for wd, dt in widths_dtypes],
        compiler_params=_params(1),
        name="odd_in_proj",
    )(x, g, w, gains, bd, cos, sin)


def _out_ffn_kernel(*refs, n_attn, d_ff, ff_chunk):
    h_ref = refs[0]
    attn_refs = refs[1:1 + n_attn]
    wo_refs = refs[1 + n_attn:1 + 2 * n_attn]
    g_ref, wgu_ref, wdn_ref, o_ref, yn_ref = refs[1 + 2 * n_attn:]
    o_ref[...] = h_ref[...]
    for a_ref, wo_ref in zip(attn_refs, wo_refs):
        o_ref[...] += _dot(a_ref[...], wo_ref[...])
    y = o_ref[...]
    ms = jnp.mean(y * y, axis=-1, keepdims=True)
    yn_ref[...] = (y * lax.rsqrt(ms + EPS) * g_ref[...]).astype(BF16)
    for c0 in range(0, d_ff, ff_chunk):
        gate = _dot(yn_ref[...], wgu_ref[:, c0:c0 + ff_chunk])
        up = _dot(yn_ref[...], wgu_ref[:, d_ff + c0:d_ff + c0 + ff_chunk])
        act = (gate * (1.0 / (1.0 + jnp.exp(-gate))) * up).astype(BF16)
        o_ref[...] += _dot(act, wdn_ref[c0:c0 + ff_chunk, :])


def _out_ffn(h, attns, wos, g, wgu, wdn):
    n, d = h.shape
    tm = _row_tile(n)
    d_ff = wdn.shape[0]
    row = lambda width: pl.BlockSpec((tm, width), lambda i: (i, 0))
    kern = functools.partial(_out_ffn_kernel, n_attn=len(attns), d_ff=d_ff, ff_chunk=MXU_COLS)
    return pl.pallas_call(
        kern,
        grid=(n // tm,),
        in_specs=[row(d)] + [row(a.shape[1]) for a in attns] + [_const_spec(w.shape) for w in wos]
                 + [_const_spec((1, d)), _const_spec(wgu.shape), _const_spec(wdn.shape)],
        out_specs=row(d),
        out_shape=jax.ShapeDtypeStruct((n, d), F32),
        scratch_shapes=[pltpu.VMEM((tm, d), BF16)],
        compiler_params=_params(1),
        name="out_proj_ffn",
    )(h, *attns, *wos, g, wgu, wdn)


def _dsa_kernel(aq_ref, iq_ref, iw_ref, ik_ref, ak_ref, av_ref, o_ref,
                iqs_ref, aqs_ref, wb_ref, keys_ref, m_ref, l_ref, acc_ref,
                *, tq, tk, q_off, l_valid, topk, idx_bits):
    n_slots = IDX_HEADS
    j = pl.program_id(1)
    qpos0 = q_off + j * tq
    n_adm_max = jnp.minimum(((qpos0 + tq - 1) // CHUNK + 1) * CHUNK, l_valid)
    nkt = (n_adm_max + tk - 1) // tk
    lane = lax.broadcasted_iota(jnp.int32, (tq, LANES), 1)
    half = (lane < HEAD_DIM, lane >= HEAD_DIM)
    lane_rep = tk // LANES

    for c in range(n_slots // 2):
        iqb = iq_ref[:, c * LANES:(c + 1) * LANES]
        aqb = aq_ref[:, c * LANES:(c + 1) * LANES]
        for p in range(2):
            n = 2 * c + p
            iqs_ref[n * tq:(n + 1) * tq, :] = jnp.where(half[p], iqb, jnp.zeros_like(iqb))
            aqs_ref[n * tq:(n + 1) * tq, :] = jnp.where(half[p], aqb, jnp.zeros_like(aqb))
            wb_ref[n] = jnp.broadcast_to(iw_ref[:, n:n + 1], (tq, LANES))

    qrow = qpos0 + lax.broadcasted_iota(jnp.int32, (tq, 1), 0)
    klim = jnp.minimum((qrow // CHUNK + 1) * CHUNK, l_valid)
    col = lax.broadcasted_iota(jnp.int32, (tq, tk), 1)

    def score_tile(kt, carry):
        k0 = pl.multiple_of(kt * tk, tk)
        s = _dot_nt(iqs_ref[...], ik_ref[pl.ds(k0, tk), :])
        score = jnp.zeros((tq, tk), F32)
        for n in range(n_slots):
            w = wb_ref[n]
            wt = w if lane_rep == 1 else jnp.concatenate([w] * lane_rep, axis=1)
            score = score + jnp.maximum(s[n * tq:(n + 1) * tq, :], 0.0) * wt
        bits = lax.bitcast_convert_type(score, jnp.int32)
        key = bits ^ ((bits >> 31) & jnp.int32(0x7FFFFFFF))
        key = jnp.where(key == -1, 0, key)
        keys_ref[kt] = jnp.where(col + k0 < klim, key, INT_MIN)
        return carry

    lax.fori_loop(0, nkt, score_tile, 0)

    def count(indicator):
        def body(kt, acc):
            ind = indicator(keys_ref[kt], kt)
            for cb in range(lane_rep):
                acc = acc + ind[:, cb * LANES:(cb + 1) * LANES]
            return acc
        acc = lax.fori_loop(0, nkt, body, jnp.zeros((tq, LANES), F32))
        return jnp.sum(acc, axis=-1, keepdims=True)

    def search_bit(i, ans):
        cand = ans | jnp.left_shift(jnp.int32(1), 31 - i)
        cand_s = cand ^ INT_MIN
        cnt = count(lambda k, kt: jnp.where(k >= cand_s, 1.0, 0.0))
        return jnp.where(cnt >= topk, cand, ans)

    ans = lax.fori_loop(0, 32, search_bit, jnp.zeros((tq, 1), jnp.int32))
    vstar = ans ^ INT_MIN
    c_gt = count(lambda k, kt: jnp.where(k > vstar, 1.0, 0.0))
    c_ge = count(lambda k, kt: jnp.where(k >= vstar, 1.0, 0.0))
    need = topk - c_gt
    has_tie = jnp.where((c_ge > topk) & (vstar > INT_MIN), 1.0, 0.0)
    idx_all = jnp.int32(2 ** idx_bits - 1)

    def tie_search():
        def tie_bit(i, jmax):
            cand = jmax | jnp.left_shift(jnp.int32(1), idx_bits - 1 - i)
            cnt = count(lambda k, kt: jnp.where(k == vstar, jnp.where(col + kt * tk < cand, 1.0, 0.0), 0.0))
            return jnp.where(cnt <= need, cand, jmax)
        return lax.fori_loop(0, idx_bits, tie_bit, jnp.zeros((tq, 1), jnp.int32))

    jmax = lax.cond(jnp.max(has_tie) > 0.0, tie_search, lambda: jnp.full((tq, 1), idx_all, jnp.int32))
    jmax = jnp.where(vstar == INT_MIN, 0, jmax)

    m_ref[...] = jnp.full(m_ref.shape, NEG, F32)
    l_ref[...] = jnp.zeros(l_ref.shape, F32)
    acc_ref[...] = jnp.zeros(acc_ref.shape, F32)

    def attend_tile(kt, carry):
        k0 = pl.multiple_of(kt * tk, tk)
        key = keys_ref[kt]
        tie_ok = jnp.where(col + k0 < jmax, 0.0, NEG)
        mb = jnp.where(key > vstar, 0.0, jnp.where(key == vstar, tie_ok, NEG))
        s_all = _dot_nt(aqs_ref[...], ak_ref[pl.ds(k0, tk), :])
        vt = av_ref[pl.ds(k0, tk), :]
        for n in range(n_slots):
            s = s_all[n * tq:(n + 1) * tq, :] + mb
            m_old = m_ref[n]
            m_new = jnp.maximum(m_old, jnp.max(s, axis=-1, keepdims=True))
            alpha = jnp.exp(m_old - m_new)
            mt = m_new if lane_rep == 1 else jnp.concatenate([m_new] * lane_rep, axis=1)
            p = jnp.exp(s - mt)
            l_ref[n] = alpha * l_ref[n] + jnp.sum(p, axis=-1, keepdims=True)
            acc_ref[n] = alpha * acc_ref[n] + _dot(p.astype(BF16), vt)
            m_ref[n] = m_new
        return carry

    lax.fori_loop(0, nkt, attend_tile, 0)

    for c in range(n_slots // 2):
        o0 = acc_ref[2 * c] / l_ref[2 * c]
        o1 = acc_ref[2 * c + 1] / l_ref[2 * c + 1]
        o_ref[:, c * LANES:(c + 1) * LANES] = jnp.where(half[0], o0, o1).astype(BF16)


def _dsa(aq, iq, iw, ik, ak, av, *, n_batch, t, row_block0, tq, tk, q_off, l_valid, topk, keys_3d):
    nq = t // tq
    l_pad = ik.shape[1] if keys_3d else t
    assert l_pad % tk == 0 and l_valid <= l_pad
    nkt_max = l_pad // tk
    qspec = lambda width: pl.BlockSpec((tq, width), lambda b, j: (row_block0 + b * nq + j, 0))
    if keys_3d:
        kspec = pl.BlockSpec((None, l_pad, LANES), lambda b, j: (b, 0, 0))
    else:
        kspec = pl.BlockSpec((l_pad, LANES), lambda b, j: (b, 0))
    kern = functools.partial(_dsa_kernel, tq=tq, tk=tk, q_off=q_off, l_valid=l_valid, topk=topk,
                             idx_bits=int(l_pad).bit_length())
    n_slots = IDX_HEADS
    return pl.pallas_call(
        kern,
        grid=(n_batch, nq),
        in_specs=[qspec(512), qspec(512), qspec(LANES), kspec, kspec, kspec],
        out_specs=pl.BlockSpec((tq, 512), lambda b, j: (b * nq + j, 0)),
        out_shape=jax.ShapeDtypeStruct((n_batch * t, 512), BF16),
        scratch_shapes=[
            pltpu.VMEM((n_slots * tq, LANES), BF16),
            pltpu.VMEM((n_slots * tq, LANES), BF16),
            pltpu.VMEM((n_slots, tq, LANES), F32),
            pltpu.VMEM((nkt_max, tq, tk), jnp.int32),
            pltpu.VMEM((n_slots, tq, LANES), F32),
            pltpu.VMEM((n_slots, tq, LANES), F32),
            pltpu.VMEM((n_slots, tq, LANES), F32),
        ],
        compiler_params=_params(2),
        name="dsa_attention",
    )(aq, iq, iw, ik, ak, av)


def _band_b_kernel(q_ref, k_ref, v_ref, bias_ref, o_ref, *, n_chunks, n_front, front_valid):
    t = pl.program_id(1)
    bw = n_front + CHUNK
    hq = MXU_COLS // HEAD_DIM
    lane = lax.broadcasted_iota(jnp.int32, (CHUNK, MXU_COLS), 1) // HEAD_DIM
    ucol = lax.broadcasted_iota(jnp.int32, (1, 1, bw), 2)
    for ci in range(n_chunks):
        c = t * n_chunks + ci
        r0 = pl.multiple_of(c * CHUNK, CHUNK)
        for cq in range(B_HEADS // hq):
            cols = slice(cq * MXU_COLS, (cq + 1) * MXU_COLS)
            qq = q_ref[ci * CHUNK:(ci + 1) * CHUNK, cols]
            kq = k_ref[pl.ds(r0, bw), cols]
            vq = v_ref[pl.ds(r0, bw), cols]
            lhs = jnp.concatenate([jnp.where(lane == e, qq, jnp.zeros_like(qq)) for e in range(hq)], axis=0)
            s = _dot_nt(lhs, kq).reshape(hq, CHUNK, bw) + bias_ref[cq * hq:(cq + 1) * hq]
            if not front_valid:
                s = jnp.where(ucol + c * CHUNK >= n_front, s, NEG)
            m = jnp.max(s, axis=-1, keepdims=True)
            e_ = jnp.exp(s - m)
            den = jnp.sum(e_, axis=-1, keepdims=True)
            o = _dot(e_.reshape(hq * CHUNK, bw).astype(BF16), vq).reshape(hq, CHUNK, MXU_COLS) / den
            out = jnp.where(lane == 0, o[0], 0.0)
            for e in range(1, hq):
                out = jnp.where(lane == e, o[e], out)
            o_ref[ci * CHUNK:(ci + 1) * CHUNK, cols] = out.astype(BF16)


def _band_b(q, k, v, bias, *, n_batch, t, row_block0, n_chunks, front_valid):
    rows = n_chunks * CHUNK
    nq = t // rows
    n_front = B_LEFT_CHUNKS * CHUNK
    kvspec = pl.BlockSpec((None, n_front + t, 512), lambda b, j: (b, 0, 0))
    kern = functools.partial(_band_b_kernel, n_chunks=n_chunks, n_front=n_front, front_valid=front_valid)
    return pl.pallas_call(
        kern,
        grid=(n_batch, nq),
        in_specs=[pl.BlockSpec((rows, 512), lambda b, j: (row_block0 + b * nq + j, 0)), kvspec, kvspec,
                  _const_spec(bias.shape)],
        out_specs=pl.BlockSpec((rows, 512), lambda b, j: (b * nq + j, 0)),
        out_shape=jax.ShapeDtypeStruct((n_batch * t, 512), BF16),
        compiler_params=_params(2),
        name="band_b_attention",
    )(q, k, v, bias)


def _band_c_kernel(sink_ref, q_ref, k_ref, v_ref, o_ref, *, n_chunks, n_front, front_valid):
    t = pl.program_id(1)
    bw = n_front + CHUNK
    n_slots = C_HEADS
    lane = lax.broadcasted_iota(jnp.int32, (CHUNK, LANES), 1)
    half = (lane < HEAD_DIM, lane >= HEAD_DIM)
    ucol = lax.broadcasted_iota(jnp.int32, (1, 1, bw), 2)
    sink = jnp.concatenate([jnp.full((1, CHUNK, 1), sink_ref[n], F32) for n in range(n_slots)], axis=0)
    for ci in range(n_chunks):
        c = t * n_chunks + ci
        r0 = pl.multiple_of(c * CHUNK, CHUNK)
        kb = k_ref[pl.ds(r0, bw), :]
        vb = v_ref[pl.ds(r0, bw), :]
        parts = []
        for cb in range(n_slots // 2):
            qb = q_ref[ci * CHUNK:(ci + 1) * CHUNK, cb * LANES:(cb + 1) * LANES]
            for p in range(2):
                parts.append(jnp.where(half[p], qb, jnp.zeros_like(qb)))
        s = _dot_nt(jnp.concatenate(parts, axis=0), kb).reshape(n_slots, CHUNK, bw)
        if not front_valid:
            s = jnp.where(ucol + c * CHUNK >= n_front, s, NEG)
        m = jnp.maximum(jnp.max(s, axis=-1, keepdims=True), sink)
        e_ = jnp.exp(s - m)
        den = jnp.sum(e_, axis=-1, keepdims=True) + jnp.exp(sink - m)
        o = _dot(e_.reshape(n_slots * CHUNK, bw).astype(BF16), vb).reshape(n_slots, CHUNK, LANES) / den
        for cb in range(n_slots // 2):
            o_ref[ci * CHUNK:(ci + 1) * CHUNK, cb * LANES:(cb + 1) * LANES] = (
                jnp.where(half[0], o[2 * cb], o[2 * cb + 1]).astype(BF16))


def _band_c(sinks, q, k, v, *, n_batch, t, row_block0, n_chunks, front_valid):
    rows = n_chunks * CHUNK
    nq = t // rows
    n_front = C_LEFT_CHUNKS * CHUNK
    kvspec = pl.BlockSpec((None, n_front + t, LANES), lambda b, j, s: (b, 0, 0))
    kern = functools.partial(_band_c_kernel, n_chunks=n_chunks, n_front=n_front, front_valid=front_valid)
    grid_spec = pltpu.PrefetchScalarGridSpec(
        num_scalar_prefetch=1,
        grid=(n_batch, nq),
        in_specs=[pl.BlockSpec((rows, 1024), lambda b, j, s: (row_block0 + b * nq + j, 0)), kvspec, kvspec],
        out_specs=pl.BlockSpec((rows, 1024), lambda b, j, s: (b * nq + j, 0)),
    )
    return pl.pallas_call(
        kern,
        grid_spec=grid_spec,
        out_shape=jax.ShapeDtypeStruct((n_batch * t, 1024), BF16),
        compiler_params=_params(2),
        name="band_c_attention",
    )(sinks, q, k, v)


def _rope_tables(pos):
    half = HEAD_DIM // 2
    inv_freq = ROPE_THETA ** (-jnp.arange(half, dtype=F32) / half)
    ang = pos.astype(F32)[:, None] * inv_freq[None, :]
    cos = jnp.cos(ang)
    sin = jnp.sin(ang)
    return (jnp.concatenate([cos, cos, cos, cos], axis=1), jnp.concatenate([-sin, sin, -sin, sin], axis=1))


def _perm_heads_cols(w, perm):
    d = w.shape[0]
    return w.reshape(d, len(perm), HEAD_DIM)[:, np.asarray(perm)].reshape(d, len(perm) * HEAD_DIM)


def _perm_heads_rows(w, perm):
    d = w.shape[1]
    return w.reshape(len(perm), HEAD_DIM, d)[np.asarray(perm)].reshape(len(perm) * HEAD_DIM, d)


def _even_weights(w_in, a_qn, a_kn, i_kn, b_qn, b_kn):
    d = w_in.shape[0]
    sizes = (A_HEADS * HEAD_DIM, A_KV_HEADS * HEAD_DIM, A_KV_HEADS * HEAD_DIM, IDX_HEADS * IDX_DIM, IDX_DIM,
             IDX_HEADS, B_HEADS * HEAD_DIM, B_HEADS * HEAD_DIM, B_HEADS * HEAD_DIM)
    aq, ak, av, iq, ik, iw, bq, bk, bv = jnp.split(w_in, np.cumsum(sizes)[:-1].tolist(), axis=1)
    w = jnp.concatenate([_perm_heads_cols(aq, A_PERM), iq, bq, bk, bv, ak, av, ik, ik,
                         iw, jnp.zeros((d, LANES - IDX_HEADS), w_in.dtype)], axis=1).astype(BF16)
    one = lambda n: jnp.ones((n,), F32)
    gains = jnp.concatenate([jnp.tile(a_qn, A_HEADS), one(512), jnp.tile(b_qn, B_HEADS), jnp.tile(b_kn, B_HEADS),
                             one(512), jnp.tile(a_kn, A_KV_HEADS), one(128), jnp.tile(i_kn, 2), one(128)])
    return w, gains[None, :].astype(F32)


def _odd_weights(w_in, c_qn, c_kn):
    q, k, v = jnp.split(w_in, [C_HEADS * HEAD_DIM, (C_HEADS + C_KV_HEADS) * HEAD_DIM], axis=1)
    w = jnp.concatenate([_perm_heads_cols(q, C_PERM), k, v], axis=1).astype(BF16)
    gains = jnp.concatenate([jnp.tile(c_qn, C_HEADS), jnp.tile(c_kn, C_KV_HEADS), jnp.ones((128,), F32)])
    return w, gains[None, :].astype(F32)


def _pad_rows(x, n, front):
    pad = [(0, 0)] * x.ndim
    pad[1] = (n, 0) if front else (0, n)
    return jnp.pad(x, pad)


def kernel(x_prompt, x_sample, cache_a_k, cache_a_v, cache_a_kidx, cache_b_k, cache_b_v, cache_c_k, cache_c_v, norm_mix, norm_ffn, w_in_even, w_out_even, a_q_norm, a_k_norm, idx_k_norm, b_q_norm, b_k_norm, b_rel_bias, w_in_odd, w_out_odd, c_q_norm, c_k_norm, c_sinks, w_ffn_in, w_ffn_out):
    nb, seq, d = x_prompt.shape
    ns, t_new, _ = x_sample.shape
    past = cache_a_k.shape[2]
    depth = norm_mix.shape[0]
    d_ff = w_ffn_out.shape[1]
    assert seq % CHUNK == 0 and t_new == CHUNK and past % CHUNK == 0
    assert cache_b_k.shape[2] == B_LEFT_CHUNKS * CHUNK and cache_c_k.shape[2] == C_LEFT_CHUNKS * CHUNK
    topk_p = min(TOPK_MAX, seq // 4)
    topk_s = min(TOPK_MAX, (past + t_new) // 4)
    keep_b = min(B_LEFT_CHUNKS * CHUNK, seq)
    keep_c = min(C_LEFT_CHUNKS * CHUNK, seq)

    hp = x_prompt.reshape(nb * seq, d)
    hs = x_sample.reshape(ns * t_new, d)
    pos_p = jnp.tile(jnp.arange(seq, dtype=jnp.int32), nb)
    pos_s = jnp.tile(past + jnp.arange(t_new, dtype=jnp.int32), ns)
    cos_p, sin_p = _rope_tables(pos_p)
    cos_s, sin_s = _rope_tables(pos_s)
    gid = np.arange(MXU_COLS) // HEAD_DIM
    bd = jnp.asarray((gid[:, None] == gid[None, :]).astype(np.float32) / HEAD_DIM, BF16)

    n_front_b = B_LEFT_CHUNKS * CHUNK
    rel = (jnp.arange(n_front_b + CHUNK, dtype=jnp.int32)[None, :] - n_front_b
           - jnp.arange(CHUNK, dtype=jnp.int32)[:, None])
    ridx = jnp.clip(rel, -B_MAX_REL, B_MAX_REL) + B_MAX_REL

    tq_p = 128 if seq % 128 == 0 else CHUNK
    tk_p = 512 if seq % 512 == 0 else (256 if seq % 256 == 0 else 128)
    tk_s = 256
    l_s = past + t_new
    l_s_pad = -(-l_s // tk_s) * tk_s
    ch_p = 4 if (seq // CHUNK) % 4 == 0 else 1

    outs = {k: [] for k in ("pa_k", "pa_v", "pa_i", "pb_k", "pb_v", "pc_k", "pc_v",
                            "sa_k", "sa_v", "sa_i", "sb_k", "sb_v", "sc_k", "sc_v")}

    def last_rows(x, keep, width):
        return x.reshape(nb, seq, width)[:, seq - keep:]

    for layer in range(depth):
        li = layer // 2
        g_mix = norm_mix[layer][None, :]
        g_ffn = norm_ffn[layer][None, :]
        wgu = w_ffn_in[layer].astype(BF16)
        wdn = w_ffn_out[layer].astype(BF16)
        if layer % 2 == 0:
            w, gains = _even_weights(w_in_even[li], a_q_norm[li], a_k_norm[li], idx_k_norm[li],
                                     b_q_norm[li], b_k_norm[li])
            wo = w_out_even[li]
            wo_a = _perm_heads_rows(wo[:A_HEADS * HEAD_DIM], A_PERM).astype(BF16)
            wo_b = wo[A_HEADS * HEAD_DIM:].astype(BF16)
            bias = b_rel_bias[li].astype(F32)[:, ridx]

            (aq, iq, bq, bk, bv, bk16, bv16, ak, av, ak16, av16, ik, ik16, iw) = _even_in(
                hp, g_mix, w, gains, bd, cos_p, sin_p)
            out_a = _dsa(aq, iq, iw, ik16, ak16, av16, n_batch=nb, t=seq, row_block0=0, tq=tq_p, tk=tk_p,
                         q_off=0, l_valid=seq, topk=topk_p, keys_3d=False)
            kb = _pad_rows(bk16.reshape(nb, seq, 512), n_front_b, True)
            vb = _pad_rows(bv16.reshape(nb, seq, 512), n_front_b, True)
            out_b = _band_b(bq, kb, vb, bias, n_batch=nb, t=seq, row_block0=0, n_chunks=ch_p, front_valid=False)
            hp = _out_ffn(hp, [out_a, out_b], [wo_a, wo_b], g_ffn, wgu, wdn)
            outs["pa_k"].append(ak.reshape(nb, seq, A_KV_HEADS, HEAD_DIM))
            outs["pa_v"].append(av.reshape(nb, seq, A_KV_HEADS, HEAD_DIM))
            outs["pa_i"].append(ik.reshape(nb, seq, LANES)[:, :, :IDX_DIM])
            outs["pb_k"].append(last_rows(bk, keep_b, 512).reshape(nb, keep_b, B_HEADS, HEAD_DIM))
            outs["pb_v"].append(last_rows(bv, keep_b, 512).reshape(nb, keep_b, B_HEADS, HEAD_DIM))

            (aq, iq, bq, bk, bv, bk16, bv16, ak, av, ak16, av16, ik, ik16, iw) = _even_in(
                hs, g_mix, w, gains, bd, cos_s, sin_s)
            ci = cache_a_kidx[li].astype(BF16)
            ik_all = jnp.concatenate([jnp.concatenate([ci, ci], axis=-1), ik16.reshape(ns, t_new, LANES)], axis=1)
            ak_all = jnp.concatenate([cache_a_k[li].reshape(ns, past, LANES).astype(BF16),
                                      ak16.reshape(ns, t_new, LANES)], axis=1)
            av_all = jnp.concatenate([cache_a_v[li].reshape(ns, past, LANES).astype(BF16),
                                      av16.reshape(ns, t_new, LANES)], axis=1)
            ik_all, ak_all, av_all = (_pad_rows(x, l_s_pad - l_s, False) for x in (ik_all, ak_all, av_all))
            out_a = _dsa(aq, iq, iw, ik_all, ak_all, av_all, n_batch=ns, t=t_new, row_block0=0, tq=CHUNK, tk=tk_s,
                         q_off=past, l_valid=l_s, topk=topk_s, keys_3d=True)
            kb = jnp.concatenate([cache_b_k[li].reshape(ns, n_front_b, 512).astype(BF16),
                                  bk16.reshape(ns, t_new, 512)], axis=1)
            vb = jnp.concatenate([cache_b_v[li].reshape(ns, n_front_b, 512).astype(BF16),
                                  bv16.reshape(ns, t_new, 512)], axis=1)
            out_b = _band_b(bq, kb, vb, bias, n_batch=ns, t=t_new, row_block0=0, n_chunks=1, front_valid=True)
            hs = _out_ffn(hs, [out_a, out_b], [wo_a, wo_b], g_ffn, wgu, wdn)
            outs["sa_k"].append(ak.reshape(ns, t_new, A_KV_HEADS, HEAD_DIM))
            outs["sa_v"].append(av.reshape(ns, t_new, A_KV_HEADS, HEAD_DIM))
            outs["sa_i"].append(ik.reshape(ns, t_new, LANES)[:, :, :IDX_DIM])
            outs["sb_k"].append(bk.reshape(ns, t_new, B_HEADS, HEAD_DIM))
            outs["sb_v"].append(bv.reshape(ns, t_new, B_HEADS, HEAD_DIM))
        else:
            w, gains = _odd_weights(w_in_odd[li], c_q_norm[li], c_k_norm[li])
            wo = _perm_heads_rows(w_out_odd[li], C_PERM).astype(BF16)
            sinks = c_sinks[li].astype(F32)[jnp.asarray(C_PERM)]
            n_front_c = C_LEFT_CHUNKS * CHUNK

            q, k, v, k16, v16 = _odd_in(hp, g_mix, w, gains, bd, cos_p, sin_p)
            kc = _pad_rows(k16.reshape(nb, seq, LANES), n_front_c, True)
            vc = _pad_rows(v16.reshape(nb, seq, LANES), n_front_c, True)
            out_c = _band_c(sinks, q, kc, vc, n_batch=nb, t=seq, row_block0=0, n_chunks=ch_p, front_valid=False)
            hp = _out_ffn(hp, [out_c], [wo], g_ffn, wgu, wdn)
            outs["pc_k"].append(last_rows(k, keep_c, LANES).reshape(nb, keep_c, C_KV_HEADS, HEAD_DIM))
            outs["pc_v"].append(last_rows(v, keep_c, LANES).reshape(nb, keep_c, C_KV_HEADS, HEAD_DIM))

            q, k, v, k16, v16 = _odd_in(hs, g_mix, w, gains, bd, cos_s, sin_s)
            kc = jnp.concatenate([cache_c_k[li].reshape(ns, n_front_c, LANES).astype(BF16),
                                  k16.reshape(ns, t_new, LANES)], axis=1)
            vc = jnp.concatenate([cache_c_v[li].reshape(ns, n_front_c, LANES).astype(BF16),
                                  v16.reshape(ns, t_new, LANES)], axis=1)
            out_c = _band_c(sinks, q, kc, vc, n_batch=ns, t=t_new, row_block0=0, n_chunks=1, front_valid=True)
            hs = _out_ffn(hs, [out_c], [wo], g_ffn, wgu, wdn)
            outs["sc_k"].append(k.reshape(ns, t_new, C_KV_HEADS, HEAD_DIM))
            outs["sc_v"].append(v.reshape(ns, t_new, C_KV_HEADS, HEAD_DIM))

    st = lambda name: jnp.stack(outs[name])
    return (hp.reshape(nb, seq, d), hs.reshape(ns, t_new, d),
            st("pa_k"), st("pa_v"), st("pa_i"), st("pb_k"), st("pb_v"), st("pc_k"), st("pc_v"),
            st("sa_k"), st("sa_v"), st("sa_i"), st("sb_k"), st("sb_v"), st("sc_k"), st("sc_v"))
```

```python
import functools
import math

import numpy as np
import jax
import jax.numpy as jnp
from jax import lax
from jax.experimental import pallas as pl
from jax.experimental.pallas import tpu as pltpu

CHUNK = 64
HEAD_DIM = 64
EPS = 1e-6
ROPE_THETA = 10000.0
A_HEADS = 8
A_KV_HEADS = 2
IDX_HEADS = 8
IDX_DIM = 64
IDX_W_SCALE = (IDX_HEADS * IDX_DIM) ** -0.5
TOPK_MAX = 256
B_HEADS = 8
B_LEFT_CHUNKS = 8
B_MAX_REL = 128
C_HEADS = 16
C_KV_HEADS = 2
C_LEFT_CHUNKS = 2
QK_SCALE = HEAD_DIM ** -0.5
QK_SCALE_LOG2 = QK_SCALE * math.log2(math.e)

LANES = 128
SUBLANES = 8
BF16_ROWS = 16
MXU_COLS = 256
VMEM_LIMIT = 56 * 1024 * 1024

NEG = -1e30
INT_MIN = np.int32(-2 ** 31)
F32 = jnp.float32
BF16 = jnp.bfloat16
V_ROWS = 2 * HEAD_DIM + BF16_ROWS

A_PERM = tuple(c + (A_HEADS // 2) * p for c in range(A_HEADS // 2) for p in range(2))
C_PERM = tuple(c + (C_HEADS // 2) * p for c in range(C_HEADS // 2) for p in range(2))


def _dot(a, b):
    return jnp.dot(a, b, preferred_element_type=F32)


def _dot_nt(a, b):
    return lax.dot_general(a, b, (((1,), (1,)), ((), ())), preferred_element_type=F32)


def _row_tile(n):
    for t in (512, 256, 128, 64):
        if n % t == 0:
            return t
    raise ValueError(f"row count {n} is not a multiple of {CHUNK}")


def _const_spec(shape):
    nd = len(shape)
    return pl.BlockSpec(shape, lambda *_: (0,) * nd)


def _params(n_axes):
    return pltpu.CompilerParams(dimension_semantics=("arbitrary",) * n_axes,
                                vmem_limit_bytes=VMEM_LIMIT)


def _group_ms(hb, bd):
    sq = (hb * hb).astype(BF16)
    w = hb.shape[1]
    parts = [_dot(sq[:, i:i + MXU_COLS], bd) for i in range(0, w, MXU_COLS)]
    return parts[0] if len(parts) == 1 else jnp.concatenate(parts, axis=1)


def _rope_blocks(y, cos, sin, first_half):
    out = []
    for i in range(0, y.shape[1], LANES):
        yb = y[:, i:i + LANES]
        sw = jnp.where(first_half, pltpu.roll(yb, LANES - 32, 1), pltpu.roll(yb, 32, 1))
        out.append(yb * cos + sw * sin)
    return out[0] if len(out) == 1 else jnp.concatenate(out, axis=1)


def _normed_input(x_ref, g_ref):
    x = x_ref[...]
    ms = jnp.mean(x * x, axis=-1, keepdims=True)
    return (x * lax.rsqrt(ms + EPS) * g_ref[...]).astype(BF16)


def _even_in_kernel(x_ref, g_ref, w_ref, gain_ref, bd_ref, cos_ref, sin_ref,
                    aq_ref, iq_ref, bq_ref, bk_ref, bv_ref, bk16_ref, bv16_ref,
                    ak_ref, av_ref, ak16_ref, av16_ref, ik_ref, ik16_ref, iw_ref):
    tm = x_ref.shape[0]
    xn = _normed_input(x_ref, g_ref)
    bd = bd_ref[...]
    cos = cos_ref[...]
    sin = sin_ref[...]
    first_half = (lax.broadcasted_iota(jnp.int32, (tm, LANES), 1) % HEAD_DIM) < HEAD_DIM // 2

    def proj(c0, width):
        return _dot(xn, w_ref[:, c0:c0 + width])

    def normed(h, c0):
        return h * lax.rsqrt(_group_ms(h, bd) + EPS) * gain_ref[:, c0:c0 + h.shape[1]]

    h = proj(0, 512)
    aq_ref[...] = (_rope_blocks(normed(h, 0), cos, sin, first_half) * QK_SCALE_LOG2).astype(BF16)
    h = proj(512, 512)
    iq_ref[...] = _rope_blocks(h, cos, sin, first_half).astype(BF16)
    h = proj(1024, 512)
    bq_ref[...] = (normed(h, 1024) * QK_SCALE).astype(BF16)
    h = normed(proj(1536, 512), 1536)
    bk_ref[...] = h
    bk16_ref[...] = h.astype(BF16)
    h = proj(2048, 512)
    bv_ref[...] = h
    bv16_ref[...] = h.astype(BF16)
    h = proj(2560, 256)
    k = _rope_blocks(normed(h, 2560)[:, :LANES], cos, sin, first_half)
    ak_ref[...] = k
    ak16_ref[...] = k.astype(BF16)
    v = h[:, LANES:]
    av_ref[...] = v
    av16_ref[...] = v.astype(BF16)
    h = proj(2816, 256)
    k = _rope_blocks(normed(h, 2816)[:, :LANES], cos, sin, first_half)
    ik_ref[...] = k
    ik16_ref[...] = k.astype(BF16)
    iw_ref[...] = h[:, LANES:] * IDX_W_SCALE


def _in_proj_call(kern, name, widths_dtypes, x, g, w, gains, bd, cos, sin):
    n, d = x.shape
    tm = _row_tile(n)
    assert cos.shape[0] % tm == 0
    n_tab = cos.shape[0] // tm
    row = lambda width: pl.BlockSpec((tm, width), lambda i: (i, 0))
    tab = pl.BlockSpec((tm, LANES), lambda i: (i % n_tab, 0))
    return pl.pallas_call(
        kern,
        grid=(n // tm,),
        in_specs=[row(d), _const_spec((1, d)), _const_spec(w.shape), _const_spec(gains.shape),
                  _const_spec(bd.shape), tab, tab],
        out_specs=[row(wd) for wd, _ in widths_dtypes],
        out_shape=[jax.ShapeDtypeStruct((n, wd), dt) for wd, dt in widths_dtypes],
        compiler_params=_params(1),
        name=name,
    )(x, g, w, gains, bd, cos, sin)


def _even_in(x, g, w, gains, bd, cos, sin):
    widths_dtypes = [(512, BF16), (512, BF16), (512, BF16), (512, F32), (512, F32), (512, BF16), (512, BF16),
                     (128, F32), (128, F32), (128, BF16), (128, BF16), (128, F32), (128, BF16), (128, F32)]
    return _in_proj_call(_even_in_kernel, "even_in_proj", widths_dtypes, x, g, w, gains, bd, cos, sin)


def _odd_in_kernel(x_ref, g_ref, w_ref, gain_ref, bd_ref, cos_ref, sin_ref,
                   q_ref, k_ref, v_ref, k16_ref, v16_ref):
    tm = x_ref.shape[0]
    xn = _normed_input(x_ref, g_ref)
    bd = bd_ref[...]
    cos = cos_ref[...]
    sin = sin_ref[...]
    first_half = (lax.broadcasted_iota(jnp.int32, (tm, LANES), 1) % HEAD_DIM) < HEAD_DIM // 2
    for c0 in (0, 512):
        h = _dot(xn, w_ref[:, c0:c0 + 512])
        h = h * lax.rsqrt(_group_ms(h, bd) + EPS) * gain_ref[:, c0:c0 + 512]
        q_ref[:, c0:c0 + 512] = (_rope_blocks(h, cos, sin, first_half) * QK_SCALE).astype(BF16)
    h = _dot(xn, w_ref[:, 1024:1280])
    hn = h * lax.rsqrt(_group_ms(h, bd) + EPS) * gain_ref[:, 1024:1280]
    k = _rope_blocks(hn[:, :LANES], cos, sin, first_half)
    k_ref[...] = k
    k16_ref[...] = k.astype(BF16)
    v = h[:, LANES:]
    v_ref[...] = v
    v16_ref[...] = v.astype(BF16)


def _odd_in(x, g, w, gains, bd, cos, sin):
    widths_dtypes = [(1024, BF16), (128, F32), (128, F32), (128, BF16), (128, BF16)]
    return _in_proj_call(_odd_in_kernel, "odd_in_proj", widths_dtypes, x, g, w, gains, bd, cos, sin)


def _out_ffn_kernel(*refs, n_attn, d_ff, ff_chunk):
    h_ref = refs[0]
    attn_refs = refs[1:1 + n_attn]
    wo_refs = refs[1 + n_attn:1 + 2 * n_attn]
    g_ref, wgu_ref, wdn_ref, o_ref, yn_ref = refs[1 + 2 * n_attn:]
    o_ref[...] = h_ref[...]
    for a_ref, wo_ref in zip(attn_refs, wo_refs):
        o_ref[...] += _dot(a_ref[...], wo_ref[...])
    y = o_ref[...]
    ms = jnp.mean(y * y, axis=-1, keepdims=True)
    yn_ref[...] = (y * lax.rsqrt(ms + EPS) * g_ref[...]).astype(BF16)
    for c0 in range(0, d_ff, ff_chunk):
        gate = _dot(yn_ref[...], wgu_ref[:, c0:c0 + ff_chunk])
        up = _dot(yn_ref[...], wgu_ref[:, d_ff + c0:d_ff + c0 + ff_chunk])
        act = (gate * (1.0 / (1.0 + jnp.exp(-gate))) * up).astype(BF16)
        o_ref[...] += _dot(act, wdn_ref[c0:c0 + ff_chunk, :])


def _out_ffn(h, attns, wos, g, wgu, wdn):
    n, d = h.shape
    tm = _row_tile(n)
    d_ff = wdn.shape[0]
    row = lambda width: pl.BlockSpec((tm, width), lambda i: (i, 0))
    kern = functools.partial(_out_ffn_kernel, n_attn=len(attns), d_ff=d_ff, ff_chunk=MXU_COLS)
    return pl.pallas_call(
        kern,
        grid=(n // tm,),
        in_specs=[row(d)] + [row(a.shape[1]) for a in attns] + [_const_spec(w.shape) for w in wos]
                 + [_const_spec((1, d)), _const_spec(wgu.shape), _const_spec(wdn.shape)],
        out_specs=row(d),
        out_shape=jax.ShapeDtypeStruct((n, d), F32),
        scratch_shapes=[pltpu.VMEM((tm, d), BF16)],
        compiler_params=_params(1),
        name="out_proj_ffn",
    )(h, *attns, *wos, g, wgu, wdn)


def _dsa_kernel(aq_ref, iq_ref, iw_ref, ik_ref, ak_ref, vt_ref, o_ref,
                iqs_ref, aqs_ref, w_ref, keys_ref, s_ref, acc_ref,
                *, tk, q_off, l_valid, topk, idx_bits):
    tq = LANES
    n_slots = IDX_HEADS
    n_chains = 4
    j = pl.program_id(1)
    qpos0 = q_off + j * tq
    n_adm_max = jnp.minimum(((qpos0 + tq - 1) // CHUNK + 1) * CHUNK, l_valid)
    nkt = (n_adm_max + tk - 1) // tk
    lane = lax.broadcasted_iota(jnp.int32, (tq, LANES), 1)
    half = (lane < HEAD_DIM, lane >= HEAD_DIM)

    for c in range(n_slots // 2):
        iqb = iq_ref[:, c * LANES:(c + 1) * LANES]
        aqb = aq_ref[:, c * LANES:(c + 1) * LANES]
        for p in range(2):
            n = 2 * c + p
            iqs_ref[n * tq:(n + 1) * tq, :] = jnp.where(half[p], iqb, jnp.zeros_like(iqb))
            aqs_ref[n * tq:(n + 1) * tq, :] = jnp.where(half[p], aqb, jnp.zeros_like(aqb))
    w_ref[...] = iw_ref[...].T

    qcol = qpos0 + lax.broadcasted_iota(jnp.int32, (1, tq), 1)
    klim = jnp.minimum((qcol // CHUNK + 1) * CHUNK, l_valid)
    krow = lax.broadcasted_iota(jnp.int32, (tk, tq), 0)

    def score_tile(kt, carry):
        k0 = pl.multiple_of(kt * tk, tk)
        ikt = ik_ref[pl.ds(k0, tk), :]
        score = jnp.zeros((tk, tq), F32)
        for c in range(n_slots // 2):
            s = _dot_nt(ikt, iqs_ref[2 * c * tq:2 * (c + 1) * tq, :])
            for p in range(2):
                score = score + jnp.maximum(s[:, p * tq:(p + 1) * tq], 0.0) * w_ref[2 * c + p:2 * c + p + 1, :]
        bits = lax.bitcast_convert_type(score, jnp.int32)
        key = bits ^ ((bits >> 31) & jnp.int32(0x7FFFFFFF))
        key = jnp.where(key == -1, 0, key)
        keys_ref[kt] = jnp.where(krow + k0 < klim, key, INT_MIN)
        return carry

    lax.fori_loop(0, nkt, score_tile, 0)

    def count(indicator):
        def body(kt, acc):
            ind = indicator(keys_ref[kt], kt)
            return acc + jnp.sum(ind.reshape(tk // (n_chains * SUBLANES), n_chains, SUBLANES, tq), axis=0)
        acc = lax.fori_loop(0, nkt, body, jnp.zeros((n_chains, SUBLANES, tq), F32))
        return jnp.sum(jnp.sum(acc, axis=0), axis=0, keepdims=True)

    def search_bit(i, ans):
        cand = ans | jnp.left_shift(jnp.int32(1), 31 - i)
        cand_s = cand ^ INT_MIN
        cnt = count(lambda k, kt: jnp.where(k >= cand_s, 1.0, 0.0))
        return jnp.where(cnt >= topk, cand, ans)

    ans = lax.fori_loop(0, 32, search_bit, jnp.zeros((1, tq), jnp.int32))
    vstar = ans ^ INT_MIN
    c_gt = count(lambda k, kt: jnp.where(k > vstar, 1.0, 0.0))
    c_ge = count(lambda k, kt: jnp.where(k >= vstar, 1.0, 0.0))
    need = topk - c_gt
    has_tie = jnp.where(c_ge > topk, jnp.where(vstar > INT_MIN, 1.0, 0.0), 0.0)
    idx_all = jnp.int32(2 ** idx_bits - 1)

    def tie_search():
        def tie_bit(i, jmax):
            cand = jmax | jnp.left_shift(jnp.int32(1), idx_bits - 1 - i)
            cnt = count(lambda k, kt: jnp.where(k == vstar, jnp.where(krow + kt * tk < cand, 1.0, 0.0), 0.0))
            return jnp.where(cnt <= need, cand, jmax)
        return lax.fori_loop(0, idx_bits, tie_bit, jnp.zeros((1, tq), jnp.int32))

    jmax = lax.cond(jnp.max(has_tie) > 0.0, tie_search, lambda: jnp.full((1, tq), idx_all, jnp.int32))
    jmax = jnp.where(vstar == INT_MIN, 0, jmax)

    acc_ref[...] = jnp.zeros(acc_ref.shape, F32)
    n_pairs = n_slots // 2

    def attend_tile(kt, m_run):
        k0 = pl.multiple_of(kt * tk, tk)
        key = keys_ref[kt]
        tie_ok = jnp.where(krow + k0 < jmax, 0.0, NEG)
        mb = jnp.where(key > vstar, 0.0, jnp.where(key == vstar, tie_ok, NEG))
        mb2 = jnp.concatenate([mb, mb], axis=1)
        akt = ak_ref[pl.ds(k0, tk), :]
        vt = vt_ref[kt]
        m_next = []
        for c in range(n_pairs):
            s = _dot_nt(akt, aqs_ref[2 * c * tq:2 * (c + 1) * tq, :]) + mb2
            s_ref[c] = s
            m_next.append(jnp.maximum(m_run[c], jnp.max(s, axis=0, keepdims=True)))
        for c in range(n_pairs):
            alpha = jnp.exp2(m_run[c] - m_next[c])
            pr = jnp.exp2(s_ref[c] - m_next[c])
            acc_ref[c] = alpha * acc_ref[c] + _dot(vt, pr.astype(BF16))
        return tuple(m_next)

    lax.fori_loop(0, nkt, attend_tile, tuple(jnp.full((1, 2 * tq), NEG, F32) for _ in range(n_pairs)))

    group0 = lax.broadcasted_iota(jnp.int32, (2 * HEAD_DIM, tq), 0) < HEAD_DIM
    for c in range(n_slots // 2):
        a = acc_ref[c]
        o = a[:2 * HEAD_DIM] / a[2 * HEAD_DIM:2 * HEAD_DIM + 1]
        o_ref[:, c * LANES:(c + 1) * LANES] = jnp.where(group0, o[:, :tq], o[:, tq:]).T.astype(BF16)


def _dsa(aq, iq, iw, ik, ak, vt, *, n_batch, nq, row_block0, tk, q_off, l_valid, topk, keys_3d):
    tq = LANES
    nkt_max = vt.shape[1]
    l_pad = nkt_max * tk
    assert l_valid <= l_pad and vt.shape[2:] == (V_ROWS, tk)
    qspec = lambda width: pl.BlockSpec((tq, width), lambda b, j: (row_block0 + b * nq + j, 0))
    if keys_3d:
        kspec = pl.BlockSpec((None, l_pad, LANES), lambda b, j: (b, 0, 0))
    else:
        kspec = pl.BlockSpec((l_pad, LANES), lambda b, j: (b, 0))
    kern = functools.partial(_dsa_kernel, tk=tk, q_off=q_off, l_valid=l_valid, topk=topk,
                             idx_bits=int(l_pad).bit_length())
    n_slots = IDX_HEADS
    return pl.pallas_call(
        kern,
        grid=(n_batch, nq),
        in_specs=[qspec(512), qspec(512), qspec(LANES), kspec, kspec,
                  pl.BlockSpec((None, nkt_max, V_ROWS, tk), lambda b, j: (b, 0, 0, 0))],
        out_specs=pl.BlockSpec((tq, 512), lambda b, j: (b * nq + j, 0)),
        out_shape=jax.ShapeDtypeStruct((n_batch * nq * tq, 512), BF16),
        scratch_shapes=[
            pltpu.VMEM((n_slots * tq, LANES), BF16),
            pltpu.VMEM((n_slots * tq, LANES), BF16),
            pltpu.VMEM((LANES, tq), F32),
            pltpu.VMEM((nkt_max, tk, tq), jnp.int32),
            pltpu.VMEM((n_slots // 2, tk, 2 * tq), F32),
            pltpu.VMEM((n_slots // 2, V_ROWS, 2 * tq), F32),
        ],
        compiler_params=_params(2),
        name="dsa_attention",
    )(aq, iq, iw, ik, ak, vt)


def _band_b_kernel(q_ref, k_ref, v_ref, bias_ref, o_ref, *, n_chunks, n_front, front_valid):
    t = pl.program_id(1)
    bw = n_front + CHUNK
    hq = MXU_COLS // HEAD_DIM
    lane = lax.broadcasted_iota(jnp.int32, (CHUNK, MXU_COLS), 1) // HEAD_DIM
    ucol = lax.broadcasted_iota(jnp.int32, (1, 1, bw), 2)
    for ci in range(n_chunks):
        c = t * n_chunks + ci
        r0 = pl.multiple_of(c * CHUNK, CHUNK)
        for cq in range(B_HEADS // hq):
            cols = slice(cq * MXU_COLS, (cq + 1) * MXU_COLS)
            qq = q_ref[ci * CHUNK:(ci + 1) * CHUNK, cols]
            kq = k_ref[pl.ds(r0, bw), cols]
            vq = v_ref[pl.ds(r0, bw), cols]
            lhs = jnp.concatenate([jnp.where(lane == e, qq, jnp.zeros_like(qq)) for e in range(hq)], axis=0)
            s = _dot_nt(lhs, kq).reshape(hq, CHUNK, bw) + bias_ref[cq * hq:(cq + 1) * hq]
            if not front_valid:
                s = jnp.where(ucol + c * CHUNK >= n_front, s, NEG)
            m = jnp.max(s, axis=-1, keepdims=True)
            e_ = jnp.exp(s - m)
            den = jnp.sum(e_, axis=-1, keepdims=True)
            o = _dot(e_.reshape(hq * CHUNK, bw).astype(BF16), vq).reshape(hq, CHUNK, MXU_COLS) / den
            out = jnp.where(lane == 0, o[0], 0.0)
            for e in range(1, hq):
                out = jnp.where(lane == e, o[e], out)
            o_ref[ci * CHUNK:(ci + 1) * CHUNK, cols] = out.astype(BF16)


def _band_b(q, k, v, bias, *, n_batch, t, row_block0, n_chunks, front_valid):
    rows = n_chunks * CHUNK
    nq = t // rows
    n_front = B_LEFT_CHUNKS * CHUNK
    kvspec = pl.BlockSpec((None, n_front + t, 512), lambda b, j: (b, 0, 0))
    kern = functools.partial(_band_b_kernel, n_chunks=n_chunks, n_front=n_front, front_valid=front_valid)
    return pl.pallas_call(
        kern,
        grid=(n_batch, nq),
        in_specs=[pl.BlockSpec((rows, 512), lambda b, j: (row_block0 + b * nq + j, 0)), kvspec, kvspec,
                  _const_spec(bias.shape)],
        out_specs=pl.BlockSpec((rows, 512), lambda b, j: (b * nq + j, 0)),
        out_shape=jax.ShapeDtypeStruct((n_batch * t, 512), BF16),
        compiler_params=_params(2),
        name="band_b_attention",
    )(q, k, v, bias)


def _band_c_kernel(q_ref, k_ref, v_ref, base_ref, tail_ref, o_ref, *, n_chunks, n_front, front_valid):
    t = pl.program_id(1)
    bw = n_front + CHUNK
    n_slots = C_HEADS
    lane = lax.broadcasted_iota(jnp.int32, (CHUNK, LANES), 1)
    half0 = lane < HEAD_DIM
    in_band = lax.broadcasted_iota(jnp.int32, (n_slots * CHUNK, MXU_COLS), 1) < bw
    ucol = lax.broadcasted_iota(jnp.int32, (1, MXU_COLS), 1)
    ktail = jnp.zeros((MXU_COLS - bw, LANES), BF16)
    for ci in range(n_chunks):
        c = t * n_chunks + ci
        r0 = pl.multiple_of(c * CHUNK, CHUNK)
        kb = jnp.concatenate([k_ref[pl.ds(r0, bw), :], ktail], axis=0)
        vb = jnp.concatenate([v_ref[pl.ds(r0, bw), :], tail_ref[...]], axis=0)
        parts = []
        for cb in range(n_slots // 2):
            qb = q_ref[ci * CHUNK:(ci + 1) * CHUNK, cb * LANES:(cb + 1) * LANES]
            parts.append(jnp.where(half0, qb, jnp.zeros_like(qb)))
            parts.append(jnp.where(half0, jnp.zeros_like(qb), qb))
        s = _dot_nt(jnp.concatenate(parts, axis=0), kb)
        if not front_valid:
            s = s + jnp.where(ucol + c * CHUNK >= n_front, 0.0, NEG)
        s = jnp.where(in_band, s, base_ref[...])
        m = jnp.max(s, axis=-1, keepdims=True)
        e_ = jnp.exp(s - m)
        o = _dot(e_.astype(BF16), vb)
        o = o[:, :LANES] / o[:, LANES:]
        for cb in range(n_slots // 2):
            o0 = o[(2 * cb) * CHUNK:(2 * cb + 1) * CHUNK]
            o1 = o[(2 * cb + 1) * CHUNK:(2 * cb + 2) * CHUNK]
            o_ref[ci * CHUNK:(ci + 1) * CHUNK, cb * LANES:(cb + 1) * LANES] = jnp.where(half0, o0, o1).astype(BF16)


def _band_c(q, k, v, base, tail, *, n_batch, t, row_block0, n_chunks, front_valid):
    rows = n_chunks * CHUNK
    nq = t // rows
    n_front = C_LEFT_CHUNKS * CHUNK
    kern = functools.partial(_band_c_kernel, n_chunks=n_chunks, n_front=n_front, front_valid=front_valid)
    return pl.pallas_call(
        kern,
        grid=(n_batch, nq),
        in_specs=[pl.BlockSpec((rows, 1024), lambda b, j: (row_block0 + b * nq + j, 0)),
                  pl.BlockSpec((None, n_front + t, LANES), lambda b, j: (b, 0, 0)),
                  pl.BlockSpec((None, n_front + t, MXU_COLS), lambda b, j: (b, 0, 0)),
                  _const_spec(base.shape), _const_spec(tail.shape)],
        out_specs=pl.BlockSpec((rows, 1024), lambda b, j: (b * nq + j, 0)),
        out_shape=jax.ShapeDtypeStruct((n_batch * t, 1024), BF16),
        compiler_params=_params(2),
        name="band_c_attention",
    )(q, k, v, base, tail)


def _rope_tables(pos):
    half = HEAD_DIM // 2
    inv_freq = ROPE_THETA ** (-jnp.arange(half, dtype=F32) / half)
    ang = pos.astype(F32)[:, None] * inv_freq[None, :]
    cos = jnp.cos(ang)
    sin = jnp.sin(ang)
    return (jnp.concatenate([cos, cos, cos, cos], axis=1), jnp.concatenate([-sin, sin, -sin, sin], axis=1))


def _perm_heads_cols(w, perm):
    d = w.shape[0]
    return w.reshape(d, len(perm), HEAD_DIM)[:, np.asarray(perm)].reshape(d, len(perm) * HEAD_DIM)


def _perm_heads_rows(w, perm):
    d = w.shape[1]
    return w.reshape(len(perm), HEAD_DIM, d)[np.asarray(perm)].reshape(len(perm) * HEAD_DIM, d)


def _even_weights(w_in, a_qn, a_kn, i_kn, b_qn, b_kn):
    d = w_in.shape[0]
    sizes = (A_HEADS * HEAD_DIM, A_KV_HEADS * HEAD_DIM, A_KV_HEADS * HEAD_DIM, IDX_HEADS * IDX_DIM, IDX_DIM,
             IDX_HEADS, B_HEADS * HEAD_DIM, B_HEADS * HEAD_DIM, B_HEADS * HEAD_DIM)
    aq, ak, av, iq, ik, iw, bq, bk, bv = jnp.split(w_in, np.cumsum(sizes)[:-1].tolist(), axis=1)
    w = jnp.concatenate([_perm_heads_cols(aq, A_PERM), iq, bq, bk, bv, ak, av, ik, ik,
                         iw, jnp.zeros((d, LANES - IDX_HEADS), w_in.dtype)], axis=1).astype(BF16)
    one = lambda n: jnp.ones((n,), F32)
    gains = jnp.concatenate([jnp.tile(a_qn, A_HEADS), one(512), jnp.tile(b_qn, B_HEADS), jnp.tile(b_kn, B_HEADS),
                             one(512), jnp.tile(a_kn, A_KV_HEADS), one(128), jnp.tile(i_kn, 2), one(128)])
    return w, gains[None, :].astype(F32)


def _odd_weights(w_in, c_qn, c_kn):
    q, k, v = jnp.split(w_in, [C_HEADS * HEAD_DIM, (C_HEADS + C_KV_HEADS) * HEAD_DIM], axis=1)
    w = jnp.concatenate([_perm_heads_cols(q, C_PERM), k, v], axis=1).astype(BF16)
    gains = jnp.concatenate([jnp.tile(c_qn, C_HEADS), jnp.tile(c_kn, C_KV_HEADS), jnp.ones((128,), F32)])
    return w, gains[None, :].astype(F32)


def _pad_rows(x, n, front):
    pad = [(0, 0)] * x.ndim
    pad[1] = (n, 0) if front else (0, n)
    return jnp.pad(x, pad)


def _values_transposed(v, tk):
    nbt, l_pad, _ = v.shape
    vt = jnp.swapaxes(v.reshape(nbt, l_pad // tk, tk, LANES), 2, 3)
    return jnp.concatenate([vt, jnp.ones((nbt, l_pad // tk, BF16_ROWS, tk), v.dtype)], axis=2)


def _band_b_bias(rb):
    n_front = B_LEFT_CHUNKS * CHUNK
    bw = n_front + CHUNK
    n_flat = n_front - B_MAX_REL + CHUNK
    ext = jnp.concatenate([jnp.broadcast_to(rb[:, :1], (rb.shape[0], n_flat)), rb[:, 1:B_MAX_REL + CHUNK]], axis=1)
    return jnp.stack([ext[:, CHUNK - 1 - t:CHUNK - 1 - t + bw] for t in range(CHUNK)], axis=1)


def kernel(x_prompt, x_sample, cache_a_k, cache_a_v, cache_a_kidx, cache_b_k, cache_b_v, cache_c_k, cache_c_v, norm_mix, norm_ffn, w_in_even, w_out_even, a_q_norm, a_k_norm, idx_k_norm, b_q_norm, b_k_norm, b_rel_bias, w_in_odd, w_out_odd, c_q_norm, c_k_norm, c_sinks, w_ffn_in, w_ffn_out):
    nb, seq, d = x_prompt.shape
    ns, t_new, _ = x_sample.shape
    past = cache_a_k.shape[2]
    depth = norm_mix.shape[0]
    assert seq % LANES == 0 and t_new == CHUNK and past % CHUNK == 0
    assert cache_b_k.shape[2] == B_LEFT_CHUNKS * CHUNK and cache_c_k.shape[2] == C_LEFT_CHUNKS * CHUNK
    topk_p = min(TOPK_MAX, seq // 4)
    topk_s = min(TOPK_MAX, (past + t_new) // 4)
    keep_b = min(B_LEFT_CHUNKS * CHUNK, seq)
    keep_c = min(C_LEFT_CHUNKS * CHUNK, seq)
    n_front_b = B_LEFT_CHUNKS * CHUNK
    n_front_c = C_LEFT_CHUNKS * CHUNK

    hp = x_prompt.reshape(nb * seq, d)
    hs = x_sample.reshape(ns * t_new, d)
    tm_p = _row_tile(nb * seq)
    tm_s = _row_tile(ns * t_new)
    pos_p = jnp.tile(jnp.arange(seq, dtype=jnp.int32), max(1, tm_p // seq))
    pos_s = jnp.tile(past + jnp.arange(t_new, dtype=jnp.int32), max(1, tm_s // t_new))
    cos_p, sin_p = _rope_tables(pos_p)
    cos_s, sin_s = _rope_tables(pos_s)
    gid = np.arange(MXU_COLS) // HEAD_DIM
    bd = jnp.asarray((gid[:, None] == gid[None, :]).astype(np.float32) / HEAD_DIM, BF16)

    tk_p = 512 if seq % 512 == 0 else (256 if seq % 256 == 0 else 128)
    tk_s = 256
    l_s = past + t_new
    l_s_pad = -(-l_s // tk_s) * tk_s
    ch_p = 4 if (seq // CHUNK) % 4 == 0 else 1
    c_tail = jnp.concatenate([jnp.zeros((MXU_COLS - n_front_c - CHUNK, LANES), BF16),
                              jnp.ones((MXU_COLS - n_front_c - CHUNK, LANES), BF16)], axis=1)

    outs = {k: [] for k in ("pa_k", "pa_v", "pa_i", "pb_k", "pb_v", "pc_k", "pc_v",
                            "sa_k", "sa_v", "sa_i", "sb_k", "sb_v", "sc_k", "sc_v")}

    def last_rows(x, keep, width):
        return x.reshape(nb, seq, width)[:, seq - keep:]

    def pad_queries(x):
        return _pad_rows(x.reshape(ns, t_new, x.shape[1]), LANES - t_new, False).reshape(ns * LANES, x.shape[1])

    for layer in range(depth):
        li = layer // 2
        g_mix = norm_mix[layer][None, :]
        g_ffn = norm_ffn[layer][None, :]
        wgu = w_ffn_in[layer].astype(BF16)
        wdn = w_ffn_out[layer].astype(BF16)
        if layer % 2 == 0:
            w, gains = _even_weights(w_in_even[li], a_q_norm[li], a_k_norm[li], idx_k_norm[li],
                                     b_q_norm[li], b_k_norm[li])
            wo = w_out_even[li]
            wo_a = _perm_heads_rows(wo[:A_HEADS * HEAD_DIM], A_PERM).astype(BF16)
            wo_b = wo[A_HEADS * HEAD_DIM:].astype(BF16)
            bias = _band_b_bias(b_rel_bias[li].astype(F32))

            (aq, iq, bq, bk, bv, bk16, bv16, ak, av, ak16, av16, ik, ik16, iw) = _even_in(
                hp, g_mix, w, gains, bd, cos_p, sin_p)
            vt = _values_transposed(av16.reshape(nb, seq, LANES), tk_p)
            out_a = _dsa(aq, iq, iw, ik16, ak16, vt, n_batch=nb, nq=seq // LANES, row_block0=0, tk=tk_p,
                         q_off=0, l_valid=seq, topk=topk_p, keys_3d=False)
            kb = _pad_rows(bk16.reshape(nb, seq, 512), n_front_b, True)
            vb = _pad_rows(bv16.reshape(nb, seq, 512), n_front_b, True)
            out_b = _band_b(bq, kb, vb, bias, n_batch=nb, t=seq, row_block0=0, n_chunks=ch_p, front_valid=False)
            hp = _out_ffn(hp, [out_a, out_b], [wo_a, wo_b], g_ffn, wgu, wdn)
            outs["pa_k"].append(ak.reshape(nb, seq, A_KV_HEADS, HEAD_DIM))
            outs["pa_v"].append(av.reshape(nb, seq, A_KV_HEADS, HEAD_DIM))
            outs["pa_i"].append(ik.reshape(nb, seq, LANES)[:, :, :IDX_DIM])
            outs["pb_k"].append(last_rows(bk, keep_b, 512).reshape(nb, keep_b, B_HEADS, HEAD_DIM))
            outs["pb_v"].append(last_rows(bv, keep_b, 512).reshape(nb, keep_b, B_HEADS, HEAD_DIM))

            (aq, iq, bq, bk, bv, bk16, bv16, ak, av, ak16, av16, ik, ik16, iw) = _even_in(
                hs, g_mix, w, gains, bd, cos_s, sin_s)
            ci = cache_a_kidx[li].astype(BF16)
            ik_all = jnp.concatenate([jnp.concatenate([ci, ci], axis=-1), ik16.reshape(ns, t_new, LANES)], axis=1)
            ak_all = jnp.concatenate([cache_a_k[li].reshape(ns, past, LANES).astype(BF16),
                                      ak16.reshape(ns, t_new, LANES)], axis=1)
            av_all = jnp.concatenate([cache_a_v[li].reshape(ns, past, LANES).astype(BF16),
                                      av16.reshape(ns, t_new, LANES)], axis=1)
            ik_all, ak_all, av_all = (_pad_rows(x, l_s_pad - l_s, False) for x in (ik_all, ak_all, av_all))
            out_a = _dsa(pad_queries(aq), pad_queries(iq), pad_queries(iw), ik_all, ak_all,
                         _values_transposed(av_all, tk_s), n_batch=ns, nq=1, row_block0=0, tk=tk_s,
                         q_off=past, l_valid=l_s, topk=topk_s, keys_3d=True)
            out_a = out_a.reshape(ns, LANES, 512)[:, :t_new].reshape(ns * t_new, 512)
            kb = jnp.concatenate([cache_b_k[li].reshape(ns, n_front_b, 512).astype(BF16),
                                  bk16.reshape(ns, t_new, 512)], axis=1)
            vb = jnp.concatenate([cache_b_v[li].reshape(ns, n_front_b, 512).astype(BF16),
                                  bv16.reshape(ns, t_new, 512)], axis=1)
            out_b = _band_b(bq, kb, vb, bias, n_batch=ns, t=t_new, row_block0=0, n_chunks=1, front_valid=True)
            hs = _out_ffn(hs, [out_a, out_b], [wo_a, wo_b], g_ffn, wgu, wdn)
            outs["sa_k"].append(ak.reshape(ns, t_new, A_KV_HEADS, HEAD_DIM))
            outs["sa_v"].append(av.reshape(ns, t_new, A_KV_HEADS, HEAD_DIM))
            outs["sa_i"].append(ik.reshape(ns, t_new, LANES)[:, :, :IDX_DIM])
            outs["sb_k"].append(bk.reshape(ns, t_new, B_HEADS, HEAD_DIM))
            outs["sb_v"].append(bv.reshape(ns, t_new, B_HEADS, HEAD_DIM))
        else:
            w, gains = _odd_weights(w_in_odd[li], c_q_norm[li], c_k_norm[li])
            wo = _perm_heads_rows(w_out_odd[li], C_PERM).astype(BF16)
            sinks = c_sinks[li].astype(F32)[np.asarray(C_PERM)]
            base = jnp.full((C_HEADS, 1, MXU_COLS), NEG, F32).at[:, 0, n_front_c + CHUNK].set(sinks)
            base = jnp.broadcast_to(base, (C_HEADS, CHUNK, MXU_COLS)).reshape(C_HEADS * CHUNK, MXU_COLS)
            ones_cols = lambda x: jnp.concatenate([x, jnp.ones_like(x)], axis=-1)

            q, k, v, k16, v16 = _odd_in(hp, g_mix, w, gains, bd, cos_p, sin_p)
            kc = _pad_rows(k16.reshape(nb, seq, LANES), n_front_c, True)
            vc = ones_cols(_pad_rows(v16.reshape(nb, seq, LANES), n_front_c, True))
            out_c = _band_c(q, kc, vc, base, c_tail, n_batch=nb, t=seq, row_block0=0, n_chunks=ch_p,
                            front_valid=False)
            hp = _out_ffn(hp, [out_c], [wo], g_ffn, wgu, wdn)
            outs["pc_k"].append(last_rows(k, keep_c, LANES).reshape(nb, keep_c, C_KV_HEADS, HEAD_DIM))
            outs["pc_v"].append(last_rows(v, keep_c, LANES).reshape(nb, keep_c, C_KV_HEADS, HEAD_DIM))

            q, k, v, k16, v16 = _odd_in(hs, g_mix, w, gains, bd, cos_s, sin_s)
            kc = jnp.concatenate([cache_c_k[li].reshape(ns, n_front_c, LANES).astype(BF16),
                                  k16.reshape(ns, t_new, LANES)], axis=1)
            vc = ones_cols(jnp.concatenate([cache_c_v[li].reshape(ns, n_front_c, LANES).astype(BF16),
                                            v16.reshape(ns, t_new, LANES)], axis=1))
            out_c = _band_c(q, kc, vc, base, c_tail, n_batch=ns, t=t_new, row_block0=0, n_chunks=1,
                            front_valid=True)
            hs = _out_ffn(hs, [out_c], [wo], g_ffn, wgu, wdn)
            outs["sc_k"].append(k.reshape(ns, t_new, C_KV_HEADS, HEAD_DIM))
            outs["sc_v"].append(v.reshape(ns, t_new, C_KV_HEADS, HEAD_DIM))

    st = lambda name: jnp.stack(outs[name])
    return (hp.reshape(nb, seq, d), hs.reshape(ns, t_new, d),
            st("pa_k"), st("pa_v"), st("pa_i"), st("pb_k"), st("pb_v"), st("pc_k"), st("pc_v"),
            st("sa_k"), st("sa_v"), st("sa_i"), st("sb_k"), st("sb_v"), st("sc_k"), st("sc_v"))
```

```python
import functools
import math

import numpy as np
import jax
import jax.numpy as jnp
from jax import lax
from jax.experimental import pallas as pl
from jax.experimental.pallas import tpu as pltpu

CHUNK = 64
HEAD_DIM = 64
EPS = 1e-6
ROPE_THETA = 10000.0
A_HEADS = 8
A_KV_HEADS = 2
IDX_HEADS = 8
IDX_DIM = 64
IDX_W_SCALE = (IDX_HEADS * IDX_DIM) ** -0.5
TOPK_MAX = 256
B_HEADS = 8
B_LEFT_CHUNKS = 8
B_MAX_REL = 128
C_HEADS = 16
C_KV_HEADS = 2
C_LEFT_CHUNKS = 2
QK_SCALE = HEAD_DIM ** -0.5
QK_SCALE_LOG2 = QK_SCALE * math.log2(math.e)

LANES = 128
SUBLANES = 8
BF16_ROWS = 16
MXU_COLS = 256
VMEM_LIMIT = 56 * 1024 * 1024

NEG = -1e30
INT_MIN = np.int32(-2 ** 31)
F32 = jnp.float32
BF16 = jnp.bfloat16
V_ROWS = 2 * HEAD_DIM + BF16_ROWS

A_PERM = tuple(c + (A_HEADS // 2) * p for c in range(A_HEADS // 2) for p in range(2))
C_PERM = tuple(c + (C_HEADS // 2) * p for c in range(C_HEADS // 2) for p in range(2))


def _dot(a, b):
    return jnp.dot(a, b, preferred_element_type=F32)


def _dot_nt(a, b):
    return lax.dot_general(a, b, (((1,), (1,)), ((), ())), preferred_element_type=F32)


def _row_tile(n):
    for t in (512, 256, 128, 64):
        if n % t == 0:
            return t
    raise ValueError(f"row count {n} is not a multiple of {CHUNK}")


def _const_spec(shape):
    nd = len(shape)
    return pl.BlockSpec(shape, lambda *_: (0,) * nd)


def _params(n_axes):
    return pltpu.CompilerParams(dimension_semantics=("arbitrary",) * n_axes,
                                vmem_limit_bytes=VMEM_LIMIT)


def _group_ms(hb, bd):
    sq = (hb * hb).astype(BF16)
    w = hb.shape[1]
    parts = [_dot(sq[:, i:i + MXU_COLS], bd) for i in range(0, w, MXU_COLS)]
    return parts[0] if len(parts) == 1 else jnp.concatenate(parts, axis=1)


def _rope_blocks(y, cos, sin, first_half):
    out = []
    for i in range(0, y.shape[1], LANES):
        yb = y[:, i:i + LANES]
        sw = jnp.where(first_half, pltpu.roll(yb, LANES - 32, 1), pltpu.roll(yb, 32, 1))
        out.append(yb * cos + sw * sin)
    return out[0] if len(out) == 1 else jnp.concatenate(out, axis=1)


def _normed_input(x_ref, g_ref):
    x = x_ref[...]
    ms = jnp.mean(x * x, axis=-1, keepdims=True)
    return (x * lax.rsqrt(ms + EPS) * g_ref[...]).astype(BF16)


def _even_in_kernel(x_ref, g_ref, w_ref, gain_ref, bd_ref, cos_ref, sin_ref,
                    aq_ref, iq_ref, bq_ref, bk_ref, bv_ref, bk16_ref, bv16_ref,
                    ak_ref, av_ref, ak16_ref, av16_ref, ik_ref, ik16_ref, iw_ref):
    tm = x_ref.shape[0]
    xn = _normed_input(x_ref, g_ref)
    bd = bd_ref[...]
    cos = cos_ref[...]
    sin = sin_ref[...]
    first_half = (lax.broadcasted_iota(jnp.int32, (tm, LANES), 1) % HEAD_DIM) < HEAD_DIM // 2

    def proj(c0, width):
        return _dot(xn, w_ref[:, c0:c0 + width])

    def normed(h, c0):
        return h * lax.rsqrt(_group_ms(h, bd) + EPS) * gain_ref[:, c0:c0 + h.shape[1]]

    h = proj(0, 512)
    aq_ref[...] = (_rope_blocks(normed(h, 0), cos, sin, first_half) * QK_SCALE_LOG2).astype(BF16)
    h = proj(512, 512)
    iq_ref[...] = _rope_blocks(h, cos, sin, first_half).astype(BF16)
    h = proj(1024, 512)
    bq_ref[...] = (normed(h, 1024) * QK_SCALE).astype(BF16)
    h = normed(proj(1536, 512), 1536)
    bk_ref[...] = h
    bk16_ref[...] = h.astype(BF16)
    h = proj(2048, 512)
    bv_ref[...] = h
    bv16_ref[...] = h.astype(BF16)
    h = proj(2560, 256)
    k = _rope_blocks(normed(h, 2560)[:, :LANES], cos, sin, first_half)
    ak_ref[...] = k
    ak16_ref[...] = k.astype(BF16)
    v = h[:, LANES:]
    av_ref[...] = v
    av16_ref[...] = v.astype(BF16)
    h = proj(2816, 256)
    k = _rope_blocks(normed(h, 2816)[:, :LANES], cos, sin, first_half)
    ik_ref[...] = k[:, :IDX_DIM]
    ik16_ref[...] = k.astype(BF16)
    iw_ref[...] = h[:, LANES:] * IDX_W_SCALE


def _in_proj_call(kern, name, widths_dtypes, x, g, w, gains, bd, cos, sin):
    n, d = x.shape
    tm = _row_tile(n)
    assert cos.shape[0] % tm == 0
    n_tab = cos.shape[0] // tm
    row = lambda width: pl.BlockSpec((tm, width), lambda i: (i, 0))
    tab = pl.BlockSpec((tm, LANES), lambda i: (i % n_tab, 0))
    return pl.pallas_call(
        kern,
        grid=(n // tm,),
        in_specs=[row(d), _const_spec((1, d)), _const_spec(w.shape), _const_spec(gains.shape),
                  _const_spec(bd.shape), tab, tab],
        out_specs=[row(wd) for wd, _ in widths_dtypes],
        out_shape=[jax.ShapeDtypeStruct((n, wd), dt) for wd, dt in widths_dtypes],
        compiler_params=_params(1),
        name=name,
    )(x, g, w, gains, bd, cos, sin)


def _even_in(x, g, w, gains, bd, cos, sin):
    widths_dtypes = [(512, BF16), (512, BF16), (512, BF16), (512, F32), (512, F32), (512, BF16), (512, BF16),
                     (128, F32), (128, F32), (128, BF16), (128, BF16), (IDX_DIM, F32), (128, BF16), (128, F32)]
    return _in_proj_call(_even_in_kernel, "even_in_proj", widths_dtypes, x, g, w, gains, bd, cos, sin)


def _odd_in_kernel(x_ref, g_ref, w_ref, gain_ref, bd_ref, cos_ref, sin_ref,
                   q_ref, k_ref, v_ref, k16_ref, v16_ref):
    tm = x_ref.shape[0]
    xn = _normed_input(x_ref, g_ref)
    bd = bd_ref[...]
    cos = cos_ref[...]
    sin = sin_ref[...]
    first_half = (lax.broadcasted_iota(jnp.int32, (tm, LANES), 1) % HEAD_DIM) < HEAD_DIM // 2
    for c0 in (0, 512):
        h = _dot(xn, w_ref[:, c0:c0 + 512])
        h = h * lax.rsqrt(_group_ms(h, bd) + EPS) * gain_ref[:, c0:c0 + 512]
        q_ref[:, c0:c0 + 512] = (_rope_blocks(h, cos, sin, first_half) * QK_SCALE).astype(BF16)
    h = _dot(xn, w_ref[:, 1024:1280])
    hn = h * lax.rsqrt(_group_ms(h, bd) + EPS) * gain_ref[:, 1024:1280]
    k = _rope_blocks(hn[:, :LANES], cos, sin, first_half)
    k_ref[...] = k
    k16_ref[...] = k.astype(BF16)
    v = h[:, LANES:]
    v_ref[...] = v
    v16_ref[...] = v.astype(BF16)


def _odd_in(x, g, w, gains, bd, cos, sin):
    widths_dtypes = [(1024, BF16), (128, F32), (128, F32), (128, BF16), (128, BF16)]
    return _in_proj_call(_odd_in_kernel, "odd_in_proj", widths_dtypes, x, g, w, gains, bd, cos, sin)


def _out_ffn_kernel(*refs, n_attn, d_ff, ff_chunk):
    h_ref = refs[0]
    attn_refs = refs[1:1 + n_attn]
    wo_refs = refs[1 + n_attn:1 + 2 * n_attn]
    g_ref, wgu_ref, wdn_ref, o_ref, yn_ref = refs[1 + 2 * n_attn:]
    o_ref[...] = h_ref[...]
    for a_ref, wo_ref in zip(attn_refs, wo_refs):
        o_ref[...] += _dot(a_ref[...], wo_ref[...])
    y = o_ref[...]
    ms = jnp.mean(y * y, axis=-1, keepdims=True)
    yn_ref[...] = (y * lax.rsqrt(ms + EPS) * g_ref[...]).astype(BF16)
    for c0 in range(0, d_ff, ff_chunk):
        gate = _dot(yn_ref[...], wgu_ref[:, c0:c0 + ff_chunk])
        up = _dot(yn_ref[...], wgu_ref[:, d_ff + c0:d_ff + c0 + ff_chunk])
        act = (gate * (1.0 / (1.0 + jnp.exp(-gate))) * up).astype(BF16)
        o_ref[...] += _dot(act, wdn_ref[c0:c0 + ff_chunk, :])


def _out_ffn(h, attns, wos, g, wgu, wdn):
    n, d = h.shape
    tm = _row_tile(n)
    d_ff = wdn.shape[0]
    row = lambda width: pl.BlockSpec((tm, width), lambda i: (i, 0))
    kern = functools.partial(_out_ffn_kernel, n_attn=len(attns), d_ff=d_ff, ff_chunk=MXU_COLS)
    return pl.pallas_call(
        kern,
        grid=(n // tm,),
        in_specs=[row(d)] + [row(a.shape[1]) for a in attns] + [_const_spec(w.shape) for w in wos]
                 + [_const_spec((1, d)), _const_spec(wgu.shape), _const_spec(wdn.shape)],
        out_specs=row(d),
        out_shape=jax.ShapeDtypeStruct((n, d), F32),
        scratch_shapes=[pltpu.VMEM((tm, d), BF16)],
        compiler_params=_params(1),
        name="out_proj_ffn",
    )(h, *attns, *wos, g, wgu, wdn)


GROUP_KEYS = 32 * SUBLANES


def _bit_transpose_32(rows):
    a = list(rows)
    j = 16
    m = 0x0000FFFF
    while j:
        k = 0
        while k < 32:
            t = (a[k] ^ lax.shift_right_logical(a[k + j], jnp.int32(j))) & jnp.int32(m)
            a[k] = a[k] ^ t
            a[k + j] = a[k + j] ^ lax.shift_left(t, jnp.int32(j))
            k = (k + j + 1) & ~j
        j >>= 1
        if j:
            m = (m ^ (m << j)) & 0xFFFFFFFF
    return a


def _dsa_kernel(aq_ref, iq_ref, iw_ref, ik_ref, ak_ref, vt_ref, o_ref,
                iqs_ref, aqs_ref, w_ref, keys_ref, planes_ref, s_ref, acc_ref,
                *, tk, q_off, l_valid, topk, idx_bits):
    tq = LANES
    n_slots = IDX_HEADS
    n_chains = 4
    groups_per_tile = tk // GROUP_KEYS
    n_groups = keys_ref.shape[0] * groups_per_tile
    j = pl.program_id(1)
    qpos0 = q_off + j * tq
    n_adm_max = jnp.minimum(((qpos0 + tq - 1) // CHUNK + 1) * CHUNK, l_valid)
    nkt = (n_adm_max + tk - 1) // tk
    lane = lax.broadcasted_iota(jnp.int32, (tq, LANES), 1)
    half = (lane < HEAD_DIM, lane >= HEAD_DIM)

    for c in range(n_slots // 2):
        iqb = iq_ref[:, c * LANES:(c + 1) * LANES]
        aqb = aq_ref[:, c * LANES:(c + 1) * LANES]
        for p in range(2):
            n = 2 * c + p
            iqs_ref[n * tq:(n + 1) * tq, :] = jnp.where(half[p], iqb, jnp.zeros_like(iqb))
            aqs_ref[n * tq:(n + 1) * tq, :] = jnp.where(half[p], aqb, jnp.zeros_like(aqb))
    w_ref[...] = iw_ref[...].T

    qcol = qpos0 + lax.broadcasted_iota(jnp.int32, (1, tq), 1)
    klim = jnp.minimum((qcol // CHUNK + 1) * CHUNK, l_valid)
    krow = lax.broadcasted_iota(jnp.int32, (tk, tq), 0)

    def score_tile(kt, carry):
        k0 = pl.multiple_of(kt * tk, tk)
        ikt = ik_ref[pl.ds(k0, tk), :]
        score = jnp.zeros((tk, tq), F32)
        for c in range(n_slots // 2):
            s = _dot_nt(ikt, iqs_ref[2 * c * tq:2 * (c + 1) * tq, :])
            for p in range(2):
                score = score + jnp.maximum(s[:, p * tq:(p + 1) * tq], 0.0) * w_ref[2 * c + p:2 * c + p + 1, :]
        bits = lax.bitcast_convert_type(score, jnp.int32)
        key = bits ^ ((bits >> 31) & jnp.int32(0x7FFFFFFF))
        key = jnp.where(key == -1, 0, key)
        key = jnp.where(krow + k0 < klim, key, INT_MIN)
        keys_ref[kt] = key
        ukey = key ^ INT_MIN
        for g in range(groups_per_tile):
            base = g * GROUP_KEYS
            planes = _bit_transpose_32([ukey[base + SUBLANES * i:base + SUBLANES * (i + 1), :] for i in range(32)])
            for b in range(32):
                planes_ref[b, kt * groups_per_tile + g] = planes[b]
        return carry

    lax.fori_loop(0, nkt, score_tile, 0)

    def count(indicator):
        def body(kt, acc):
            ind = indicator(keys_ref[kt], kt)
            return acc + jnp.sum(ind.reshape(tk // (n_chains * SUBLANES), n_chains, SUBLANES, tq), axis=0)
        acc = lax.fori_loop(0, nkt, body, jnp.zeros((n_chains, SUBLANES, tq), F32))
        return jnp.sum(jnp.sum(acc, axis=0), axis=0, keepdims=True)

    def popcount_rows(words):
        pcs = [lax.population_count(w) for w in words]
        chains = [sum(pcs[c::n_chains][1:], pcs[c]) for c in range(min(n_chains, len(pcs)))]
        return jnp.sum(sum(chains[1:], chains[0]).astype(F32), axis=0, keepdims=True)

    def search_bit(i, carry):
        ans, n_gt, und = carry
        plane = planes_ref[i]
        hit = [und[g] & plane[g] for g in range(n_groups)]
        total = n_gt + popcount_rows(hit)
        take = total >= topk
        takem = jnp.where(take, jnp.int32(-1), jnp.int32(0))
        ans = jnp.where(take, ans | jnp.left_shift(jnp.int32(1), 31 - i), ans)
        n_gt = jnp.where(take, n_gt, total)
        und = tuple((und[g] ^ hit[g]) ^ (und[g] & takem) for g in range(n_groups))
        return ans, n_gt, und

    live_groups = nkt * groups_per_tile
    und0 = tuple(jnp.full((SUBLANES, tq), jnp.where(g < live_groups, jnp.int32(-1), jnp.int32(0)), jnp.int32)
                 for g in range(n_groups))
    ans, c_gt, und = lax.fori_loop(
        0, 32, search_bit, (jnp.zeros((1, tq), jnp.int32), jnp.zeros((1, tq), F32), und0))
    vstar = ans ^ INT_MIN
    c_ge = c_gt + popcount_rows(und)
    need = topk - c_gt
    has_tie = jnp.where(c_ge > topk, jnp.where(vstar > INT_MIN, 1.0, 0.0), 0.0)
    idx_all = jnp.int32(2 ** idx_bits - 1)

    def tie_search():
        def tie_bit(i, jmax):
            cand = jmax | jnp.left_shift(jnp.int32(1), idx_bits - 1 - i)
            cnt = count(lambda k, kt: jnp.where(k == vstar, jnp.where(krow + kt * tk < cand, 1.0, 0.0), 0.0))
            return jnp.where(cnt <= need, cand, jmax)
        return lax.fori_loop(0, idx_bits, tie_bit, jnp.zeros((1, tq), jnp.int32))

    jmax = lax.cond(jnp.max(has_tie) > 0.0, tie_search, lambda: jnp.full((1, tq), idx_all, jnp.int32))
    jmax = jnp.where(vstar == INT_MIN, 0, jmax)

    acc_ref[...] = jnp.zeros(acc_ref.shape, F32)
    n_pairs = n_slots // 2

    def attend_tile(kt, m_run):
        k0 = pl.multiple_of(kt * tk, tk)
        key = keys_ref[kt]
        tie_ok = jnp.where(krow + k0 < jmax, 0.0, NEG)
        mb = jnp.where(key > vstar, 0.0, jnp.where(key == vstar, tie_ok, NEG))
        mb2 = jnp.concatenate([mb, mb], axis=1)
        akt = ak_ref[pl.ds(k0, tk), :]
        vt = vt_ref[kt]
        m_next = []
        for c in range(n_pairs):
            s = _dot_nt(akt, aqs_ref[2 * c * tq:2 * (c + 1) * tq, :]) + mb2
            s_ref[c] = s
            m_next.append(jnp.maximum(m_run[c], jnp.max(s, axis=0, keepdims=True)))
        for c in range(n_pairs):
            alpha = jnp.exp2(m_run[c] - m_next[c])
            pr = jnp.exp2(s_ref[c] - m_next[c])
            acc_ref[c] = alpha * acc_ref[c] + _dot(vt, pr.astype(BF16))
        return tuple(m_next)

    lax.fori_loop(0, nkt, attend_tile, tuple(jnp.full((1, 2 * tq), NEG, F32) for _ in range(n_pairs)))

    group0 = lax.broadcasted_iota(jnp.int32, (2 * HEAD_DIM, tq), 0) < HEAD_DIM
    for c in range(n_slots // 2):
        a = acc_ref[c]
        o = a[:2 * HEAD_DIM] / a[2 * HEAD_DIM:2 * HEAD_DIM + 1]
        o_ref[:, c * LANES:(c + 1) * LANES] = jnp.where(group0, o[:, :tq], o[:, tq:]).T.astype(BF16)


def _dsa(aq, iq, iw, ik, ak, vt, *, n_batch, nq, row_block0, tk, q_off, l_valid, topk, keys_3d):
    tq = LANES
    nkt_max = vt.shape[1]
    l_pad = nkt_max * tk
    assert l_valid <= l_pad and vt.shape[2:] == (V_ROWS, tk) and tk % GROUP_KEYS == 0
    qspec = lambda width: pl.BlockSpec((tq, width), lambda b, j: (row_block0 + b * nq + j, 0))
    if keys_3d:
        kspec = pl.BlockSpec((None, l_pad, LANES), lambda b, j: (b, 0, 0))
    else:
        kspec = pl.BlockSpec((l_pad, LANES), lambda b, j: (b, 0))
    kern = functools.partial(_dsa_kernel, tk=tk, q_off=q_off, l_valid=l_valid, topk=topk,
                             idx_bits=int(l_pad).bit_length())
    n_slots = IDX_HEADS
    return pl.pallas_call(
        kern,
        grid=(n_batch, nq),
        in_specs=[qspec(512), qspec(512), qspec(LANES), kspec, kspec,
                  pl.BlockSpec((None, nkt_max, V_ROWS, tk), lambda b, j: (b, 0, 0, 0))],
        out_specs=pl.BlockSpec((tq, 512), lambda b, j: (b * nq + j, 0)),
        out_shape=jax.ShapeDtypeStruct((n_batch * nq * tq, 512), BF16),
        scratch_shapes=[
            pltpu.VMEM((n_slots * tq, LANES), BF16),
            pltpu.VMEM((n_slots * tq, LANES), BF16),
            pltpu.VMEM((LANES, tq), F32),
            pltpu.VMEM((nkt_max, tk, tq), jnp.int32),
            pltpu.VMEM((32, l_pad // GROUP_KEYS, SUBLANES, tq), jnp.int32),
            pltpu.VMEM((n_slots // 2, tk, 2 * tq), F32),
            pltpu.VMEM((n_slots // 2, V_ROWS, 2 * tq), F32),
        ],
        compiler_params=_params(2),
        name="dsa_attention",
    )(aq, iq, iw, ik, ak, vt)


def _fill_band(buf, prev_ref, cur_ref, n_front, rows):
    buf[0:n_front, :] = prev_ref[...].astype(BF16)
    buf[n_front:n_front + rows, :] = cur_ref[...]
    buf[n_front + rows:, :] = jnp.zeros((buf.shape[0] - n_front - rows, buf.shape[1]), BF16)


def _band_call(kern, name, q, kprev, kcur, vprev, vcur, consts, *, n_batch, t, n_chunks, n_front, prev_3d):
    rows = n_chunks * CHUNK
    nq = t // rows
    wq = q.shape[1]
    wkv = kcur.shape[1]
    assert t % rows == 0
    if prev_3d:
        pspec = pl.BlockSpec((None, n_front, wkv), lambda b, j: (b, 0, 0))
    else:
        assert rows % n_front == 0 and t % n_front == 0
        per_seq, per_step = t // n_front, rows // n_front
        pspec = pl.BlockSpec((n_front, wkv), lambda b, j: (jnp.maximum(b * per_seq + j * per_step - 1, 0), 0))
    cspec = pl.BlockSpec((rows, wkv), lambda b, j: (b * nq + j, 0))
    qspec = pl.BlockSpec((rows, wq), lambda b, j: (b * nq + j, 0))
    return pl.pallas_call(
        kern,
        grid=(n_batch, nq),
        in_specs=[qspec, pspec, cspec, pspec, cspec] + [_const_spec(c.shape) for c in consts],
        out_specs=qspec,
        out_shape=jax.ShapeDtypeStruct((n_batch * t, wq), BF16),
        scratch_shapes=[pltpu.VMEM((n_front + rows + CHUNK, wkv), BF16)] * 2,
        compiler_params=_params(2),
        name=name,
    )(q, kprev, kcur, vprev, vcur, *consts)


def _band_b_kernel(q_ref, kp_ref, kc_ref, vp_ref, vc_ref, bias_ref, o_ref, kbuf, vbuf,
                   *, n_chunks, n_front, front_valid):
    j = pl.program_id(1)
    rows = n_chunks * CHUNK
    bwp = n_front + 2 * CHUNK
    _fill_band(kbuf, kp_ref, kc_ref, n_front, rows)
    _fill_band(vbuf, vp_ref, vc_ref, n_front, rows)
    half0 = lax.broadcasted_iota(jnp.int32, (CHUNK, LANES), 1) < HEAD_DIM
    ucol = lax.broadcasted_iota(jnp.int32, (1, bwp), 1)
    ones = jnp.ones((bwp, LANES), BF16)
    for ci in range(n_chunks):
        r0 = ci * CHUNK
        if not front_valid:
            front_mask = jnp.where(ucol >= jnp.where(j > 0, 0, n_front - r0), 0.0, NEG)
        for cp in range(B_HEADS // 2):
            cols = slice(cp * LANES, (cp + 1) * LANES)
            qq = q_ref[r0:r0 + CHUNK, cols]
            lhs = jnp.concatenate([jnp.where(half0, qq, jnp.zeros_like(qq)),
                                   jnp.where(half0, jnp.zeros_like(qq), qq)], axis=0)
            s = _dot_nt(lhs, kbuf[r0:r0 + bwp, cols]) + bias_ref[cp]
            if not front_valid:
                s = s + front_mask
            m = jnp.max(s, axis=-1, keepdims=True)
            e_ = jnp.exp(s - m).astype(BF16)
            o = _dot(e_, jnp.concatenate([vbuf[r0:r0 + bwp, cols], ones], axis=1))
            o = o[:, :LANES] / o[:, LANES:]
            o_ref[r0:r0 + CHUNK, cols] = jnp.where(half0, o[:CHUNK], o[CHUNK:]).astype(BF16)


def _band_b(q, kprev, kcur, vprev, vcur, bias, *, n_batch, t, n_chunks, front_valid, prev_3d):
    n_front = B_LEFT_CHUNKS * CHUNK
    kern = functools.partial(_band_b_kernel, n_chunks=n_chunks, n_front=n_front, front_valid=front_valid)
    return _band_call(kern, "band_b_attention", q, kprev, kcur, vprev, vcur, [bias],
                      n_batch=n_batch, t=t, n_chunks=n_chunks, n_front=n_front, prev_3d=prev_3d)


def _band_c_kernel(q_ref, kp_ref, kc_ref, vp_ref, vc_ref, base_ref, o_ref, kbuf, vbuf,
                   *, n_chunks, n_front, front_valid):
    j = pl.program_id(1)
    rows = n_chunks * CHUNK
    bw = n_front + CHUNK
    n_slots = C_HEADS
    _fill_band(kbuf, kp_ref, kc_ref, n_front, rows)
    _fill_band(vbuf, vp_ref, vc_ref, n_front, rows)
    lane = lax.broadcasted_iota(jnp.int32, (CHUNK, LANES), 1)
    half0 = lane < HEAD_DIM
    in_band = lax.broadcasted_iota(jnp.int32, (n_slots * CHUNK, MXU_COLS), 1) < bw
    ucol = lax.broadcasted_iota(jnp.int32, (1, MXU_COLS), 1)
    ones_band = jnp.ones((bw, LANES), BF16)
    tail = jnp.concatenate([jnp.zeros((MXU_COLS - bw, LANES), BF16), jnp.ones((MXU_COLS - bw, LANES), BF16)], axis=1)
    for ci in range(n_chunks):
        r0 = ci * CHUNK
        kb = kbuf[r0:r0 + MXU_COLS, :]
        vb = jnp.concatenate([jnp.concatenate([vbuf[r0:r0 + bw, :], ones_band], axis=1), tail], axis=0)
        parts = []
        for cb in range(n_slots // 2):
            qb = q_ref[r0:r0 + CHUNK, cb * LANES:(cb + 1) * LANES]
            parts.append(jnp.where(half0, qb, jnp.zeros_like(qb)))
            parts.append(jnp.where(half0, jnp.zeros_like(qb), qb))
        s = _dot_nt(jnp.concatenate(parts, axis=0), kb)
        if not front_valid:
            s = s + jnp.where(ucol >= jnp.where(j > 0, 0, n_front - r0), 0.0, NEG)
        s = jnp.where(in_band, s, base_ref[...])
        m = jnp.max(s, axis=-1, keepdims=True)
        e_ = jnp.exp(s - m)
        o = _dot(e_.astype(BF16), vb)
        o = o[:, :LANES] / o[:, LANES:]
        for cb in range(n_slots // 2):
            o0 = o[(2 * cb) * CHUNK:(2 * cb + 1) * CHUNK]
            o1 = o[(2 * cb + 1) * CHUNK:(2 * cb + 2) * CHUNK]
            o_ref[r0:r0 + CHUNK, cb * LANES:(cb + 1) * LANES] = jnp.where(half0, o0, o1).astype(BF16)


def _band_c(q, kprev, kcur, vprev, vcur, base, *, n_batch, t, n_chunks, front_valid, prev_3d):
    n_front = C_LEFT_CHUNKS * CHUNK
    kern = functools.partial(_band_c_kernel, n_chunks=n_chunks, n_front=n_front, front_valid=front_valid)
    return _band_call(kern, "band_c_attention", q, kprev, kcur, vprev, vcur, [base],
                      n_batch=n_batch, t=t, n_chunks=n_chunks, n_front=n_front, prev_3d=prev_3d)


def _rope_tables(pos):
    half = HEAD_DIM // 2
    inv_freq = ROPE_THETA ** (-jnp.arange(half, dtype=F32) / half)
    ang = pos.astype(F32)[:, None] * inv_freq[None, :]
    cos = jnp.cos(ang)
    sin = jnp.sin(ang)
    return (jnp.concatenate([cos, cos, cos, cos], axis=1), jnp.concatenate([-sin, sin, -sin, sin], axis=1))


def _perm_heads_cols(w, perm):
    d = w.shape[0]
    return w.reshape(d, len(perm), HEAD_DIM)[:, np.asarray(perm)].reshape(d, len(perm) * HEAD_DIM)


def _perm_heads_rows(w, perm):
    d = w.shape[1]
    return w.reshape(len(perm), HEAD_DIM, d)[np.asarray(perm)].reshape(len(perm) * HEAD_DIM, d)


def _even_weights(w_in, a_qn, a_kn, i_kn, b_qn, b_kn):
    d = w_in.shape[0]
    sizes = (A_HEADS * HEAD_DIM, A_KV_HEADS * HEAD_DIM, A_KV_HEADS * HEAD_DIM, IDX_HEADS * IDX_DIM, IDX_DIM,
             IDX_HEADS, B_HEADS * HEAD_DIM, B_HEADS * HEAD_DIM, B_HEADS * HEAD_DIM)
    aq, ak, av, iq, ik, iw, bq, bk, bv = jnp.split(w_in, np.cumsum(sizes)[:-1].tolist(), axis=1)
    w = jnp.concatenate([_perm_heads_cols(aq, A_PERM), iq, bq, bk, bv, ak, av, ik, ik,
                         iw, jnp.zeros((d, LANES - IDX_HEADS), w_in.dtype)], axis=1).astype(BF16)
    one = lambda n: jnp.ones((n,), F32)
    gains = jnp.concatenate([jnp.tile(a_qn, A_HEADS), one(512), jnp.tile(b_qn, B_HEADS), jnp.tile(b_kn, B_HEADS),
                             one(512), jnp.tile(a_kn, A_KV_HEADS), one(128), jnp.tile(i_kn, 2), one(128)])
    return w, gains[None, :].astype(F32)


def _odd_weights(w_in, c_qn, c_kn):
    q, k, v = jnp.split(w_in, [C_HEADS * HEAD_DIM, (C_HEADS + C_KV_HEADS) * HEAD_DIM], axis=1)
    w = jnp.concatenate([_perm_heads_cols(q, C_PERM), k, v], axis=1).astype(BF16)
    gains = jnp.concatenate([jnp.tile(c_qn, C_HEADS), jnp.tile(c_kn, C_KV_HEADS), jnp.ones((128,), F32)])
    return w, gains[None, :].astype(F32)


def _pad_rows(x, n, front):
    pad = [(0, 0)] * x.ndim
    pad[1] = (n, 0) if front else (0, n)
    return jnp.pad(x, pad)


def _values_transposed(v, tk):
    nbt, l_pad, _ = v.shape
    vt = jnp.swapaxes(v.reshape(nbt, l_pad // tk, tk, LANES), 2, 3)
    return jnp.concatenate([vt, jnp.ones((nbt, l_pad // tk, BF16_ROWS, tk), v.dtype)], axis=2)


def _band_b_bias(rb):
    n_front = B_LEFT_CHUNKS * CHUNK
    bw = n_front + CHUNK
    n_flat = n_front - B_MAX_REL + CHUNK
    ext = jnp.concatenate([jnp.broadcast_to(rb[:, :1], (rb.shape[0], n_flat)), rb[:, 1:B_MAX_REL + CHUNK]], axis=1)
    bias = jnp.stack([ext[:, CHUNK - 1 - t:CHUNK - 1 - t + bw] for t in range(CHUNK)], axis=1)
    bias = jnp.pad(bias, ((0, 0), (0, 0), (0, CHUNK)), constant_values=NEG)
    return bias.reshape(B_HEADS // 2, 2 * CHUNK, bw + CHUNK)


def kernel(x_prompt, x_sample, cache_a_k, cache_a_v, cache_a_kidx, cache_b_k, cache_b_v, cache_c_k, cache_c_v, norm_mix, norm_ffn, w_in_even, w_out_even, a_q_norm, a_k_norm, idx_k_norm, b_q_norm, b_k_norm, b_rel_bias, w_in_odd, w_out_odd, c_q_norm, c_k_norm, c_sinks, w_ffn_in, w_ffn_out):
    nb, seq, d = x_prompt.shape
    ns, t_new, _ = x_sample.shape
    past = cache_a_k.shape[2]
    depth = norm_mix.shape[0]
    assert seq % LANES == 0 and t_new == CHUNK and past % CHUNK == 0
    assert cache_b_k.shape[2] == B_LEFT_CHUNKS * CHUNK and cache_c_k.shape[2] == C_LEFT_CHUNKS * CHUNK
    topk_p = min(TOPK_MAX, seq // 4)
    topk_s = min(TOPK_MAX, (past + t_new) // 4)
    keep_b = min(B_LEFT_CHUNKS * CHUNK, seq)
    keep_c = min(C_LEFT_CHUNKS * CHUNK, seq)
    n_front_b = B_LEFT_CHUNKS * CHUNK
    n_front_c = C_LEFT_CHUNKS * CHUNK

    hp = x_prompt.reshape(nb * seq, d)
    hs = x_sample.reshape(ns * t_new, d)
    tm_p = _row_tile(nb * seq)
    tm_s = _row_tile(ns * t_new)
    pos_p = jnp.tile(jnp.arange(seq, dtype=jnp.int32), max(1, tm_p // seq))
    pos_s = jnp.tile(past + jnp.arange(t_new, dtype=jnp.int32), max(1, tm_s // t_new))
    cos_p, sin_p = _rope_tables(pos_p)
    cos_s, sin_s = _rope_tables(pos_s)
    gid = np.arange(MXU_COLS) // HEAD_DIM
    bd = jnp.asarray((gid[:, None] == gid[None, :]).astype(np.float32) / HEAD_DIM, BF16)

    tk_p = 512 if seq % 512 == 0 else (256 if seq % 256 == 0 else 128)
    tk_s = 256
    l_s = past + t_new
    l_s_pad = -(-l_s // tk_s) * tk_s
    assert seq % n_front_b == 0
    ch_p = n_front_b // CHUNK

    outs = {k: [] for k in ("pa_k", "pa_v", "pa_i", "pb_k", "pb_v", "pc_k", "pc_v",
                            "sa_k", "sa_v", "sa_i", "sb_k", "sb_v", "sc_k", "sc_v")}

    def last_rows(x, keep, width):
        return x.reshape(nb, seq, width)[:, seq - keep:]

    def pad_queries(x):
        return _pad_rows(x.reshape(ns, t_new, x.shape[1]), LANES - t_new, False).reshape(ns * LANES, x.shape[1])

    for layer in range(depth):
        li = layer // 2
        g_mix = norm_mix[layer][None, :]
        g_ffn = norm_ffn[layer][None, :]
        wgu = w_ffn_in[layer].astype(BF16)
        wdn = w_ffn_out[layer].astype(BF16)
        if layer % 2 == 0:
            w, gains = _even_weights(w_in_even[li], a_q_norm[li], a_k_norm[li], idx_k_norm[li],
                                     b_q_norm[li], b_k_norm[li])
            wo = w_out_even[li]
            wo_a = _perm_heads_rows(wo[:A_HEADS * HEAD_DIM], A_PERM).astype(BF16)
            wo_b = wo[A_HEADS * HEAD_DIM:].astype(BF16)
            bias = _band_b_bias(b_rel_bias[li].astype(F32))

            (aq, iq, bq, bk, bv, bk16, bv16, ak, av, ak16, av16, ik, ik16, iw) = _even_in(
                hp, g_mix, w, gains, bd, cos_p, sin_p)
            vt = _values_transposed(av16.reshape(nb, seq, LANES), tk_p)
            out_a = _dsa(aq, iq, iw, ik16, ak16, vt, n_batch=nb, nq=seq // LANES, row_block0=0, tk=tk_p,
                         q_off=0, l_valid=seq, topk=topk_p, keys_3d=False)
            out_b = _band_b(bq, bk16, bk16, bv16, bv16, bias, n_batch=nb, t=seq, n_chunks=ch_p,
                            front_valid=False, prev_3d=False)
            hp = _out_ffn(hp, [out_a, out_b], [wo_a, wo_b], g_ffn, wgu, wdn)
            outs["pa_k"].append(ak.reshape(nb, seq, A_KV_HEADS, HEAD_DIM))
            outs["pa_v"].append(av.reshape(nb, seq, A_KV_HEADS, HEAD_DIM))
            outs["pa_i"].append(ik.reshape(nb, seq, IDX_DIM))
            outs["pb_k"].append(last_rows(bk, keep_b, 512).reshape(nb, keep_b, B_HEADS, HEAD_DIM))
            outs["pb_v"].append(last_rows(bv, keep_b, 512).reshape(nb, keep_b, B_HEADS, HEAD_DIM))

            (aq, iq, bq, bk, bv, bk16, bv16, ak, av, ak16, av16, ik, ik16, iw) = _even_in(
                hs, g_mix, w, gains, bd, cos_s, sin_s)
            ci = cache_a_kidx[li].astype(BF16)
            ik_all = jnp.concatenate([jnp.concatenate([ci, ci], axis=-1), ik16.reshape(ns, t_new, LANES)], axis=1)
            ak_all = jnp.concatenate([cache_a_k[li].reshape(ns, past, LANES).astype(BF16),
                                      ak16.reshape(ns, t_new, LANES)], axis=1)
            av_all = jnp.concatenate([cache_a_v[li].reshape(ns, past, LANES).astype(BF16),
                                      av16.reshape(ns, t_new, LANES)], axis=1)
            ik_all, ak_all, av_all = (_pad_rows(x, l_s_pad - l_s, False) for x in (ik_all, ak_all, av_all))
            out_a = _dsa(pad_queries(aq), pad_queries(iq), pad_queries(iw), ik_all, ak_all,
                         _values_transposed(av_all, tk_s), n_batch=ns, nq=1, row_block0=0, tk=tk_s,
                         q_off=past, l_valid=l_s, topk=topk_s, keys_3d=True)
            out_a = out_a.reshape(ns, LANES, 512)[:, :t_new].reshape(ns * t_new, 512)
            out_b = _band_b(bq, cache_b_k[li].reshape(ns, n_front_b, 512), bk16,
                            cache_b_v[li].reshape(ns, n_front_b, 512), bv16, bias, n_batch=ns, t=t_new,
                            n_chunks=1, front_valid=True, prev_3d=True)
            hs = _out_ffn(hs, [out_a, out_b], [wo_a, wo_b], g_ffn, wgu, wdn)
            outs["sa_k"].append(ak.reshape(ns, t_new, A_KV_HEADS, HEAD_DIM))
            outs["sa_v"].append(av.reshape(ns, t_new, A_KV_HEADS, HEAD_DIM))
            outs["sa_i"].append(ik.reshape(ns, t_new, IDX_DIM))
            outs["sb_k"].append(bk.reshape(ns, t_new, B_HEADS, HEAD_DIM))
            outs["sb_v"].append(bv.reshape(ns, t_new, B_HEADS, HEAD_DIM))
        else:
            w, gains = _odd_weights(w_in_odd[li], c_q_norm[li], c_k_norm[li])
            wo = _perm_heads_rows(w_out_odd[li], C_PERM).astype(BF16)
            sinks = c_sinks[li].astype(F32)[np.asarray(C_PERM)]
            base = jnp.full((C_HEADS, 1, MXU_COLS), NEG, F32).at[:, 0, n_front_c + CHUNK].set(sinks)
            base = jnp.broadcast_to(base, (C_HEADS, CHUNK, MXU_COLS)).reshape(C_HEADS * CHUNK, MXU_COLS)

            q, k, v, k16, v16 = _odd_in(hp, g_mix, w, gains, bd, cos_p, sin_p)
            out_c = _band_c(q, k16, k16, v16, v16, base, n_batch=nb, t=seq, n_chunks=ch_p,
                            front_valid=False, prev_3d=False)
            hp = _out_ffn(hp, [out_c], [wo], g_ffn, wgu, wdn)
            outs["pc_k"].append(last_rows(k, keep_c, LANES).reshape(nb, keep_c, C_KV_HEADS, HEAD_DIM))
            outs["pc_v"].append(last_rows(v, keep_c, LANES).reshape(nb, keep_c, C_KV_HEADS, HEAD_DIM))

            q, k, v, k16, v16 = _odd_in(hs, g_mix, w, gains, bd, cos_s, sin_s)
            out_c = _band_c(q, cache_c_k[li].reshape(ns, n_front_c, LANES), k16,
                            cache_c_v[li].reshape(ns, n_front_c, LANES), v16, base, n_batch=ns, t=t_new,
                            n_chunks=1, front_valid=True, prev_3d=True)
            hs = _out_ffn(hs, [out_c], [wo], g_ffn, wgu, wdn)
            outs["sc_k"].append(k.reshape(ns, t_new, C_KV_HEADS, HEAD_DIM))
            outs["sc_v"].append(v.reshape(ns, t_new, C_KV_HEADS, HEAD_DIM))

    st = lambda name: jnp.stack(outs[name])
    return (hp.reshape(nb, seq, d), hs.reshape(ns, t_new, d),
            st("pa_k"), st("pa_v"), st("pa_i"), st("pb_k"), st("pb_v"), st("pc_k"), st("pc_v"),
            st("sa_k"), st("sa_v"), st("sa_i"), st("sb_k"), st("sb_v"), st("sc_k"), st("sc_v"))
```

```python
import functools
import math

import numpy as np
import jax
import jax.numpy as jnp
from jax import lax
from jax.experimental import pallas as pl
from jax.experimental.pallas import tpu as pltpu

CHUNK = 64
HEAD_DIM = 64
EPS = 1e-6
ROPE_THETA = 10000.0
A_HEADS = 8
A_KV_HEADS = 2
IDX_HEADS = 8
IDX_DIM = 64
IDX_W_SCALE = (IDX_HEADS * IDX_DIM) ** -0.5
TOPK_MAX = 256
B_HEADS = 8
B_LEFT_CHUNKS = 8
B_MAX_REL = 128
C_HEADS = 16
C_KV_HEADS = 2
C_LEFT_CHUNKS = 2
QK_SCALE = HEAD_DIM ** -0.5
QK_SCALE_LOG2 = QK_SCALE * math.log2(math.e)

LANES = 128
SUBLANES = 8
BF16_ROWS = 16
MXU_COLS = 256
VMEM_LIMIT = 56 * 1024 * 1024

NEG = -1e30
INT_MIN = np.int32(-2 ** 31)
F32 = jnp.float32
BF16 = jnp.bfloat16
V_ROWS = 2 * HEAD_DIM + BF16_ROWS

A_PERM = tuple(c + (A_HEADS // 2) * p for c in range(A_HEADS // 2) for p in range(2))
C_PERM = tuple(c + (C_HEADS // 2) * p for c in range(C_HEADS // 2) for p in range(2))


def _dot(a, b):
    return jnp.dot(a, b, preferred_element_type=F32)


def _dot_nt(a, b):
    return lax.dot_general(a, b, (((1,), (1,)), ((), ())), preferred_element_type=F32)


def _row_tile(n):
    for t in (512, 256, 128, 64):
        if n % t == 0:
            return t
    raise ValueError(f"row count {n} is not a multiple of {CHUNK}")


def _const_spec(shape):
    nd = len(shape)
    return pl.BlockSpec(shape, lambda *_: (0,) * nd)


def _params(n_axes):
    return pltpu.CompilerParams(dimension_semantics=("arbitrary",) * n_axes,
                                vmem_limit_bytes=VMEM_LIMIT)


def _group_ms(hb, bd):
    sq = (hb * hb).astype(BF16)
    w = hb.shape[1]
    parts = [_dot(sq[:, i:i + MXU_COLS], bd) for i in range(0, w, MXU_COLS)]
    return parts[0] if len(parts) == 1 else jnp.concatenate(parts, axis=1)


def _rope_blocks(y, cos, sin, first_half):
    out = []
    for i in range(0, y.shape[1], LANES):
        yb = y[:, i:i + LANES]
        sw = jnp.where(first_half, pltpu.roll(yb, LANES - 32, 1), pltpu.roll(yb, 32, 1))
        out.append(yb * cos + sw * sin)
    return out[0] if len(out) == 1 else jnp.concatenate(out, axis=1)


def _normed_input(x_ref, g_ref):
    x = x_ref[...]
    ms = jnp.mean(x * x, axis=-1, keepdims=True)
    return (x * lax.rsqrt(ms + EPS) * g_ref[...]).astype(BF16)


def _even_in_kernel(x_ref, g_ref, w_ref, gain_ref, bd_ref, cos_ref, sin_ref,
                    aq_ref, iq_ref, bq_ref, bk_ref, bv_ref, bk16_ref, bv16_ref,
                    ak_ref, av_ref, ak16_ref, av16_ref, ik_ref, ik16_ref, iw_ref):
    tm = x_ref.shape[0]
    xn = _normed_input(x_ref, g_ref)
    bd = bd_ref[...]
    cos = cos_ref[...]
    sin = sin_ref[...]
    first_half = (lax.broadcasted_iota(jnp.int32, (tm, LANES), 1) % HEAD_DIM) < HEAD_DIM // 2

    def proj(c0, width):
        return _dot(xn, w_ref[:, c0:c0 + width])

    def normed(h, c0):
        return h * lax.rsqrt(_group_ms(h, bd) + EPS) * gain_ref[:, c0:c0 + h.shape[1]]

    h = proj(0, 512)
    aq_ref[...] = (_rope_blocks(normed(h, 0), cos, sin, first_half) * QK_SCALE_LOG2).astype(BF16)
    h = proj(512, 512)
    iq_ref[...] = _rope_blocks(h, cos, sin, first_half).astype(BF16)
    h = proj(1024, 512)
    bq_ref[...] = (normed(h, 1024) * QK_SCALE).astype(BF16)
    h = normed(proj(1536, 512), 1536)
    bk_ref[...] = h
    bk16_ref[...] = h.astype(BF16)
    h = proj(2048, 512)
    bv_ref[...] = h
    bv16_ref[...] = h.astype(BF16)
    h = proj(2560, 256)
    k = _rope_blocks(normed(h, 2560)[:, :LANES], cos, sin, first_half)
    ak_ref[...] = k
    ak16_ref[...] = k.astype(BF16)
    v = h[:, LANES:]
    av_ref[...] = v
    av16_ref[...] = v.astype(BF16)
    h = proj(2816, 256)
    k = _rope_blocks(normed(h, 2816)[:, :LANES], cos, sin, first_half)
    ik_ref[...] = k[:, :IDX_DIM]
    ik16_ref[...] = k.astype(BF16)
    iw_ref[...] = h[:, LANES:] * IDX_W_SCALE


def _in_proj_call(kern, name, widths_dtypes, x, g, w, gains, bd, cos, sin):
    n, d = x.shape
    tm = _row_tile(n)
    assert cos.shape[0] % tm == 0
    n_tab = cos.shape[0] // tm
    row = lambda width: pl.BlockSpec((tm, width), lambda i: (i, 0))
    tab = pl.BlockSpec((tm, LANES), lambda i: (i % n_tab, 0))
    return pl.pallas_call(
        kern,
        grid=(n // tm,),
        in_specs=[row(d), _const_spec((1, d)), _const_spec(w.shape), _const_spec(gains.shape),
                  _const_spec(bd.shape), tab, tab],
        out_specs=[row(wd) for wd, _ in widths_dtypes],
        out_shape=[jax.ShapeDtypeStruct((n, wd), dt) for wd, dt in widths_dtypes],
        compiler_params=_params(1),
        name=name,
    )(x, g, w, gains, bd, cos, sin)


def _even_in(x, g, w, gains, bd, cos, sin):
    widths_dtypes = [(512, BF16), (512, BF16), (512, BF16), (512, F32), (512, F32), (512, BF16), (512, BF16),
                     (128, F32), (128, F32), (128, BF16), (128, BF16), (IDX_DIM, F32), (128, BF16), (128, F32)]
    return _in_proj_call(_even_in_kernel, "even_in_proj", widths_dtypes, x, g, w, gains, bd, cos, sin)


def _odd_in_kernel(x_ref, g_ref, w_ref, gain_ref, bd_ref, cos_ref, sin_ref,
                   q_ref, k_ref, v_ref, k16_ref, v16_ref):
    tm = x_ref.shape[0]
    xn = _normed_input(x_ref, g_ref)
    bd = bd_ref[...]
    cos = cos_ref[...]
    sin = sin_ref[...]
    first_half = (lax.broadcasted_iota(jnp.int32, (tm, LANES), 1) % HEAD_DIM) < HEAD_DIM // 2
    for c0 in (0, 512):
        h = _dot(xn, w_ref[:, c0:c0 + 512])
        h = h * lax.rsqrt(_group_ms(h, bd) + EPS) * gain_ref[:, c0:c0 + 512]
        q_ref[:, c0:c0 + 512] = (_rope_blocks(h, cos, sin, first_half) * QK_SCALE).astype(BF16)
    h = _dot(xn, w_ref[:, 1024:1280])
    hn = h * lax.rsqrt(_group_ms(h, bd) + EPS) * gain_ref[:, 1024:1280]
    k = _rope_blocks(hn[:, :LANES], cos, sin, first_half)
    k_ref[...] = k
    k16_ref[...] = k.astype(BF16)
    v = h[:, LANES:]
    v_ref[...] = v
    v16_ref[...] = v.astype(BF16)


def _odd_in(x, g, w, gains, bd, cos, sin):
    widths_dtypes = [(1024, BF16), (128, F32), (128, F32), (128, BF16), (128, BF16)]
    return _in_proj_call(_odd_in_kernel, "odd_in_proj", widths_dtypes, x, g, w, gains, bd, cos, sin)


def _out_ffn_kernel(*refs, n_attn, d_ff, ff_chunk):
    h_ref = refs[0]
    attn_refs = refs[1:1 + n_attn]
    wo_refs = refs[1 + n_attn:1 + 2 * n_attn]
    g_ref, wgu_ref, wdn_ref, o_ref, yn_ref = refs[1 + 2 * n_attn:]
    o_ref[...] = h_ref[...]
    for a_ref, wo_ref in zip(attn_refs, wo_refs):
        o_ref[...] += _dot(a_ref[...], wo_ref[...])
    y = o_ref[...]
    ms = jnp.mean(y * y, axis=-1, keepdims=True)
    yn_ref[...] = (y * lax.rsqrt(ms + EPS) * g_ref[...]).astype(BF16)
    for c0 in range(0, d_ff, ff_chunk):
        gate = _dot(yn_ref[...], wgu_ref[:, c0:c0 + ff_chunk])
        up = _dot(yn_ref[...], wgu_ref[:, d_ff + c0:d_ff + c0 + ff_chunk])
        act = (gate * (1.0 / (1.0 + jnp.exp(-gate))) * up).astype(BF16)
        o_ref[...] += _dot(act, wdn_ref[c0:c0 + ff_chunk, :])


def _out_ffn(h, attns, wos, g, wgu, wdn):
    n, d = h.shape
    tm = _row_tile(n)
    d_ff = wdn.shape[0]
    row = lambda width: pl.BlockSpec((tm, width), lambda i: (i, 0))
    kern = functools.partial(_out_ffn_kernel, n_attn=len(attns), d_ff=d_ff, ff_chunk=MXU_COLS)
    return pl.pallas_call(
        kern,
        grid=(n // tm,),
        in_specs=[row(d)] + [row(a.shape[1]) for a in attns] + [_const_spec(w.shape) for w in wos]
                 + [_const_spec((1, d)), _const_spec(wgu.shape), _const_spec(wdn.shape)],
        out_specs=row(d),
        out_shape=jax.ShapeDtypeStruct((n, d), F32),
        scratch_shapes=[pltpu.VMEM((tm, d), BF16)],
        compiler_params=_params(1),
        name="out_proj_ffn",
    )(h, *attns, *wos, g, wgu, wdn)


GROUP_KEYS = 32 * SUBLANES


def _bit_transpose_32(rows):
    a = list(rows)
    j = 16
    m = 0x0000FFFF
    while j:
        k = 0
        while k < 32:
            t = (a[k] ^ lax.shift_right_logical(a[k + j], jnp.int32(j))) & jnp.int32(m)
            a[k] = a[k] ^ t
            a[k + j] = a[k + j] ^ lax.shift_left(t, jnp.int32(j))
            k = (k + j + 1) & ~j
        j >>= 1
        if j:
            m = (m ^ (m << j)) & 0xFFFFFFFF
    return a


def _dsa_kernel(aq_ref, iq_ref, iw_ref, ik_ref, ak_ref, vt_ref, o_ref,
                iqs_ref, aqs_ref, w_ref, keys_ref, planes_ref, s_ref, acc_ref,
                *, tk, q_off, l_valid, topk, idx_bits, n_sub):
    tq = LANES
    n_slots = IDX_HEADS
    n_pairs = n_slots // 2
    n_chains = 4
    nkt_max = keys_ref.shape[1]
    groups_per_tile = tk // GROUP_KEYS
    n_groups = nkt_max * groups_per_tile
    jj = pl.program_id(1)
    lane = lax.broadcasted_iota(jnp.int32, (tq, LANES), 1)
    half = (lane < HEAD_DIM, lane >= HEAD_DIM)
    krow = lax.broadcasted_iota(jnp.int32, (tk, tq), 0)
    group0 = lax.broadcasted_iota(jnp.int32, (2 * HEAD_DIM, tq), 0) < HEAD_DIM
    idx_all = jnp.int32(2 ** idx_bits - 1)

    def tile_info(t):
        qpos0 = q_off + (jj * n_sub + t) * tq
        n_adm_max = jnp.minimum(((qpos0 + tq - 1) // CHUNK + 1) * CHUNK, l_valid)
        qcol = qpos0 + lax.broadcasted_iota(jnp.int32, (1, tq), 1)
        return (n_adm_max + tk - 1) // tk, jnp.minimum((qcol // CHUNK + 1) * CHUNK, l_valid)

    def build_stacks(t):
        rows = slice(t * tq, (t + 1) * tq)
        for c in range(n_pairs):
            iqb = iq_ref[rows, c * LANES:(c + 1) * LANES]
            aqb = aq_ref[rows, c * LANES:(c + 1) * LANES]
            for p in range(2):
                n = 2 * c + p
                iqs_ref[t, n * tq:(n + 1) * tq, :] = jnp.where(half[p], iqb, jnp.zeros_like(iqb))
                aqs_ref[t, n * tq:(n + 1) * tq, :] = jnp.where(half[p], aqb, jnp.zeros_like(aqb))
        w_ref[t] = iw_ref[rows, :].T

    def score_tile(t, klim, kt):
        par = t % 2
        k0 = pl.multiple_of(kt * tk, tk)
        ikt = ik_ref[pl.ds(k0, tk), :]
        score = jnp.zeros((tk, tq), F32)
        for c in range(n_pairs):
            s = _dot_nt(ikt, iqs_ref[t, 2 * c * tq:2 * (c + 1) * tq, :])
            for p in range(2):
                score = score + jnp.maximum(s[:, p * tq:(p + 1) * tq], 0.0) * w_ref[t, 2 * c + p:2 * c + p + 1, :]
        bits = lax.bitcast_convert_type(score, jnp.int32)
        key = bits ^ ((bits >> 31) & jnp.int32(0x7FFFFFFF))
        key = jnp.where(key == -1, 0, key)
        key = jnp.where(krow + k0 < klim, key, INT_MIN)
        keys_ref[par, kt] = key
        ukey = key ^ INT_MIN
        for g in range(groups_per_tile):
            base = g * GROUP_KEYS
            planes = _bit_transpose_32([ukey[base + SUBLANES * i:base + SUBLANES * (i + 1), :] for i in range(32)])
            for b in range(32):
                planes_ref[b, par, kt * groups_per_tile + g] = planes[b]

    def clear_unused_planes(t, nkt):
        zero = jnp.zeros((SUBLANES, tq), jnp.int32)

        def body(kt, carry):
            for g in range(groups_per_tile):
                for b in range(32):
                    planes_ref[b, t % 2, kt * groups_per_tile + g] = zero
            return carry
        lax.fori_loop(nkt, nkt_max, body, 0)

    def popcount_rows(words):
        pcs = [lax.population_count(w) for w in words]
        chains = [sum(pcs[c::n_chains][1:], pcs[c]) for c in range(min(n_chains, len(pcs)))]
        return jnp.sum(sum(chains[1:], chains[0]).astype(F32), axis=0, keepdims=True)

    def search(t, nkt):
        par = t % 2

        def count(indicator):
            def body(kt, acc):
                ind = indicator(keys_ref[par, kt], kt)
                return acc + jnp.sum(ind.reshape(tk // (n_chains * SUBLANES), n_chains, SUBLANES, tq), axis=0)
            acc = lax.fori_loop(0, nkt, body, jnp.zeros((n_chains, SUBLANES, tq), F32))
            return jnp.sum(jnp.sum(acc, axis=0), axis=0, keepdims=True)

        def search_bit(i, carry):
            ans, n_gt, und = carry
            plane = planes_ref[i, par]
            hit = [und[g] & plane[g] for g in range(n_groups)]
            total = n_gt + popcount_rows(hit)
            take = total >= topk
            takem = jnp.where(take, jnp.int32(-1), jnp.int32(0))
            ans = jnp.where(take, ans | jnp.left_shift(jnp.int32(1), 31 - i), ans)
            n_gt = jnp.where(take, n_gt, total)
            und = tuple((und[g] ^ hit[g]) ^ (und[g] & takem) for g in range(n_groups))
            return ans, n_gt, und

        live_groups = nkt * groups_per_tile
        und0 = tuple(jnp.full((SUBLANES, tq), jnp.where(g < live_groups, jnp.int32(-1), jnp.int32(0)), jnp.int32)
                     for g in range(n_groups))
        ans, c_gt, und = lax.fori_loop(
            0, 32, search_bit, (jnp.zeros((1, tq), jnp.int32), jnp.zeros((1, tq), F32), und0))
        vstar = ans ^ INT_MIN
        c_ge = c_gt + popcount_rows(und)
        need = topk - c_gt
        has_tie = jnp.where(c_ge > topk, jnp.where(vstar > INT_MIN, 1.0, 0.0), 0.0)

        def tie_search():
            def tie_bit(i, jmax):
                cand = jmax | jnp.left_shift(jnp.int32(1), idx_bits - 1 - i)
                cnt = count(lambda k, kt: jnp.where(k == vstar, jnp.where(krow + kt * tk < cand, 1.0, 0.0), 0.0))
                return jnp.where(cnt <= need, cand, jmax)
            return lax.fori_loop(0, idx_bits, tie_bit, jnp.zeros((1, tq), jnp.int32))

        jmax = lax.cond(jnp.max(has_tie) > 0.0, tie_search, lambda: jnp.full((1, tq), idx_all, jnp.int32))
        return vstar, jnp.where(vstar == INT_MIN, 0, jmax)

    def attend_tile(t, vstar, jmax, kt, m_run):
        k0 = pl.multiple_of(kt * tk, tk)
        key = keys_ref[t % 2, kt]
        tie_ok = jnp.where(krow + k0 < jmax, 0.0, NEG)
        mb = jnp.where(key > vstar, 0.0, jnp.where(key == vstar, tie_ok, NEG))
        mb2 = jnp.concatenate([mb, mb], axis=1)
        akt = ak_ref[pl.ds(k0, tk), :]
        vt = vt_ref[kt]
        m_next = []
        for c in range(n_pairs):
            s = _dot_nt(akt, aqs_ref[t, 2 * c * tq:2 * (c + 1) * tq, :]) + mb2
            s_ref[c] = s
            m_next.append(jnp.maximum(m_run[c], jnp.max(s, axis=0, keepdims=True)))
        for c in range(n_pairs):
            alpha = jnp.exp2(m_run[c] - m_next[c])
            pr = jnp.exp2(s_ref[c] - m_next[c])
            acc_ref[c] = alpha * acc_ref[c] + _dot(vt, pr.astype(BF16))
        return tuple(m_next)

    def finalize(t):
        for c in range(n_pairs):
            a = acc_ref[c]
            o = a[:2 * HEAD_DIM] / a[2 * HEAD_DIM:2 * HEAD_DIM + 1]
            o_ref[t * tq:(t + 1) * tq, c * LANES:(c + 1) * LANES] = (
                jnp.where(group0, o[:, :tq], o[:, tq:]).T.astype(BF16))

    info = [tile_info(t) for t in range(n_sub)]
    for t in range(n_sub):
        build_stacks(t)

    def score_only(t, lo, hi):
        def body(kt, carry):
            score_tile(t, info[t][1], kt)
            return carry
        lax.fori_loop(lo, hi, body, 0)

    score_only(0, 0, info[0][0])
    for t in range(n_sub):
        nkt = info[t][0]
        clear_unused_planes(t, nkt)
        vstar, jmax = search(t, nkt)
        acc_ref[...] = jnp.zeros(acc_ref.shape, F32)
        m0 = tuple(jnp.full((1, 2 * tq), NEG, F32) for _ in range(n_pairs))
        if t + 1 < n_sub:
            def fused(kt, m_run, t=t, vstar=vstar, jmax=jmax):
                score_tile(t + 1, info[t + 1][1], kt)
                return attend_tile(t, vstar, jmax, kt, m_run)
            lax.fori_loop(0, nkt, fused, m0)
            score_only(t + 1, nkt, info[t + 1][0])
        else:
            lax.fori_loop(0, nkt, lambda kt, m_run, t=t, vstar=vstar, jmax=jmax:
                          attend_tile(t, vstar, jmax, kt, m_run), m0)
        finalize(t)


def _dsa(aq, iq, iw, ik, ak, vt, *, n_batch, nq, tk, q_off, l_valid, topk, keys_3d):
    tq = LANES
    nkt_max = vt.shape[1]
    l_pad = nkt_max * tk
    assert l_valid <= l_pad and vt.shape[2:] == (V_ROWS, tk) and tk % GROUP_KEYS == 0
    n_sub = 4 if nq % 4 == 0 else (2 if nq % 2 == 0 else 1)
    steps = nq // n_sub
    qspec = lambda width: pl.BlockSpec((n_sub * tq, width), lambda b, j: (b * steps + j, 0))
    if keys_3d:
        kspec = pl.BlockSpec((None, l_pad, LANES), lambda b, j: (b, 0, 0))
    else:
        kspec = pl.BlockSpec((l_pad, LANES), lambda b, j: (b, 0))
    kern = functools.partial(_dsa_kernel, tk=tk, q_off=q_off, l_valid=l_valid, topk=topk,
                             idx_bits=int(l_pad).bit_length(), n_sub=n_sub)
    n_slots = IDX_HEADS
    return pl.pallas_call(
        kern,
        grid=(n_batch, steps),
        in_specs=[qspec(512), qspec(512), qspec(LANES), kspec, kspec,
                  pl.BlockSpec((None, nkt_max, V_ROWS, tk), lambda b, j: (b, 0, 0, 0))],
        out_specs=qspec(512),
        out_shape=jax.ShapeDtypeStruct((n_batch * nq * tq, 512), BF16),
        scratch_shapes=[
            pltpu.VMEM((n_sub, n_slots * tq, LANES), BF16),
            pltpu.VMEM((n_sub, n_slots * tq, LANES), BF16),
            pltpu.VMEM((n_sub, LANES, tq), F32),
            pltpu.VMEM((2, nkt_max, tk, tq), jnp.int32),
            pltpu.VMEM((32, 2, l_pad // GROUP_KEYS, SUBLANES, tq), jnp.int32),
            pltpu.VMEM((n_slots // 2, tk, 2 * tq), F32),
            pltpu.VMEM((n_slots // 2, V_ROWS, 2 * tq), F32),
        ],
        compiler_params=_params(2),
        name="dsa_attention",
    )(aq, iq, iw, ik, ak, vt)


def _fill_band(buf, prev_ref, cur_ref, n_front, rows):
    buf[0:n_front, :] = prev_ref[...].astype(BF16)
    buf[n_front:n_front + rows, :] = cur_ref[...]
    buf[n_front + rows:, :] = jnp.zeros((buf.shape[0] - n_front - rows, buf.shape[1]), BF16)


def _band_call(kern, name, q, kprev, kcur, vprev, vcur, consts, *, n_batch, t, n_chunks, n_front, prev_3d):
    rows = n_chunks * CHUNK
    nq = t // rows
    wq = q.shape[1]
    wkv = kcur.shape[1]
    assert t % rows == 0
    if prev_3d:
        pspec = pl.BlockSpec((None, n_front, wkv), lambda b, j: (b, 0, 0))
    else:
        assert rows % n_front == 0 and t % n_front == 0
        per_seq, per_step = t // n_front, rows // n_front
        pspec = pl.BlockSpec((n_front, wkv), lambda b, j: (jnp.maximum(b * per_seq + j * per_step - 1, 0), 0))
    cspec = pl.BlockSpec((rows, wkv), lambda b, j: (b * nq + j, 0))
    qspec = pl.BlockSpec((rows, wq), lambda b, j: (b * nq + j, 0))
    return pl.pallas_call(
        kern,
        grid=(n_batch, nq),
        in_specs=[qspec, pspec, cspec, pspec, cspec] + [_const_spec(c.shape) for c in consts],
        out_specs=qspec,
        out_shape=jax.ShapeDtypeStruct((n_batch * t, wq), BF16),
        scratch_shapes=[pltpu.VMEM((n_front + rows + CHUNK, wkv), BF16)] * 2,
        compiler_params=_params(2),
        name=name,
    )(q, kprev, kcur, vprev, vcur, *consts)


def _band_b_kernel(q_ref, kp_ref, kc_ref, vp_ref, vc_ref, bias_ref, o_ref, kbuf, vbuf,
                   *, n_chunks, n_front, front_valid):
    j = pl.program_id(1)
    rows = n_chunks * CHUNK
    bwp = n_front + 2 * CHUNK
    _fill_band(kbuf, kp_ref, kc_ref, n_front, rows)
    _fill_band(vbuf, vp_ref, vc_ref, n_front, rows)
    half0 = lax.broadcasted_iota(jnp.int32, (CHUNK, LANES), 1) < HEAD_DIM
    ucol = lax.broadcasted_iota(jnp.int32, (1, bwp), 1)
    ones = jnp.ones((bwp, LANES), BF16)
    for ci in range(n_chunks):
        r0 = ci * CHUNK
        if not front_valid:
            front_mask = jnp.where(ucol >= jnp.where(j > 0, 0, n_front - r0), 0.0, NEG)
        for cp in range(B_HEADS // 2):
            cols = slice(cp * LANES, (cp + 1) * LANES)
            qq = q_ref[r0:r0 + CHUNK, cols]
            lhs = jnp.concatenate([jnp.where(half0, qq, jnp.zeros_like(qq)),
                                   jnp.where(half0, jnp.zeros_like(qq), qq)], axis=0)
            s = _dot_nt(lhs, kbuf[r0:r0 + bwp, cols]) + bias_ref[cp]
            if not front_valid:
                s = s + front_mask
            m = jnp.max(s, axis=-1, keepdims=True)
            e_ = jnp.exp(s - m).astype(BF16)
            o = _dot(e_, jnp.concatenate([vbuf[r0:r0 + bwp, cols], ones], axis=1))
            o = o[:, :LANES] / o[:, LANES:]
            o_ref[r0:r0 + CHUNK, cols] = jnp.where(half0, o[:CHUNK], o[CHUNK:]).astype(BF16)


def _band_b(q, kprev, kcur, vprev, vcur, bias, *, n_batch, t, n_chunks, front_valid, prev_3d):
    n_front = B_LEFT_CHUNKS * CHUNK
    kern = functools.partial(_band_b_kernel, n_chunks=n_chunks, n_front=n_front, front_valid=front_valid)
    return _band_call(kern, "band_b_attention", q, kprev, kcur, vprev, vcur, [bias],
                      n_batch=n_batch, t=t, n_chunks=n_chunks, n_front=n_front, prev_3d=prev_3d)


def _band_c_kernel(q_ref, kp_ref, kc_ref, vp_ref, vc_ref, base_ref, o_ref, kbuf, vbuf,
                   *, n_chunks, n_front, front_valid):
    j = pl.program_id(1)
    rows = n_chunks * CHUNK
    bw = n_front + CHUNK
    n_slots = C_HEADS
    _fill_band(kbuf, kp_ref, kc_ref, n_front, rows)
    _fill_band(vbuf, vp_ref, vc_ref, n_front, rows)
    lane = lax.broadcasted_iota(jnp.int32, (CHUNK, LANES), 1)
    half0 = lane < HEAD_DIM
    in_band = lax.broadcasted_iota(jnp.int32, (n_slots * CHUNK, MXU_COLS), 1) < bw
    ucol = lax.broadcasted_iota(jnp.int32, (1, MXU_COLS), 1)
    ones_band = jnp.ones((bw, LANES), BF16)
    tail = jnp.concatenate([jnp.zeros((MXU_COLS - bw, LANES), BF16), jnp.ones((MXU_COLS - bw, LANES), BF16)], axis=1)
    for ci in range(n_chunks):
        r0 = ci * CHUNK
        kb = kbuf[r0:r0 + MXU_COLS, :]
        vb = jnp.concatenate([jnp.concatenate([vbuf[r0:r0 + bw, :], ones_band], axis=1), tail], axis=0)
        parts = []
        for cb in range(n_slots // 2):
            qb = q_ref[r0:r0 + CHUNK, cb * LANES:(cb + 1) * LANES]
            parts.append(jnp.where(half0, qb, jnp.zeros_like(qb)))
            parts.append(jnp.where(half0, jnp.zeros_like(qb), qb))
        s = _dot_nt(jnp.concatenate(parts, axis=0), kb)
        if not front_valid:
            s = s + jnp.where(ucol >= jnp.where(j > 0, 0, n_front - r0), 0.0, NEG)
        s = jnp.where(in_band, s, base_ref[...])
        m = jnp.max(s, axis=-1, keepdims=True)
        e_ = jnp.exp(s - m)
        o = _dot(e_.astype(BF16), vb)
        o = o[:, :LANES] / o[:, LANES:]
        for cb in range(n_slots // 2):
            o0 = o[(2 * cb) * CHUNK:(2 * cb + 1) * CHUNK]
            o1 = o[(2 * cb + 1) * CHUNK:(2 * cb + 2) * CHUNK]
            o_ref[r0:r0 + CHUNK, cb * LANES:(cb + 1) * LANES] = jnp.where(half0, o0, o1).astype(BF16)


def _band_c(q, kprev, kcur, vprev, vcur, base, *, n_batch, t, n_chunks, front_valid, prev_3d):
    n_front = C_LEFT_CHUNKS * CHUNK
    kern = functools.partial(_band_c_kernel, n_chunks=n_chunks, n_front=n_front, front_valid=front_valid)
    return _band_call(kern, "band_c_attention", q, kprev, kcur, vprev, vcur, [base],
                      n_batch=n_batch, t=t, n_chunks=n_chunks, n_front=n_front, prev_3d=prev_3d)


def _rope_tables(pos):
    half = HEAD_DIM // 2
    inv_freq = ROPE_THETA ** (-jnp.arange(half, dtype=F32) / half)
    ang = pos.astype(F32)[:, None] * inv_freq[None, :]
    cos = jnp.cos(ang)
    sin = jnp.sin(ang)
    return (jnp.concatenate([cos, cos, cos, cos], axis=1), jnp.concatenate([-sin, sin, -sin, sin], axis=1))


def _perm_heads_cols(w, perm):
    d = w.shape[0]
    return w.reshape(d, len(perm), HEAD_DIM)[:, np.asarray(perm)].reshape(d, len(perm) * HEAD_DIM)


def _perm_heads_rows(w, perm):
    d = w.shape[1]
    return w.reshape(len(perm), HEAD_DIM, d)[np.asarray(perm)].reshape(len(perm) * HEAD_DIM, d)


def _even_weights(w_in, a_qn, a_kn, i_kn, b_qn, b_kn):
    d = w_in.shape[0]
    sizes = (A_HEADS * HEAD_DIM, A_KV_HEADS * HEAD_DIM, A_KV_HEADS * HEAD_DIM, IDX_HEADS * IDX_DIM, IDX_DIM,
             IDX_HEADS, B_HEADS * HEAD_DIM, B_HEADS * HEAD_DIM, B_HEADS * HEAD_DIM)
    aq, ak, av, iq, ik, iw, bq, bk, bv = jnp.split(w_in, np.cumsum(sizes)[:-1].tolist(), axis=1)
    w = jnp.concatenate([_perm_heads_cols(aq, A_PERM), iq, bq, bk, bv, ak, av, ik, ik,
                         iw, jnp.zeros((d, LANES - IDX_HEADS), w_in.dtype)], axis=1).astype(BF16)
    one = lambda n: jnp.ones((n,), F32)
    gains = jnp.concatenate([jnp.tile(a_qn, A_HEADS), one(512), jnp.tile(b_qn, B_HEADS), jnp.tile(b_kn, B_HEADS),
                             one(512), jnp.tile(a_kn, A_KV_HEADS), one(128), jnp.tile(i_kn, 2), one(128)])
    return w, gains[None, :].astype(F32)


def _odd_weights(w_in, c_qn, c_kn):
    q, k, v = jnp.split(w_in, [C_HEADS * HEAD_DIM, (C_HEADS + C_KV_HEADS) * HEAD_DIM], axis=1)
    w = jnp.concatenate([_perm_heads_cols(q, C_PERM), k, v], axis=1).astype(BF16)
    gains = jnp.concatenate([jnp.tile(c_qn, C_HEADS), jnp.tile(c_kn, C_KV_HEADS), jnp.ones((128,), F32)])
    return w, gains[None, :].astype(F32)


def _pad_rows(x, n, front):
    pad = [(0, 0)] * x.ndim
    pad[1] = (n, 0) if front else (0, n)
    return jnp.pad(x, pad)


def _values_transposed(v, tk):
    nbt, l_pad, _ = v.shape
    vt = jnp.swapaxes(v.reshape(nbt, l_pad // tk, tk, LANES), 2, 3)
    return jnp.concatenate([vt, jnp.ones((nbt, l_pad // tk, BF16_ROWS, tk), v.dtype)], axis=2)


def _band_b_bias(rb):
    n_front = B_LEFT_CHUNKS * CHUNK
    bw = n_front + CHUNK
    n_flat = n_front - B_MAX_REL + CHUNK
    ext = jnp.concatenate([jnp.broadcast_to(rb[:, :1], (rb.shape[0], n_flat)), rb[:, 1:B_MAX_REL + CHUNK]], axis=1)
    bias = jnp.stack([ext[:, CHUNK - 1 - t:CHUNK - 1 - t + bw] for t in range(CHUNK)], axis=1)
    bias = jnp.pad(bias, ((0, 0), (0, 0), (0, CHUNK)), constant_values=NEG)
    return bias.reshape(B_HEADS // 2, 2 * CHUNK, bw + CHUNK)


def kernel(x_prompt, x_sample, cache_a_k, cache_a_v, cache_a_kidx, cache_b_k, cache_b_v, cache_c_k, cache_c_v, norm_mix, norm_ffn, w_in_even, w_out_even, a_q_norm, a_k_norm, idx_k_norm, b_q_norm, b_k_norm, b_rel_bias, w_in_odd, w_out_odd, c_q_norm, c_k_norm, c_sinks, w_ffn_in, w_ffn_out):
    nb, seq, d = x_prompt.shape
    ns, t_new, _ = x_sample.shape
    past = cache_a_k.shape[2]
    depth = norm_mix.shape[0]
    assert seq % LANES == 0 and t_new == CHUNK and past % CHUNK == 0
    assert cache_b_k.shape[2] == B_LEFT_CHUNKS * CHUNK and cache_c_k.shape[2] == C_LEFT_CHUNKS * CHUNK
    topk_p = min(TOPK_MAX, seq // 4)
    topk_s = min(TOPK_MAX, (past + t_new) // 4)
    keep_b = min(B_LEFT_CHUNKS * CHUNK, seq)
    keep_c = min(C_LEFT_CHUNKS * CHUNK, seq)
    n_front_b = B_LEFT_CHUNKS * CHUNK
    n_front_c = C_LEFT_CHUNKS * CHUNK

    hp = x_prompt.reshape(nb * seq, d)
    hs = x_sample.reshape(ns * t_new, d)
    tm_p = _row_tile(nb * seq)
    tm_s = _row_tile(ns * t_new)
    pos_p = jnp.tile(jnp.arange(seq, dtype=jnp.int32), max(1, tm_p // seq))
    pos_s = jnp.tile(past + jnp.arange(t_new, dtype=jnp.int32), max(1, tm_s // t_new))
    cos_p, sin_p = _rope_tables(pos_p)
    cos_s, sin_s = _rope_tables(pos_s)
    gid = np.arange(MXU_COLS) // HEAD_DIM
    bd = jnp.asarray((gid[:, None] == gid[None, :]).astype(np.float32) / HEAD_DIM, BF16)

    tk_p = 512 if seq % 512 == 0 else (256 if seq % 256 == 0 else 128)
    tk_s = 256
    l_s = past + t_new
    l_s_pad = -(-l_s // tk_s) * tk_s
    assert seq % n_front_b == 0
    ch_p = n_front_b // CHUNK

    outs = {k: [] for k in ("pa_k", "pa_v", "pa_i", "pb_k", "pb_v", "pc_k", "pc_v",
                            "sa_k", "sa_v", "sa_i", "sb_k", "sb_v", "sc_k", "sc_v")}

    def last_rows(x, keep, width):
        return x.reshape(nb, seq, width)[:, seq - keep:]

    def pad_queries(x):
        return _pad_rows(x.reshape(ns, t_new, x.shape[1]), LANES - t_new, False).reshape(ns * LANES, x.shape[1])

    for layer in range(depth):
        li = layer // 2
        g_mix = norm_mix[layer][None, :]
        g_ffn = norm_ffn[layer][None, :]
        wgu = w_ffn_in[layer].astype(BF16)
        wdn = w_ffn_out[layer].astype(BF16)
        if layer % 2 == 0:
            w, gains = _even_weights(w_in_even[li], a_q_norm[li], a_k_norm[li], idx_k_norm[li],
                                     b_q_norm[li], b_k_norm[li])
            wo = w_out_even[li]
            wo_a = _perm_heads_rows(wo[:A_HEADS * HEAD_DIM], A_PERM).astype(BF16)
            wo_b = wo[A_HEADS * HEAD_DIM:].astype(BF16)
            bias = _band_b_bias(b_rel_bias[li].astype(F32))

            (aq, iq, bq, bk, bv, bk16, bv16, ak, av, ak16, av16, ik, ik16, iw) = _even_in(
                hp, g_mix, w, gains, bd, cos_p, sin_p)
            vt = _values_transposed(av16.reshape(nb, seq, LANES), tk_p)
            out_a = _dsa(aq, iq, iw, ik16, ak16, vt, n_batch=nb, nq=seq // LANES, tk=tk_p,
                         q_off=0, l_valid=seq, topk=topk_p, keys_3d=False)
            out_b = _band_b(bq, bk16, bk16, bv16, bv16, bias, n_batch=nb, t=seq, n_chunks=ch_p,
                            front_valid=False, prev_3d=False)
            hp = _out_ffn(hp, [out_a, out_b], [wo_a, wo_b], g_ffn, wgu, wdn)
            outs["pa_k"].append(ak.reshape(nb, seq, A_KV_HEADS, HEAD_DIM))
            outs["pa_v"].append(av.reshape(nb, seq, A_KV_HEADS, HEAD_DIM))
            outs["pa_i"].append(ik.reshape(nb, seq, IDX_DIM))
            outs["pb_k"].append(last_rows(bk, keep_b, 512).reshape(nb, keep_b, B_HEADS, HEAD_DIM))
            outs["pb_v"].append(last_rows(bv, keep_b, 512).reshape(nb, keep_b, B_HEADS, HEAD_DIM))

            (aq, iq, bq, bk, bv, bk16, bv16, ak, av, ak16, av16, ik, ik16, iw) = _even_in(
                hs, g_mix, w, gains, bd, cos_s, sin_s)
            ci = cache_a_kidx[li].astype(BF16)
            ik_all = jnp.concatenate([jnp.concatenate([ci, ci], axis=-1), ik16.reshape(ns, t_new, LANES)], axis=1)
            ak_all = jnp.concatenate([cache_a_k[li].reshape(ns, past, LANES).astype(BF16),
                                      ak16.reshape(ns, t_new, LANES)], axis=1)
            av_all = jnp.concatenate([cache_a_v[li].reshape(ns, past, LANES).astype(BF16),
                                      av16.reshape(ns, t_new, LANES)], axis=1)
            ik_all, ak_all, av_all = (_pad_rows(x, l_s_pad - l_s, False) for x in (ik_all, ak_all, av_all))
            out_a = _dsa(pad_queries(aq), pad_queries(iq), pad_queries(iw), ik_all, ak_all,
                         _values_transposed(av_all, tk_s), n_batch=ns, nq=1, tk=tk_s,
                         q_off=past, l_valid=l_s, topk=topk_s, keys_3d=True)
            out_a = out_a.reshape(ns, LANES, 512)[:, :t_new].reshape(ns * t_new, 512)
            out_b = _band_b(bq, cache_b_k[li].reshape(ns, n_front_b, 512), bk16,
                            cache_b_v[li].reshape(ns, n_front_b, 512), bv16, bias, n_batch=ns, t=t_new,
                            n_chunks=1, front_valid=True, prev_3d=True)
            hs = _out_ffn(hs, [out_a, out_b], [wo_a, wo_b], g_ffn, wgu, wdn)
            outs["sa_k"].append(ak.reshape(ns, t_new, A_KV_HEADS, HEAD_DIM))
            outs["sa_v"].append(av.reshape(ns, t_new, A_KV_HEADS, HEAD_DIM))
            outs["sa_i"].append(ik.reshape(ns, t_new, IDX_DIM))
            outs["sb_k"].append(bk.reshape(ns, t_new, B_HEADS, HEAD_DIM))
            outs["sb_v"].append(bv.reshape(ns, t_new, B_HEADS, HEAD_DIM))
        else:
            w, gains = _odd_weights(w_in_odd[li], c_q_norm[li], c_k_norm[li])
            wo = _perm_heads_rows(w_out_odd[li], C_PERM).astype(BF16)
            sinks = c_sinks[li].astype(F32)[np.asarray(C_PERM)]
            base = jnp.full((C_HEADS, 1, MXU_COLS), NEG, F32).at[:, 0, n_front_c + CHUNK].set(sinks)
            base = jnp.broadcast_to(base, (C_HEADS, CHUNK, MXU_COLS)).reshape(C_HEADS * CHUNK, MXU_COLS)

            q, k, v, k16, v16 = _odd_in(hp, g_mix, w, gains, bd, cos_p, sin_p)
            out_c = _band_c(q, k16, k16, v16, v16, base, n_batch=nb, t=seq, n_chunks=ch_p,
                            front_valid=False, prev_3d=False)
            hp = _out_ffn(hp, [out_c], [wo], g_ffn, wgu, wdn)
            outs["pc_k"].append(last_rows(k, keep_c, LANES).reshape(nb, keep_c, C_KV_HEADS, HEAD_DIM))
            outs["pc_v"].append(last_rows(v, keep_c, LANES).reshape(nb, keep_c, C_KV_HEADS, HEAD_DIM))

            q, k, v, k16, v16 = _odd_in(hs, g_mix, w, gains, bd, cos_s, sin_s)
            out_c = _band_c(q, cache_c_k[li].reshape(ns, n_front_c, LANES), k16,
                            cache_c_v[li].reshape(ns, n_front_c, LANES), v16, base, n_batch=ns, t=t_new,
                            n_chunks=1, front_valid=True, prev_3d=True)
            hs = _out_ffn(hs, [out_c], [wo], g_ffn, wgu, wdn)
            outs["sc_k"].append(k.reshape(ns, t_new, C_KV_HEADS, HEAD_DIM))
            outs["sc_v"].append(v.reshape(ns, t_new, C_KV_HEADS, HEAD_DIM))

    st = lambda name: jnp.stack(outs[name])
    return (hp.reshape(nb, seq, d), hs.reshape(ns, t_new, d),
            st("pa_k"), st("pa_v"), st("pa_i"), st("pb_k"), st("pb_v"), st("pc_k"), st("pc_v"),
            st("sa_k"), st("sa_v"), st("sa_i"), st("sb_k"), st("sb_v"), st("sc_k"), st("sc_v"))
```

```python
import functools
import math

import numpy as np
import jax
import jax.numpy as jnp
from jax import lax
from jax.experimental import pallas as pl
from jax.experimental.pallas import tpu as pltpu

CHUNK = 64
HEAD_DIM = 64
EPS = 1e-6
ROPE_THETA = 10000.0
A_HEADS = 8
A_KV_HEADS = 2
IDX_HEADS = 8
IDX_DIM = 64
IDX_W_SCALE = (IDX_HEADS * IDX_DIM) ** -0.5
TOPK_MAX = 256
B_HEADS = 8
B_LEFT_CHUNKS = 8
B_MAX_REL = 128
C_HEADS = 16
C_KV_HEADS = 2
C_LEFT_CHUNKS = 2
QK_SCALE = HEAD_DIM ** -0.5
QK_SCALE_LOG2 = QK_SCALE * math.log2(math.e)

LANES = 128
SUBLANES = 8
BF16_ROWS = 16
MXU_COLS = 256
VMEM_LIMIT = 56 * 1024 * 1024

NEG = -1e30
INT_MIN = np.int32(-2 ** 31)
F32 = jnp.float32
BF16 = jnp.bfloat16
V_ROWS = 2 * HEAD_DIM + BF16_ROWS

A_PERM = tuple(c + (A_HEADS // 2) * p for c in range(A_HEADS // 2) for p in range(2))
C_PERM = tuple(c + (C_HEADS // 2) * p for c in range(C_HEADS // 2) for p in range(2))


def _dot(a, b):
    return jnp.dot(a, b, preferred_element_type=F32)


def _dot_nt(a, b):
    return lax.dot_general(a, b, (((1,), (1,)), ((), ())), preferred_element_type=F32)


def _row_tile(n):
    for t in (512, 256, 128, 64):
        if n % t == 0:
            return t
    raise ValueError(f"row count {n} is not a multiple of {CHUNK}")


def _const_spec(shape):
    nd = len(shape)
    return pl.BlockSpec(shape, lambda *_: (0,) * nd)


def _params(n_axes):
    return pltpu.CompilerParams(dimension_semantics=("arbitrary",) * n_axes,
                                vmem_limit_bytes=VMEM_LIMIT)


def _group_ms(hb, bd):
    sq = (hb * hb).astype(BF16)
    w = hb.shape[1]
    parts = [_dot(sq[:, i:i + MXU_COLS], bd) for i in range(0, w, MXU_COLS)]
    return parts[0] if len(parts) == 1 else jnp.concatenate(parts, axis=1)


def _rope_blocks(y, cos, sin, first_half):
    out = []
    for i in range(0, y.shape[1], LANES):
        yb = y[:, i:i + LANES]
        sw = jnp.where(first_half, pltpu.roll(yb, LANES - 32, 1), pltpu.roll(yb, 32, 1))
        out.append(yb * cos + sw * sin)
    return out[0] if len(out) == 1 else jnp.concatenate(out, axis=1)


def _normed_input(x_ref, g_ref):
    x = x_ref[...]
    ms = jnp.mean(x * x, axis=-1, keepdims=True)
    return (x * lax.rsqrt(ms + EPS) * g_ref[...]).astype(BF16)


def _even_in_kernel(x_ref, g_ref, w_ref, gain_ref, bd_ref, cos_ref, sin_ref,
                    aq_ref, iq_ref, bq_ref, bk_ref, bv_ref, bk16_ref, bv16_ref,
                    ak_ref, av_ref, ak16_ref, av16_ref, ik_ref, ik16_ref, iw_ref):
    tm = x_ref.shape[0]
    xn = _normed_input(x_ref, g_ref)
    bd = bd_ref[...]
    cos = cos_ref[...]
    sin = sin_ref[...]
    first_half = (lax.broadcasted_iota(jnp.int32, (tm, LANES), 1) % HEAD_DIM) < HEAD_DIM // 2

    def proj(c0, width):
        return _dot(xn, w_ref[:, c0:c0 + width])

    def normed(h, c0):
        return h * lax.rsqrt(_group_ms(h, bd) + EPS) * gain_ref[:, c0:c0 + h.shape[1]]

    h = proj(0, 512)
    aq_ref[...] = (_rope_blocks(normed(h, 0), cos, sin, first_half) * QK_SCALE_LOG2).astype(BF16)
    h = proj(512, 512)
    iq_ref[...] = _rope_blocks(h, cos, sin, first_half).astype(BF16)
    h = proj(1024, 512)
    bq_ref[...] = (normed(h, 1024) * QK_SCALE).astype(BF16)
    h = normed(proj(1536, 512), 1536)
    bk_ref[...] = h
    bk16_ref[...] = h.astype(BF16)
    h = proj(2048, 512)
    bv_ref[...] = h
    bv16_ref[...] = h.astype(BF16)
    h = proj(2560, 256)
    k = _rope_blocks(normed(h, 2560)[:, :LANES], cos, sin, first_half)
    ak_ref[...] = k
    ak16_ref[...] = k.astype(BF16)
    v = h[:, LANES:]
    av_ref[...] = v
    av16_ref[...] = v.astype(BF16)
    h = proj(2816, 256)
    k = _rope_blocks(normed(h, 2816)[:, :LANES], cos, sin, first_half)
    ik_ref[...] = k[:, :IDX_DIM]
    ik16_ref[...] = k.astype(BF16)
    iw_ref[...] = h[:, LANES:] * IDX_W_SCALE


def _in_proj_call(kern, name, widths_dtypes, x, g, w, gains, bd, cos, sin, tail_only, blocks_per_seq):
    n, d = x.shape
    tm = _row_tile(n)
    assert cos.shape[0] % tm == 0 and (n // tm) % blocks_per_seq == 0
    n_tab = cos.shape[0] // tm
    row = lambda width: pl.BlockSpec((tm, width), lambda i: (i, 0))
    tail = lambda width: pl.BlockSpec((tm, width), lambda i: (i // blocks_per_seq, 0))
    tab = pl.BlockSpec((tm, LANES), lambda i: (i % n_tab, 0))
    out_rows = [n // blocks_per_seq if o in tail_only else n for o in range(len(widths_dtypes))]
    return pl.pallas_call(
        kern,
        grid=(n // tm,),
        in_specs=[row(d), _const_spec((1, d)), _const_spec(w.shape), _const_spec(gains.shape),
                  _const_spec(bd.shape), tab, tab],
        out_specs=[tail(wd) if o in tail_only else row(wd) for o, (wd, _) in enumerate(widths_dtypes)],
        out_shape=[jax.ShapeDtypeStruct((r, wd), dt) for r, (wd, dt) in zip(out_rows, widths_dtypes)],
        compiler_params=_params(1),
        name=name,
    )(x, g, w, gains, bd, cos, sin)


def _even_in(x, g, w, gains, bd, cos, sin, blocks_per_seq):
    widths_dtypes = [(512, BF16), (512, BF16), (512, BF16), (512, F32), (512, F32), (512, BF16), (512, BF16),
                     (128, F32), (128, F32), (128, BF16), (128, BF16), (IDX_DIM, F32), (128, BF16), (128, F32)]
    return _in_proj_call(_even_in_kernel, "even_in_proj", widths_dtypes, x, g, w, gains, bd, cos, sin,
                         tail_only=(3, 4), blocks_per_seq=blocks_per_seq)


def _odd_in_kernel(x_ref, g_ref, w_ref, gain_ref, bd_ref, cos_ref, sin_ref,
                   q_ref, k_ref, v_ref, k16_ref, v16_ref):
    tm = x_ref.shape[0]
    xn = _normed_input(x_ref, g_ref)
    bd = bd_ref[...]
    cos = cos_ref[...]
    sin = sin_ref[...]
    first_half = (lax.broadcasted_iota(jnp.int32, (tm, LANES), 1) % HEAD_DIM) < HEAD_DIM // 2
    for c0 in (0, 512):
        h = _dot(xn, w_ref[:, c0:c0 + 512])
        h = h * lax.rsqrt(_group_ms(h, bd) + EPS) * gain_ref[:, c0:c0 + 512]
        q_ref[:, c0:c0 + 512] = (_rope_blocks(h, cos, sin, first_half) * QK_SCALE).astype(BF16)
    h = _dot(xn, w_ref[:, 1024:1280])
    hn = h * lax.rsqrt(_group_ms(h, bd) + EPS) * gain_ref[:, 1024:1280]
    k = _rope_blocks(hn[:, :LANES], cos, sin, first_half)
    k_ref[...] = k
    k16_ref[...] = k.astype(BF16)
    v = h[:, LANES:]
    v_ref[...] = v
    v16_ref[...] = v.astype(BF16)


def _odd_in(x, g, w, gains, bd, cos, sin, blocks_per_seq):
    widths_dtypes = [(1024, BF16), (128, F32), (128, F32), (128, BF16), (128, BF16)]
    return _in_proj_call(_odd_in_kernel, "odd_in_proj", widths_dtypes, x, g, w, gains, bd, cos, sin,
                         tail_only=(1, 2), blocks_per_seq=blocks_per_seq)


def _out_ffn_kernel(*refs, n_attn, d_ff, ff_chunk):
    h_ref = refs[0]
    attn_refs = refs[1:1 + n_attn]
    wo_refs = refs[1 + n_attn:1 + 2 * n_attn]
    g_ref, wgu_ref, wdn_ref, o_ref, yn_ref = refs[1 + 2 * n_attn:]
    o_ref[...] = h_ref[...]
    for a_ref, wo_ref in zip(attn_refs, wo_refs):
        o_ref[...] += _dot(a_ref[...], wo_ref[...])
    y = o_ref[...]
    ms = jnp.mean(y * y, axis=-1, keepdims=True)
    yn_ref[...] = (y * lax.rsqrt(ms + EPS) * g_ref[...]).astype(BF16)
    for c0 in range(0, d_ff, ff_chunk):
        gate = _dot(yn_ref[...], wgu_ref[:, c0:c0 + ff_chunk])
        up = _dot(yn_ref[...], wgu_ref[:, d_ff + c0:d_ff + c0 + ff_chunk])
        act = (gate * (1.0 / (1.0 + jnp.exp(-gate))) * up).astype(BF16)
        o_ref[...] += _dot(act, wdn_ref[c0:c0 + ff_chunk, :])


def _out_ffn(h, attns, wos, g, wgu, wdn):
    n, d = h.shape
    tm = _row_tile(n)
    d_ff = wdn.shape[0]
    row = lambda width: pl.BlockSpec((tm, width), lambda i: (i, 0))
    kern = functools.partial(_out_ffn_kernel, n_attn=len(attns), d_ff=d_ff, ff_chunk=MXU_COLS)
    return pl.pallas_call(
        kern,
        grid=(n // tm,),
        in_specs=[row(d)] + [row(a.shape[1]) for a in attns] + [_const_spec(w.shape) for w in wos]
                 + [_const_spec((1, d)), _const_spec(wgu.shape), _const_spec(wdn.shape)],
        out_specs=row(d),
        out_shape=jax.ShapeDtypeStruct((n, d), F32),
        scratch_shapes=[pltpu.VMEM((tm, d), BF16)],
        compiler_params=_params(1),
        name="out_proj_ffn",
    )(h, *attns, *wos, g, wgu, wdn)


GROUP_KEYS = 32 * SUBLANES


def _bit_transpose_32(rows):
    a = list(rows)
    j = 16
    m = 0x0000FFFF
    while j:
        k = 0
        while k < 32:
            t = (a[k] ^ lax.shift_right_logical(a[k + j], jnp.int32(j))) & jnp.int32(m)
            a[k] = a[k] ^ t
            a[k + j] = a[k + j] ^ lax.shift_left(t, jnp.int32(j))
            k = (k + j + 1) & ~j
        j >>= 1
        if j:
            m = (m ^ (m << j)) & 0xFFFFFFFF
    return a


def _dsa_kernel(aq_ref, iq_ref, iw_ref, ik_ref, ak_ref, vt_ref, o_ref,
                iqs_ref, aqs_ref, w_ref, keys_ref, planes_ref, s_ref, acc_ref,
                *, tk, q_off, l_valid, topk, idx_bits, n_sub):
    tq = LANES
    n_slots = IDX_HEADS
    n_pairs = n_slots // 2
    n_chains = 4
    nkt_max = keys_ref.shape[1]
    groups_per_tile = tk // GROUP_KEYS
    n_groups = nkt_max * groups_per_tile
    jj = pl.program_id(1)
    lane = lax.broadcasted_iota(jnp.int32, (tq, LANES), 1)
    half = (lane < HEAD_DIM, lane >= HEAD_DIM)
    krow = lax.broadcasted_iota(jnp.int32, (tk, tq), 0)
    group0 = lax.broadcasted_iota(jnp.int32, (2 * HEAD_DIM, tq), 0) < HEAD_DIM
    idx_all = jnp.int32(2 ** idx_bits - 1)

    def tile_info(t):
        qpos0 = q_off + (jj * n_sub + t) * tq
        n_adm_max = jnp.minimum(((qpos0 + tq - 1) // CHUNK + 1) * CHUNK, l_valid)
        qcol = qpos0 + lax.broadcasted_iota(jnp.int32, (1, tq), 1)
        return (n_adm_max + tk - 1) // tk, jnp.minimum((qcol // CHUNK + 1) * CHUNK, l_valid)

    def build_stacks(t):
        rows = slice(t * tq, (t + 1) * tq)
        for c in range(n_pairs):
            iqb = iq_ref[rows, c * LANES:(c + 1) * LANES]
            aqb = aq_ref[rows, c * LANES:(c + 1) * LANES]
            for p in range(2):
                n = 2 * c + p
                iqs_ref[t, n * tq:(n + 1) * tq, :] = jnp.where(half[p], iqb, jnp.zeros_like(iqb))
                aqs_ref[t, n * tq:(n + 1) * tq, :] = jnp.where(half[p], aqb, jnp.zeros_like(aqb))
        w_ref[t] = iw_ref[rows, :].T

    def score_tile(t, klim, kt):
        par = t % 2
        k0 = pl.multiple_of(kt * tk, tk)
        ikt = ik_ref[pl.ds(k0, tk), :]
        score = jnp.zeros((tk, tq), F32)
        for c in range(n_pairs):
            s = _dot_nt(ikt, iqs_ref[t, 2 * c * tq:2 * (c + 1) * tq, :])
            for p in range(2):
                score = score + jnp.maximum(s[:, p * tq:(p + 1) * tq], 0.0) * w_ref[t, 2 * c + p:2 * c + p + 1, :]
        bits = lax.bitcast_convert_type(score, jnp.int32)
        key = bits ^ ((bits >> 31) & jnp.int32(0x7FFFFFFF))
        key = jnp.where(key == -1, 0, key)
        key = jnp.where(krow + k0 < klim, key, INT_MIN)
        keys_ref[par, kt] = key
        ukey = key ^ INT_MIN
        for g in range(groups_per_tile):
            base = g * GROUP_KEYS
            planes = _bit_transpose_32([ukey[base + SUBLANES * i:base + SUBLANES * (i + 1), :] for i in range(32)])
            for b in range(32):
                planes_ref[b, par, kt * groups_per_tile + g] = planes[b]

    def clear_unused_planes(t, nkt):
        zero = jnp.zeros((SUBLANES, tq), jnp.int32)

        def body(kt, carry):
            for g in range(groups_per_tile):
                for b in range(32):
                    planes_ref[b, t % 2, kt * groups_per_tile + g] = zero
            return carry
        lax.fori_loop(nkt, nkt_max, body, 0)

    def popcount_rows(words):
        pcs = [lax.population_count(w) for w in words]
        chains = [sum(pcs[c::n_chains][1:], pcs[c]) for c in range(min(n_chains, len(pcs)))]
        return jnp.sum(sum(chains[1:], chains[0]).astype(F32), axis=0, keepdims=True)

    def search(t, nkt):
        par = t % 2

        def count(indicator):
            def body(kt, acc):
                ind = indicator(keys_ref[par, kt], kt)
                return acc + jnp.sum(ind.reshape(tk // (n_chains * SUBLANES), n_chains, SUBLANES, tq), axis=0)
            acc = lax.fori_loop(0, nkt, body, jnp.zeros((n_chains, SUBLANES, tq), F32))
            return jnp.sum(jnp.sum(acc, axis=0), axis=0, keepdims=True)

        def search_bit(i, carry):
            ans, n_gt, und = carry
            plane = planes_ref[i, par]
            hit = [und[g] & plane[g] for g in range(n_groups)]
            total = n_gt + popcount_rows(hit)
            take = total >= topk
            takem = jnp.where(take, jnp.int32(-1), jnp.int32(0))
            ans = jnp.where(take, ans | jnp.left_shift(jnp.int32(1), 31 - i), ans)
            n_gt = jnp.where(take, n_gt, total)
            und = tuple((und[g] ^ hit[g]) ^ (und[g] & takem) for g in range(n_groups))
            return ans, n_gt, und

        live_groups = nkt * groups_per_tile
        und0 = tuple(jnp.full((SUBLANES, tq), jnp.where(g < live_groups, jnp.int32(-1), jnp.int32(0)), jnp.int32)
                     for g in range(n_groups))
        ans, c_gt, und = lax.fori_loop(
            0, 32, search_bit, (jnp.zeros((1, tq), jnp.int32), jnp.zeros((1, tq), F32), und0))
        vstar = ans ^ INT_MIN
        c_ge = c_gt + popcount_rows(und)
        need = topk - c_gt
        has_tie = jnp.where(c_ge > topk, jnp.where(vstar > INT_MIN, 1.0, 0.0), 0.0)

        def tie_search():
            def tie_bit(i, jmax):
                cand = jmax | jnp.left_shift(jnp.int32(1), idx_bits - 1 - i)
                cnt = count(lambda k, kt: jnp.where(k == vstar, jnp.where(krow + kt * tk < cand, 1.0, 0.0), 0.0))
                return jnp.where(cnt <= need, cand, jmax)
            return lax.fori_loop(0, idx_bits, tie_bit, jnp.zeros((1, tq), jnp.int32))

        jmax = lax.cond(jnp.max(has_tie) > 0.0, tie_search, lambda: jnp.full((1, tq), idx_all, jnp.int32))
        return vstar, jnp.where(vstar == INT_MIN, 0, jmax)

    def attend_tile(t, vstar, jmax, kt, m_run):
        k0 = pl.multiple_of(kt * tk, tk)
        key = keys_ref[t % 2, kt]
        tie_ok = jnp.where(krow + k0 < jmax, 0.0, NEG)
        mb = jnp.where(key > vstar, 0.0, jnp.where(key == vstar, tie_ok, NEG))
        mb2 = jnp.concatenate([mb, mb], axis=1)
        akt = ak_ref[pl.ds(k0, tk), :]
        vt = vt_ref[kt]
        m_next = []
        for c in range(n_pairs):
            s = _dot_nt(akt, aqs_ref[t, 2 * c * tq:2 * (c + 1) * tq, :]) + mb2
            s_ref[c] = s
            m_next.append(jnp.maximum(m_run[c], jnp.max(s, axis=0, keepdims=True)))
        for c in range(n_pairs):
            alpha = jnp.exp2(m_run[c] - m_next[c])
            pr = jnp.exp2(s_ref[c] - m_next[c])
            acc_ref[c] = alpha * acc_ref[c] + _dot(vt, pr.astype(BF16))
        return tuple(m_next)

    def finalize(t):
        for c in range(n_pairs):
            a = acc_ref[c]
            o = a[:2 * HEAD_DIM] / a[2 * HEAD_DIM:2 * HEAD_DIM + 1]
            o_ref[t * tq:(t + 1) * tq, c * LANES:(c + 1) * LANES] = (
                jnp.where(group0, o[:, :tq], o[:, tq:]).T.astype(BF16))

    info = [tile_info(t) for t in range(n_sub)]
    for t in range(n_sub):
        build_stacks(t)

    def score_only(t, lo, hi):
        def body(kt, carry):
            score_tile(t, info[t][1], kt)
            return carry
        lax.fori_loop(lo, hi, body, 0)

    score_only(0, 0, info[0][0])
    for t in range(n_sub):
        nkt = info[t][0]
        clear_unused_planes(t, nkt)
        vstar, jmax = search(t, nkt)
        acc_ref[...] = jnp.zeros(acc_ref.shape, F32)
        m0 = tuple(jnp.full((1, 2 * tq), NEG, F32) for _ in range(n_pairs))
        if t + 1 < n_sub:
            def fused(kt, m_run, t=t, vstar=vstar, jmax=jmax):
                score_tile(t + 1, info[t + 1][1], kt)
                return attend_tile(t, vstar, jmax, kt, m_run)
            lax.fori_loop(0, nkt, fused, m0)
            score_only(t + 1, nkt, info[t + 1][0])
        else:
            lax.fori_loop(0, nkt, lambda kt, m_run, t=t, vstar=vstar, jmax=jmax:
                          attend_tile(t, vstar, jmax, kt, m_run), m0)
        finalize(t)


def _dsa(aq, iq, iw, ik, ak, vt, *, n_batch, nq, tk, q_off, l_valid, topk, keys_3d):
    tq = LANES
    nkt_max = vt.shape[1]
    l_pad = nkt_max * tk
    assert l_valid <= l_pad and vt.shape[2:] == (V_ROWS, tk) and tk % GROUP_KEYS == 0
    n_sub = 4 if nq % 4 == 0 else (2 if nq % 2 == 0 else 1)
    steps = nq // n_sub
    qspec = lambda width: pl.BlockSpec((n_sub * tq, width), lambda b, j: (b * steps + j, 0))
    if keys_3d:
        kspec = pl.BlockSpec((None, l_pad, LANES), lambda b, j: (b, 0, 0))
    else:
        kspec = pl.BlockSpec((l_pad, LANES), lambda b, j: (b, 0))
    kern = functools.partial(_dsa_kernel, tk=tk, q_off=q_off, l_valid=l_valid, topk=topk,
                             idx_bits=int(l_pad).bit_length(), n_sub=n_sub)
    n_slots = IDX_HEADS
    return pl.pallas_call(
        kern,
        grid=(n_batch, steps),
        in_specs=[qspec(512), qspec(512), qspec(LANES), kspec, kspec,
                  pl.BlockSpec((None, nkt_max, V_ROWS, tk), lambda b, j: (b, 0, 0, 0))],
        out_specs=qspec(512),
        out_shape=jax.ShapeDtypeStruct((n_batch * nq * tq, 512), BF16),
        scratch_shapes=[
            pltpu.VMEM((n_sub, n_slots * tq, LANES), BF16),
            pltpu.VMEM((n_sub, n_slots * tq, LANES), BF16),
            pltpu.VMEM((n_sub, LANES, tq), F32),
            pltpu.VMEM((2, nkt_max, tk, tq), jnp.int32),
            pltpu.VMEM((32, 2, l_pad // GROUP_KEYS, SUBLANES, tq), jnp.int32),
            pltpu.VMEM((n_slots // 2, tk, 2 * tq), F32),
            pltpu.VMEM((n_slots // 2, V_ROWS, 2 * tq), F32),
        ],
        compiler_params=_params(2),
        name="dsa_attention",
    )(aq, iq, iw, ik, ak, vt)


def _fill_band(buf, prev_ref, cur_ref, n_front, rows):
    buf[0:n_front, :] = prev_ref[...].astype(BF16)
    buf[n_front:n_front + rows, :] = cur_ref[...]
    buf[n_front + rows:, :] = jnp.zeros((buf.shape[0] - n_front - rows, buf.shape[1]), BF16)


def _band_call(kern, name, q, kprev, kcur, vprev, vcur, consts, *, n_batch, t, n_chunks, n_front, prev_3d):
    rows = n_chunks * CHUNK
    nq = t // rows
    wq = q.shape[1]
    wkv = kcur.shape[1]
    assert t % rows == 0
    if prev_3d:
        pspec = pl.BlockSpec((None, n_front, wkv), lambda b, j: (b, 0, 0))
    else:
        assert rows % n_front == 0 and t % n_front == 0
        per_seq, per_step = t // n_front, rows // n_front
        pspec = pl.BlockSpec((n_front, wkv), lambda b, j: (jnp.maximum(b * per_seq + j * per_step - 1, 0), 0))
    cspec = pl.BlockSpec((rows, wkv), lambda b, j: (b * nq + j, 0))
    qspec = pl.BlockSpec((rows, wq), lambda b, j: (b * nq + j, 0))
    return pl.pallas_call(
        kern,
        grid=(n_batch, nq),
        in_specs=[qspec, pspec, cspec, pspec, cspec] + [_const_spec(c.shape) for c in consts],
        out_specs=qspec,
        out_shape=jax.ShapeDtypeStruct((n_batch * t, wq), BF16),
        scratch_shapes=[pltpu.VMEM((n_front + rows + CHUNK, wkv), BF16)] * 2,
        compiler_params=_params(2),
        name=name,
    )(q, kprev, kcur, vprev, vcur, *consts)


def _band_b_kernel(q_ref, kp_ref, kc_ref, vp_ref, vc_ref, bias_ref, o_ref, kbuf, vbuf,
                   *, n_chunks, n_front, front_valid):
    j = pl.program_id(1)
    rows = n_chunks * CHUNK
    bwp = n_front + 2 * CHUNK
    _fill_band(kbuf, kp_ref, kc_ref, n_front, rows)
    _fill_band(vbuf, vp_ref, vc_ref, n_front, rows)
    half0 = lax.broadcasted_iota(jnp.int32, (CHUNK, LANES), 1) < HEAD_DIM
    ucol = lax.broadcasted_iota(jnp.int32, (1, bwp), 1)
    ones = jnp.ones((bwp, LANES), BF16)
    for ci in range(n_chunks):
        r0 = ci * CHUNK
        if not front_valid:
            front_mask = jnp.where(ucol >= jnp.where(j > 0, 0, n_front - r0), 0.0, NEG)
        for cp in range(B_HEADS // 2):
            cols = slice(cp * LANES, (cp + 1) * LANES)
            qq = q_ref[r0:r0 + CHUNK, cols]
            lhs = jnp.concatenate([jnp.where(half0, qq, jnp.zeros_like(qq)),
                                   jnp.where(half0, jnp.zeros_like(qq), qq)], axis=0)
            s = _dot_nt(lhs, kbuf[r0:r0 + bwp, cols]) + bias_ref[cp]
            if not front_valid:
                s = s + front_mask
            m = jnp.max(s, axis=-1, keepdims=True)
            e_ = jnp.exp(s - m).astype(BF16)
            o = _dot(e_, jnp.concatenate([vbuf[r0:r0 + bwp, cols], ones], axis=1))
            o = o[:, :LANES] / o[:, LANES:]
            o_ref[r0:r0 + CHUNK, cols] = jnp.where(half0, o[:CHUNK], o[CHUNK:]).astype(BF16)


def _band_b(q, kprev, kcur, vprev, vcur, bias, *, n_batch, t, n_chunks, front_valid, prev_3d):
    n_front = B_LEFT_CHUNKS * CHUNK
    kern = functools.partial(_band_b_kernel, n_chunks=n_chunks, n_front=n_front, front_valid=front_valid)
    return _band_call(kern, "band_b_attention", q, kprev, kcur, vprev, vcur, [bias],
                      n_batch=n_batch, t=t, n_chunks=n_chunks, n_front=n_front, prev_3d=prev_3d)


def _band_c_kernel(q_ref, kp_ref, kc_ref, vp_ref, vc_ref, base_ref, o_ref, kbuf, vbuf,
                   *, n_chunks, n_front, front_valid):
    j = pl.program_id(1)
    rows = n_chunks * CHUNK
    bw = n_front + CHUNK
    n_slots = C_HEADS
    _fill_band(kbuf, kp_ref, kc_ref, n_front, rows)
    _fill_band(vbuf, vp_ref, vc_ref, n_front, rows)
    lane = lax.broadcasted_iota(jnp.int32, (CHUNK, LANES), 1)
    half0 = lane < HEAD_DIM
    in_band = lax.broadcasted_iota(jnp.int32, (n_slots * CHUNK, MXU_COLS), 1) < bw
    ucol = lax.broadcasted_iota(jnp.int32, (1, MXU_COLS), 1)
    ones_band = jnp.ones((bw, LANES), BF16)
    tail = jnp.concatenate([jnp.zeros((MXU_COLS - bw, LANES), BF16), jnp.ones((MXU_COLS - bw, LANES), BF16)], axis=1)
    for ci in range(n_chunks):
        r0 = ci * CHUNK
        kb = kbuf[r0:r0 + MXU_COLS, :]
        vb = jnp.concatenate([jnp.concatenate([vbuf[r0:r0 + bw, :], ones_band], axis=1), tail], axis=0)
        parts = []
        for cb in range(n_slots // 2):
            qb = q_ref[r0:r0 + CHUNK, cb * LANES:(cb + 1) * LANES]
            parts.append(jnp.where(half0, qb, jnp.zeros_like(qb)))
            parts.append(jnp.where(half0, jnp.zeros_like(qb), qb))
        s = _dot_nt(jnp.concatenate(parts, axis=0), kb)
        if not front_valid:
            s = s + jnp.where(ucol >= jnp.where(j > 0, 0, n_front - r0), 0.0, NEG)
        s = jnp.where(in_band, s, base_ref[...])
        m = jnp.max(s, axis=-1, keepdims=True)
        e_ = jnp.exp(s - m)
        o = _dot(e_.astype(BF16), vb)
        o = o[:, :LANES] / o[:, LANES:]
        for cb in range(n_slots // 2):
            o0 = o[(2 * cb) * CHUNK:(2 * cb + 1) * CHUNK]
            o1 = o[(2 * cb + 1) * CHUNK:(2 * cb + 2) * CHUNK]
            o_ref[r0:r0 + CHUNK, cb * LANES:(cb + 1) * LANES] = jnp.where(half0, o0, o1).astype(BF16)


def _band_c(q, kprev, kcur, vprev, vcur, base, *, n_batch, t, n_chunks, front_valid, prev_3d):
    n_front = C_LEFT_CHUNKS * CHUNK
    kern = functools.partial(_band_c_kernel, n_chunks=n_chunks, n_front=n_front, front_valid=front_valid)
    return _band_call(kern, "band_c_attention", q, kprev, kcur, vprev, vcur, [base],
                      n_batch=n_batch, t=t, n_chunks=n_chunks, n_front=n_front, prev_3d=prev_3d)


def _rope_tables(pos):
    half = HEAD_DIM // 2
    inv_freq = ROPE_THETA ** (-jnp.arange(half, dtype=F32) / half)
    ang = pos.astype(F32)[:, None] * inv_freq[None, :]
    cos = jnp.cos(ang)
    sin = jnp.sin(ang)
    return (jnp.concatenate([cos, cos, cos, cos], axis=1), jnp.concatenate([-sin, sin, -sin, sin], axis=1))


def _perm_heads_cols(w, perm):
    d = w.shape[0]
    return w.reshape(d, len(perm), HEAD_DIM)[:, np.asarray(perm)].reshape(d, len(perm) * HEAD_DIM)


def _perm_heads_rows(w, perm):
    d = w.shape[1]
    return w.reshape(len(perm), HEAD_DIM, d)[np.asarray(perm)].reshape(len(perm) * HEAD_DIM, d)


def _even_weights(w_in, a_qn, a_kn, i_kn, b_qn, b_kn):
    d = w_in.shape[0]
    sizes = (A_HEADS * HEAD_DIM, A_KV_HEADS * HEAD_DIM, A_KV_HEADS * HEAD_DIM, IDX_HEADS * IDX_DIM, IDX_DIM,
             IDX_HEADS, B_HEADS * HEAD_DIM, B_HEADS * HEAD_DIM, B_HEADS * HEAD_DIM)
    aq, ak, av, iq, ik, iw, bq, bk, bv = jnp.split(w_in, np.cumsum(sizes)[:-1].tolist(), axis=1)
    w = jnp.concatenate([_perm_heads_cols(aq, A_PERM), iq, bq, bk, bv, ak, av, ik, ik,
                         iw, jnp.zeros((d, LANES - IDX_HEADS), w_in.dtype)], axis=1).astype(BF16)
    one = lambda n: jnp.ones((n,), F32)
    gains = jnp.concatenate([jnp.tile(a_qn, A_HEADS), one(512), jnp.tile(b_qn, B_HEADS), jnp.tile(b_kn, B_HEADS),
                             one(512), jnp.tile(a_kn, A_KV_HEADS), one(128), jnp.tile(i_kn, 2), one(128)])
    return w, gains[None, :].astype(F32)


def _odd_weights(w_in, c_qn, c_kn):
    q, k, v = jnp.split(w_in, [C_HEADS * HEAD_DIM, (C_HEADS + C_KV_HEADS) * HEAD_DIM], axis=1)
    w = jnp.concatenate([_perm_heads_cols(q, C_PERM), k, v], axis=1).astype(BF16)
    gains = jnp.concatenate([jnp.tile(c_qn, C_HEADS), jnp.tile(c_kn, C_KV_HEADS), jnp.ones((128,), F32)])
    return w, gains[None, :].astype(F32)


def _pad_rows(x, n, front):
    pad = [(0, 0)] * x.ndim
    pad[1] = (n, 0) if front else (0, n)
    return jnp.pad(x, pad)


def _values_transposed(v, tk):
    nbt, l_pad, _ = v.shape
    vt = jnp.swapaxes(v.reshape(nbt, l_pad // tk, tk, LANES), 2, 3)
    return jnp.concatenate([vt, jnp.ones((nbt, l_pad // tk, BF16_ROWS, tk), v.dtype)], axis=2)


def _band_b_bias(rb):
    n_front = B_LEFT_CHUNKS * CHUNK
    bw = n_front + CHUNK
    n_flat = n_front - B_MAX_REL + CHUNK
    ext = jnp.concatenate([jnp.broadcast_to(rb[:, :1], (rb.shape[0], n_flat)), rb[:, 1:B_MAX_REL + CHUNK]], axis=1)
    period = ext.shape[1]
    rolled = jnp.roll(ext, -(CHUNK - 1), axis=1)
    bias = jnp.tile(rolled, (1, CHUNK))[:, :CHUNK * (period - 1)].reshape(rb.shape[0], CHUNK, period - 1)[:, :, :bw]
    bias = jnp.pad(bias, ((0, 0), (0, 0), (0, CHUNK)), constant_values=NEG)
    return bias.reshape(B_HEADS // 2, 2 * CHUNK, bw + CHUNK)


def kernel(x_prompt, x_sample, cache_a_k, cache_a_v, cache_a_kidx, cache_b_k, cache_b_v, cache_c_k, cache_c_v, norm_mix, norm_ffn, w_in_even, w_out_even, a_q_norm, a_k_norm, idx_k_norm, b_q_norm, b_k_norm, b_rel_bias, w_in_odd, w_out_odd, c_q_norm, c_k_norm, c_sinks, w_ffn_in, w_ffn_out):
    nb, seq, d = x_prompt.shape
    ns, t_new, _ = x_sample.shape
    past = cache_a_k.shape[2]
    depth = norm_mix.shape[0]
    assert seq % LANES == 0 and t_new == CHUNK and past % CHUNK == 0
    assert cache_b_k.shape[2] == B_LEFT_CHUNKS * CHUNK and cache_c_k.shape[2] == C_LEFT_CHUNKS * CHUNK
    topk_p = min(TOPK_MAX, seq // 4)
    topk_s = min(TOPK_MAX, (past + t_new) // 4)
    keep_b = min(B_LEFT_CHUNKS * CHUNK, seq)
    keep_c = min(C_LEFT_CHUNKS * CHUNK, seq)
    n_front_b = B_LEFT_CHUNKS * CHUNK
    n_front_c = C_LEFT_CHUNKS * CHUNK

    hp = x_prompt.reshape(nb * seq, d)
    hs = x_sample.reshape(ns * t_new, d)
    tm_p = _row_tile(nb * seq)
    bps_p = seq // tm_p if (seq % tm_p == 0 and tm_p >= max(keep_b, keep_c)) else 1
    tm_s = _row_tile(ns * t_new)
    pos_p = jnp.tile(jnp.arange(seq, dtype=jnp.int32), max(1, tm_p // seq))
    pos_s = jnp.tile(past + jnp.arange(t_new, dtype=jnp.int32), max(1, tm_s // t_new))
    cos_p, sin_p = _rope_tables(pos_p)
    cos_s, sin_s = _rope_tables(pos_s)
    gid = np.arange(MXU_COLS) // HEAD_DIM
    bd = jnp.asarray((gid[:, None] == gid[None, :]).astype(np.float32) / HEAD_DIM, BF16)

    tk_p = 512 if seq % 512 == 0 else (256 if seq % 256 == 0 else 128)
    tk_s = 256
    l_s = past + t_new
    l_s_pad = -(-l_s // tk_s) * tk_s
    assert seq % n_front_b == 0
    ch_p = n_front_b // CHUNK

    outs = {k: [] for k in ("pa_k", "pa_v", "pa_i", "pb_k", "pb_v", "pc_k", "pc_v",
                            "sa_k", "sa_v", "sa_i", "sb_k", "sb_v", "sc_k", "sc_v")}

    def last_rows(x, keep, width):
        rows = x.shape[0] // nb
        return x.reshape(nb, rows, width)[:, rows - keep:]

    def pad_queries(x):
        return _pad_rows(x.reshape(ns, t_new, x.shape[1]), LANES - t_new, False).reshape(ns * LANES, x.shape[1])

    for layer in range(depth):
        li = layer // 2
        g_mix = norm_mix[layer][None, :]
        g_ffn = norm_ffn[layer][None, :]
        wgu = w_ffn_in[layer].astype(BF16)
        wdn = w_ffn_out[layer].astype(BF16)
        if layer % 2 == 0:
            w, gains = _even_weights(w_in_even[li], a_q_norm[li], a_k_norm[li], idx_k_norm[li],
                                     b_q_norm[li], b_k_norm[li])
            wo = w_out_even[li]
            wo_a = _perm_heads_rows(wo[:A_HEADS * HEAD_DIM], A_PERM).astype(BF16)
            wo_b = wo[A_HEADS * HEAD_DIM:].astype(BF16)
            bias = _band_b_bias(b_rel_bias[li].astype(F32))

            (aq, iq, bq, bk, bv, bk16, bv16, ak, av, ak16, av16, ik, ik16, iw) = _even_in(
                hp, g_mix, w, gains, bd, cos_p, sin_p, bps_p)
            vt = _values_transposed(av16.reshape(nb, seq, LANES), tk_p)
            out_a = _dsa(aq, iq, iw, ik16, ak16, vt, n_batch=nb, nq=seq // LANES, tk=tk_p,
                         q_off=0, l_valid=seq, topk=topk_p, keys_3d=False)
            out_b = _band_b(bq, bk16, bk16, bv16, bv16, bias, n_batch=nb, t=seq, n_chunks=ch_p,
                            front_valid=False, prev_3d=False)
            hp = _out_ffn(hp, [out_a, out_b], [wo_a, wo_b], g_ffn, wgu, wdn)
            outs["pa_k"].append(ak.reshape(nb, seq, A_KV_HEADS, HEAD_DIM))
            outs["pa_v"].append(av.reshape(nb, seq, A_KV_HEADS, HEAD_DIM))
            outs["pa_i"].append(ik.reshape(nb, seq, IDX_DIM))
            outs["pb_k"].append(last_rows(bk, keep_b, 512).reshape(nb, keep_b, B_HEADS, HEAD_DIM))
            outs["pb_v"].append(last_rows(bv, keep_b, 512).reshape(nb, keep_b, B_HEADS, HEAD_DIM))

            (aq, iq, bq, bk, bv, bk16, bv16, ak, av, ak16, av16, ik, ik16, iw) = _even_in(
                hs, g_mix, w, gains, bd, cos_s, sin_s, 1)
            ci = cache_a_kidx[li].astype(BF16)
            ik_all = jnp.concatenate([jnp.concatenate([ci, ci], axis=-1), ik16.reshape(ns, t_new, LANES)], axis=1)
            ak_all = jnp.concatenate([cache_a_k[li].reshape(ns, past, LANES).astype(BF16),
                                      ak16.reshape(ns, t_new, LANES)], axis=1)
            av_all = jnp.concatenate([cache_a_v[li].reshape(ns, past, LANES).astype(BF16),
                                      av16.reshape(ns, t_new, LANES)], axis=1)
            ik_all, ak_all, av_all = (_pad_rows(x, l_s_pad - l_s, False) for x in (ik_all, ak_all, av_all))
            out_a = _dsa(pad_queries(aq), pad_queries(iq), pad_queries(iw), ik_all, ak_all,
                         _values_transposed(av_all, tk_s), n_batch=ns, nq=1, tk=tk_s,
                         q_off=past, l_valid=l_s, topk=topk_s, keys_3d=True)
            out_a = out_a.reshape(ns, LANES, 512)[:, :t_new].reshape(ns * t_new, 512)
            out_b = _band_b(bq, cache_b_k[li].reshape(ns, n_front_b, 512), bk16,
                            cache_b_v[li].reshape(ns, n_front_b, 512), bv16, bias, n_batch=ns, t=t_new,
                            n_chunks=1, front_valid=True, prev_3d=True)
            hs = _out_ffn(hs, [out_a, out_b], [wo_a, wo_b], g_ffn, wgu, wdn)
            outs["sa_k"].append(ak.reshape(ns, t_new, A_KV_HEADS, HEAD_DIM))
            outs["sa_v"].append(av.reshape(ns, t_new, A_KV_HEADS, HEAD_DIM))
            outs["sa_i"].append(ik.reshape(ns, t_new, IDX_DIM))
            outs["sb_k"].append(bk.reshape(ns, t_new, B_HEADS, HEAD_DIM))
            outs["sb_v"].append(bv.reshape(ns, t_new, B_HEADS, HEAD_DIM))
        else:
            w, gains = _odd_weights(w_in_odd[li], c_q_norm[li], c_k_norm[li])
            wo = _perm_heads_rows(w_out_odd[li], C_PERM).astype(BF16)
            sinks = c_sinks[li].astype(F32)[np.asarray(C_PERM)]
            base = jnp.full((C_HEADS, 1, MXU_COLS), NEG, F32).at[:, 0, n_front_c + CHUNK].set(sinks)
            base = jnp.broadcast_to(base, (C_HEADS, CHUNK, MXU_COLS)).reshape(C_HEADS * CHUNK, MXU_COLS)

            q, k, v, k16, v16 = _odd_in(hp, g_mix, w, gains, bd, cos_p, sin_p, bps_p)
            out_c = _band_c(q, k16, k16, v16, v16, base, n_batch=nb, t=seq, n_chunks=ch_p,
                            front_valid=False, prev_3d=False)
            hp = _out_ffn(hp, [out_c], [wo], g_ffn, wgu, wdn)
            outs["pc_k"].append(last_rows(k, keep_c, LANES).reshape(nb, keep_c, C_KV_HEADS, HEAD_DIM))
            outs["pc_v"].append(last_rows(v, keep_c, LANES).reshape(nb, keep_c, C_KV_HEADS, HEAD_DIM))

            q, k, v, k16, v16 = _odd_in(hs, g_mix, w, gains, bd, cos_s, sin_s, 1)
            out_c = _band_c(q, cache_c_k[li].reshape(ns, n_front_c, LANES), k16,
                            cache_c_v[li].reshape(ns, n_front_c, LANES), v16, base, n_batch=ns, t=t_new,
                            n_chunks=1, front_valid=True, prev_3d=True)
            hs = _out_ffn(hs, [out_c], [wo], g_ffn, wgu, wdn)
            outs["sc_k"].append(k.reshape(ns, t_new, C_KV_HEADS, HEAD_DIM))
            outs["sc_v"].append(v.reshape(ns, t_new, C_KV_HEADS, HEAD_DIM))

    st = lambda name: jnp.stack(outs[name])
    return (hp.reshape(nb, seq, d), hs.reshape(ns, t_new, d),
            st("pa_k"), st("pa_v"), st("pa_i"), st("pb_k"), st("pb_v"), st("pc_k"), st("pc_v"),
            st("sa_k"), st("sa_v"), st("sa_i"), st("sb_k"), st("sb_v"), st("sc_k"), st("sc_v"))
```

```python
import functools
import math

import numpy as np
import jax
import jax.numpy as jnp
from jax import lax
from jax.experimental import pallas as pl
from jax.experimental.pallas import tpu as pltpu

CHUNK = 64
HEAD_DIM = 64
EPS = 1e-6
ROPE_THETA = 10000.0
A_HEADS = 8
A_KV_HEADS = 2
IDX_HEADS = 8
IDX_DIM = 64
IDX_W_SCALE = (IDX_HEADS * IDX_DIM) ** -0.5
TOPK_MAX = 256
B_HEADS = 8
B_LEFT_CHUNKS = 8
B_MAX_REL = 128
C_HEADS = 16
C_KV_HEADS = 2
C_LEFT_CHUNKS = 2
QK_SCALE = HEAD_DIM ** -0.5
QK_SCALE_LOG2 = QK_SCALE * math.log2(math.e)

LANES = 128
SUBLANES = 8
BF16_ROWS = 16
MXU_COLS = 256
VMEM_LIMIT = 56 * 1024 * 1024

NEG = -1e30
INT_MIN = np.int32(-2 ** 31)
F32 = jnp.float32
BF16 = jnp.bfloat16
V_ROWS = 2 * HEAD_DIM + BF16_ROWS

A_PERM = tuple(c + (A_HEADS // 2) * p for c in range(A_HEADS // 2) for p in range(2))
C_PERM = tuple(c + (C_HEADS // 2) * p for c in range(C_HEADS // 2) for p in range(2))


def _dot(a, b):
    return jnp.dot(a, b, preferred_element_type=F32)


def _dot_nt(a, b):
    return lax.dot_general(a, b, (((1,), (1,)), ((), ())), preferred_element_type=F32)


def _row_tile(n):
    for t in (512, 256, 128, 64):
        if n % t == 0:
            return t
    raise ValueError(f"row count {n} is not a multiple of {CHUNK}")


def _const_spec(shape):
    nd = len(shape)
    return pl.BlockSpec(shape, lambda *_: (0,) * nd)


def _params(n_axes):
    return pltpu.CompilerParams(dimension_semantics=("arbitrary",) * n_axes,
                                vmem_limit_bytes=VMEM_LIMIT)


def _group_ms(hb, bd):
    sq = (hb * hb).astype(BF16)
    w = hb.shape[1]
    parts = [_dot(sq[:, i:i + MXU_COLS], bd) for i in range(0, w, MXU_COLS)]
    return parts[0] if len(parts) == 1 else jnp.concatenate(parts, axis=1)


def _rope_blocks(y, cos, sin, first_half):
    out = []
    for i in range(0, y.shape[1], LANES):
        yb = y[:, i:i + LANES]
        sw = jnp.where(first_half, pltpu.roll(yb, LANES - 32, 1), pltpu.roll(yb, 32, 1))
        out.append(yb * cos + sw * sin)
    return out[0] if len(out) == 1 else jnp.concatenate(out, axis=1)


def _normed_input(x_ref, g_ref):
    x = x_ref[...]
    ms = jnp.mean(x * x, axis=-1, keepdims=True)
    return (x * lax.rsqrt(ms + EPS) * g_ref[...]).astype(BF16)


def _even_in_kernel(x_ref, g_ref, w_ref, gain_ref, bd_ref, cos_ref, sin_ref,
                    aq_ref, iq_ref, bq_ref, bk_ref, bv_ref, bk16_ref, bv16_ref,
                    ak_ref, av_ref, ak16_ref, av16_ref, ik_ref, ik16_ref, iw_ref):
    tm = x_ref.shape[0]
    xn = _normed_input(x_ref, g_ref)
    bd = bd_ref[...]
    cos = cos_ref[...]
    sin = sin_ref[...]
    first_half = (lax.broadcasted_iota(jnp.int32, (tm, LANES), 1) % HEAD_DIM) < HEAD_DIM // 2

    def proj(c0, width):
        return _dot(xn, w_ref[:, c0:c0 + width])

    def normed(h, c0):
        return h * lax.rsqrt(_group_ms(h, bd) + EPS) * gain_ref[:, c0:c0 + h.shape[1]]

    h = proj(0, 512)
    aq_ref[...] = (_rope_blocks(normed(h, 0), cos, sin, first_half) * QK_SCALE_LOG2).astype(BF16)
    h = proj(512, 512)
    iq_ref[...] = _rope_blocks(h, cos, sin, first_half).astype(BF16)
    h = proj(1024, 512)
    bq_ref[...] = (normed(h, 1024) * QK_SCALE).astype(BF16)
    h = normed(proj(1536, 512), 1536)
    bk_ref[...] = h
    bk16_ref[...] = h.astype(BF16)
    h = proj(2048, 512)
    bv_ref[...] = h
    bv16_ref[...] = h.astype(BF16)
    h = proj(2560, 256)
    k = _rope_blocks(normed(h, 2560)[:, :LANES], cos, sin, first_half)
    ak_ref[...] = k
    ak16_ref[...] = k.astype(BF16)
    v = h[:, LANES:]
    av_ref[...] = v
    av16_ref[...] = v.astype(BF16)
    h = proj(2816, 256)
    k = _rope_blocks(normed(h, 2816)[:, :LANES], cos, sin, first_half)
    ik_ref[...] = k[:, :IDX_DIM]
    ik16_ref[...] = k.astype(BF16)
    iw_ref[...] = h[:, LANES:] * IDX_W_SCALE


def _in_proj_call(kern, name, widths_dtypes, x, g, w, gains, bd, cos, sin, tail_only, blocks_per_seq):
    n, d = x.shape
    tm = _row_tile(n)
    assert cos.shape[0] % tm == 0 and (n // tm) % blocks_per_seq == 0
    n_tab = cos.shape[0] // tm
    row = lambda width: pl.BlockSpec((tm, width), lambda i: (i, 0))
    tail = lambda width: pl.BlockSpec((tm, width), lambda i: (i // blocks_per_seq, 0))
    tab = pl.BlockSpec((tm, LANES), lambda i: (i % n_tab, 0))
    out_rows = [n // blocks_per_seq if o in tail_only else n for o in range(len(widths_dtypes))]
    return pl.pallas_call(
        kern,
        grid=(n // tm,),
        in_specs=[row(d), _const_spec((1, d)), _const_spec(w.shape), _const_spec(gains.shape),
                  _const_spec(bd.shape), tab, tab],
        out_specs=[tail(wd) if o in tail_only else row(wd) for o, (wd, _) in enumerate(widths_dtypes)],
        out_shape=[jax.ShapeDtypeStruct((r, wd), dt) for r, (wd, dt) in zip(out_rows, widths_dtypes)],
        compiler_params=_params(1),
        name=name,
    )(x, g, w, gains, bd, cos, sin)


def _even_in(x, g, w, gains, bd, cos, sin, blocks_per_seq):
    widths_dtypes = [(512, BF16), (512, BF16), (512, BF16), (512, F32), (512, F32), (512, BF16), (512, BF16),
                     (128, F32), (128, F32), (128, BF16), (128, BF16), (IDX_DIM, F32), (128, BF16), (128, F32)]
    return _in_proj_call(_even_in_kernel, "even_in_proj", widths_dtypes, x, g, w, gains, bd, cos, sin,
                         tail_only=(3, 4), blocks_per_seq=blocks_per_seq)


def _odd_in_kernel(x_ref, g_ref, w_ref, gain_ref, bd_ref, cos_ref, sin_ref,
                   q_ref, k_ref, v_ref, k16_ref, v16_ref):
    tm = x_ref.shape[0]
    xn = _normed_input(x_ref, g_ref)
    bd = bd_ref[...]
    cos = cos_ref[...]
    sin = sin_ref[...]
    first_half = (lax.broadcasted_iota(jnp.int32, (tm, LANES), 1) % HEAD_DIM) < HEAD_DIM // 2
    for c0 in (0, 512):
        h = _dot(xn, w_ref[:, c0:c0 + 512])
        h = h * lax.rsqrt(_group_ms(h, bd) + EPS) * gain_ref[:, c0:c0 + 512]
        q_ref[:, c0:c0 + 512] = (_rope_blocks(h, cos, sin, first_half) * QK_SCALE).astype(BF16)
    h = _dot(xn, w_ref[:, 1024:1280])
    hn = h * lax.rsqrt(_group_ms(h, bd) + EPS) * gain_ref[:, 1024:1280]
    k = _rope_blocks(hn[:, :LANES], cos, sin, first_half)
    k_ref[...] = k
    k16_ref[...] = k.astype(BF16)
    v = h[:, LANES:]
    v_ref[...] = v
    v16_ref[...] = v.astype(BF16)


def _odd_in(x, g, w, gains, bd, cos, sin, blocks_per_seq):
    widths_dtypes = [(1024, BF16), (128, F32), (128, F32), (128, BF16), (128, BF16)]
    return _in_proj_call(_odd_in_kernel, "odd_in_proj", widths_dtypes, x, g, w, gains, bd, cos, sin,
                         tail_only=(1, 2), blocks_per_seq=blocks_per_seq)


def _out_ffn_kernel(*refs, n_attn, d_ff, ff_chunk):
    h_ref = refs[0]
    attn_refs = refs[1:1 + n_attn]
    wo_refs = refs[1 + n_attn:1 + 2 * n_attn]
    g_ref, wgu_ref, wdn_ref, o_ref, yn_ref = refs[1 + 2 * n_attn:]
    o_ref[...] = h_ref[...]
    for a_ref, wo_ref in zip(attn_refs, wo_refs):
        o_ref[...] += _dot(a_ref[...], wo_ref[...])
    y = o_ref[...]
    ms = jnp.mean(y * y, axis=-1, keepdims=True)
    yn_ref[...] = (y * lax.rsqrt(ms + EPS) * g_ref[...]).astype(BF16)
    for c0 in range(0, d_ff, ff_chunk):
        gate = _dot(yn_ref[...], wgu_ref[:, c0:c0 + ff_chunk])
        up = _dot(yn_ref[...], wgu_ref[:, d_ff + c0:d_ff + c0 + ff_chunk])
        act = (gate * (1.0 / (1.0 + jnp.exp(-gate))) * up).astype(BF16)
        o_ref[...] += _dot(act, wdn_ref[c0:c0 + ff_chunk, :])


def _out_ffn(h, attns, wos, g, wgu, wdn, layer):
    n, d = h.shape
    tm = _row_tile(n)
    d_ff = wdn.shape[1]
    row = lambda width: pl.BlockSpec((tm, width), lambda i: (i, 0))
    layer_spec = lambda w: pl.BlockSpec((None,) + w.shape[1:], lambda i: (layer, 0, 0))
    kern = functools.partial(_out_ffn_kernel, n_attn=len(attns), d_ff=d_ff, ff_chunk=MXU_COLS)
    return pl.pallas_call(
        kern,
        grid=(n // tm,),
        in_specs=[row(d)] + [row(a.shape[1]) for a in attns] + [_const_spec(w.shape) for w in wos]
                 + [_const_spec((1, d)), layer_spec(wgu), layer_spec(wdn)],
        out_specs=row(d),
        out_shape=jax.ShapeDtypeStruct((n, d), F32),
        scratch_shapes=[pltpu.VMEM((tm, d), BF16)],
        compiler_params=_params(1),
        name="out_proj_ffn",
    )(h, *attns, *wos, g, wgu, wdn)


GROUP_KEYS = 32 * SUBLANES


def _bit_transpose_32(rows):
    a = list(rows)
    j = 16
    m = 0x0000FFFF
    while j:
        k = 0
        while k < 32:
            t = (a[k] ^ lax.shift_right_logical(a[k + j], jnp.int32(j))) & jnp.int32(m)
            a[k] = a[k] ^ t
            a[k + j] = a[k + j] ^ lax.shift_left(t, jnp.int32(j))
            k = (k + j + 1) & ~j
        j >>= 1
        if j:
            m = (m ^ (m << j)) & 0xFFFFFFFF
    return a


def _dsa_kernel(aq_ref, iq_ref, iw_ref, ik_ref, ak_ref, vt_ref, o_ref,
                iqs_ref, aqs_ref, w_ref, keys_ref, planes_ref, s_ref, acc_ref,
                *, tk, q_off, l_valid, topk, idx_bits, n_sub):
    tq = LANES
    n_slots = IDX_HEADS
    n_pairs = n_slots // 2
    n_chains = 4
    nkt_max = keys_ref.shape[1]
    groups_per_tile = tk // GROUP_KEYS
    n_groups = nkt_max * groups_per_tile
    jj = pl.program_id(1)
    krow = lax.broadcasted_iota(jnp.int32, (tk, tq), 0)
    group0 = lax.broadcasted_iota(jnp.int32, (2 * HEAD_DIM, tq), 0) < HEAD_DIM
    idx_all = jnp.int32(2 ** idx_bits - 1)

    def tile_info(t):
        qpos0 = q_off + (jj * n_sub + t) * tq
        n_adm_max = jnp.minimum(((qpos0 + tq - 1) // CHUNK + 1) * CHUNK, l_valid)
        qcol = qpos0 + lax.broadcasted_iota(jnp.int32, (1, tq), 1)
        return (n_adm_max + tk - 1) // tk, jnp.minimum((qcol // CHUNK + 1) * CHUNK, l_valid)

    def build_stacks(t):
        rows = slice(t * tq, (t + 1) * tq)
        for c in range(n_pairs):
            iqt = iq_ref[rows, c * LANES:(c + 1) * LANES].astype(F32).T
            aqt = aq_ref[rows, c * LANES:(c + 1) * LANES].astype(F32).T
            for p in range(2):
                n = 2 * c + p
                keep = group0 if p == 0 else jnp.logical_not(group0)
                iqs_ref[t, :, n * tq:(n + 1) * tq] = jnp.where(keep, iqt, 0.0).astype(BF16)
                aqs_ref[t, :, n * tq:(n + 1) * tq] = jnp.where(keep, aqt, 0.0).astype(BF16)
        w_ref[t] = iw_ref[rows, :].T

    def score_tile(t, klim, kt):
        par = t % 2
        k0 = pl.multiple_of(kt * tk, tk)
        ikt = ik_ref[pl.ds(k0, tk), :]
        score = jnp.zeros((tk, tq), F32)
        for c in range(n_pairs):
            s = _dot(ikt, iqs_ref[t, :, 2 * c * tq:2 * (c + 1) * tq])
            for p in range(2):
                score = score + jnp.maximum(s[:, p * tq:(p + 1) * tq], 0.0) * w_ref[t, 2 * c + p:2 * c + p + 1, :]
        bits = lax.bitcast_convert_type(score, jnp.int32)
        key = bits ^ ((bits >> 31) & jnp.int32(0x7FFFFFFF))
        key = jnp.where(key == -1, 0, key)
        key = jnp.where(krow + k0 < klim, key, INT_MIN)
        keys_ref[par, kt] = key
        ukey = key ^ INT_MIN
        for g in range(groups_per_tile):
            base = g * GROUP_KEYS
            planes = _bit_transpose_32([ukey[base + SUBLANES * i:base + SUBLANES * (i + 1), :] for i in range(32)])
            for b in range(32):
                planes_ref[b, par, kt * groups_per_tile + g] = planes[b]

    def clear_unused_planes(t, nkt):
        zero = jnp.zeros((SUBLANES, tq), jnp.int32)

        def body(kt, carry):
            for g in range(groups_per_tile):
                for b in range(32):
                    planes_ref[b, t % 2, kt * groups_per_tile + g] = zero
            return carry
        lax.fori_loop(nkt, nkt_max, body, 0)

    def popcount_rows(words):
        pcs = [lax.population_count(w) for w in words]
        chains = [sum(pcs[c::n_chains][1:], pcs[c]) for c in range(min(n_chains, len(pcs)))]
        return jnp.sum(sum(chains[1:], chains[0]).astype(F32), axis=0, keepdims=True)

    def search(t, nkt):
        par = t % 2

        def count(indicator):
            def body(kt, acc):
                ind = indicator(keys_ref[par, kt], kt)
                return acc + jnp.sum(ind.reshape(tk // (n_chains * SUBLANES), n_chains, SUBLANES, tq), axis=0)
            acc = lax.fori_loop(0, nkt, body, jnp.zeros((n_chains, SUBLANES, tq), F32))
            return jnp.sum(jnp.sum(acc, axis=0), axis=0, keepdims=True)

        def search_bit(i, carry):
            ans, n_gt, und = carry
            plane = planes_ref[i, par]
            hit = [und[g] & plane[g] for g in range(n_groups)]
            total = n_gt + popcount_rows(hit)
            take = total >= topk
            takem = jnp.where(take, jnp.int32(-1), jnp.int32(0))
            ans = jnp.where(take, ans | jnp.left_shift(jnp.int32(1), 31 - i), ans)
            n_gt = jnp.where(take, n_gt, total)
            und = tuple((und[g] ^ hit[g]) ^ (und[g] & takem) for g in range(n_groups))
            return ans, n_gt, und

        live_groups = nkt * groups_per_tile
        und0 = tuple(jnp.full((SUBLANES, tq), jnp.where(g < live_groups, jnp.int32(-1), jnp.int32(0)), jnp.int32)
                     for g in range(n_groups))
        ans, c_gt, und = lax.fori_loop(
            0, 32, search_bit, (jnp.zeros((1, tq), jnp.int32), jnp.zeros((1, tq), F32), und0))
        vstar = ans ^ INT_MIN
        c_ge = c_gt + popcount_rows(und)
        need = topk - c_gt
        has_tie = jnp.where(c_ge > topk, jnp.where(vstar > INT_MIN, 1.0, 0.0), 0.0)

        def tie_search():
            def tie_bit(i, jmax):
                cand = jmax | jnp.left_shift(jnp.int32(1), idx_bits - 1 - i)
                cnt = count(lambda k, kt: jnp.where(k == vstar, jnp.where(krow + kt * tk < cand, 1.0, 0.0), 0.0))
                return jnp.where(cnt <= need, cand, jmax)
            return lax.fori_loop(0, idx_bits, tie_bit, jnp.zeros((1, tq), jnp.int32))

        jmax = lax.cond(jnp.max(has_tie) > 0.0, tie_search, lambda: jnp.full((1, tq), idx_all, jnp.int32))
        return vstar, jnp.where(vstar == INT_MIN, 0, jmax)

    def attend_tile(t, vstar, jmax, kt, m_run):
        k0 = pl.multiple_of(kt * tk, tk)
        key = keys_ref[t % 2, kt]
        tie_ok = jnp.where(krow + k0 < jmax, 0.0, NEG)
        mb = jnp.where(key > vstar, 0.0, jnp.where(key == vstar, tie_ok, NEG))
        mb2 = jnp.concatenate([mb, mb], axis=1)
        akt = ak_ref[pl.ds(k0, tk), :]
        vt = vt_ref[kt]
        m_next = []
        for c in range(n_pairs):
            s = _dot(akt, aqs_ref[t, :, 2 * c * tq:2 * (c + 1) * tq]) + mb2
            s_ref[c] = s
            m_next.append(jnp.maximum(m_run[c], jnp.max(s, axis=0, keepdims=True)))
        for c in range(n_pairs):
            alpha = jnp.exp2(m_run[c] - m_next[c])
            pr = jnp.exp2(s_ref[c] - m_next[c])
            acc_ref[c] = alpha * acc_ref[c] + _dot(vt, pr.astype(BF16))
        return tuple(m_next)

    def finalize(t):
        for c in range(n_pairs):
            a = acc_ref[c]
            o = a[:2 * HEAD_DIM] / a[2 * HEAD_DIM:2 * HEAD_DIM + 1]
            o_ref[t * tq:(t + 1) * tq, c * LANES:(c + 1) * LANES] = (
                jnp.where(group0, o[:, :tq], o[:, tq:]).T.astype(BF16))

    info = [tile_info(t) for t in range(n_sub)]
    for t in range(n_sub):
        build_stacks(t)

    def score_only(t, lo, hi):
        def body(kt, carry):
            score_tile(t, info[t][1], kt)
            return carry
        lax.fori_loop(lo, hi, body, 0)

    score_only(0, 0, info[0][0])
    for t in range(n_sub):
        nkt = info[t][0]
        clear_unused_planes(t, nkt)
        vstar, jmax = search(t, nkt)
        acc_ref[...] = jnp.zeros(acc_ref.shape, F32)
        m0 = tuple(jnp.full((1, 2 * tq), NEG, F32) for _ in range(n_pairs))
        if t + 1 < n_sub:
            def fused(kt, m_run, t=t, vstar=vstar, jmax=jmax):
                score_tile(t + 1, info[t + 1][1], kt)
                return attend_tile(t, vstar, jmax, kt, m_run)
            lax.fori_loop(0, nkt, fused, m0)
            score_only(t + 1, nkt, info[t + 1][0])
        else:
            lax.fori_loop(0, nkt, lambda kt, m_run, t=t, vstar=vstar, jmax=jmax:
                          attend_tile(t, vstar, jmax, kt, m_run), m0)
        finalize(t)


def _dsa(aq, iq, iw, ik, ak, vt, *, n_batch, nq, tk, q_off, l_valid, topk, keys_3d):
    tq = LANES
    nkt_max = vt.shape[1]
    l_pad = nkt_max * tk
    assert l_valid <= l_pad and vt.shape[2:] == (V_ROWS, tk) and tk % GROUP_KEYS == 0
    n_sub = 4 if nq % 4 == 0 else (2 if nq % 2 == 0 else 1)
    steps = nq // n_sub
    qspec = lambda width: pl.BlockSpec((n_sub * tq, width), lambda b, j: (b * steps + j, 0))
    if keys_3d:
        kspec = pl.BlockSpec((None, l_pad, LANES), lambda b, j: (b, 0, 0))
    else:
        kspec = pl.BlockSpec((l_pad, LANES), lambda b, j: (b, 0))
    kern = functools.partial(_dsa_kernel, tk=tk, q_off=q_off, l_valid=l_valid, topk=topk,
                             idx_bits=int(l_pad).bit_length(), n_sub=n_sub)
    n_slots = IDX_HEADS
    return pl.pallas_call(
        kern,
        grid=(n_batch, steps),
        in_specs=[qspec(512), qspec(512), qspec(LANES), kspec, kspec,
                  pl.BlockSpec((None, nkt_max, V_ROWS, tk), lambda b, j: (b, 0, 0, 0))],
        out_specs=qspec(512),
        out_shape=jax.ShapeDtypeStruct((n_batch * nq * tq, 512), BF16),
        scratch_shapes=[
            pltpu.VMEM((n_sub, LANES, n_slots * tq), BF16),
            pltpu.VMEM((n_sub, LANES, n_slots * tq), BF16),
            pltpu.VMEM((n_sub, LANES, tq), F32),
            pltpu.VMEM((2, nkt_max, tk, tq), jnp.int32),
            pltpu.VMEM((32, 2, l_pad // GROUP_KEYS, SUBLANES, tq), jnp.int32),
            pltpu.VMEM((n_slots // 2, tk, 2 * tq), F32),
            pltpu.VMEM((n_slots // 2, V_ROWS, 2 * tq), F32),
        ],
        compiler_params=_params(2),
        name="dsa_attention",
    )(aq, iq, iw, ik, ak, vt)


def _fill_band(buf, prev_ref, cur_ref, n_front, rows):
    buf[0:n_front, :] = prev_ref[...].astype(BF16)
    buf[n_front:n_front + rows, :] = cur_ref[...]
    buf[n_front + rows:, :] = jnp.zeros((buf.shape[0] - n_front - rows, buf.shape[1]), BF16)


def _band_call(kern, name, q, kprev, kcur, vprev, vcur, consts, *, n_batch, t, n_chunks, n_front, prev_3d):
    rows = n_chunks * CHUNK
    nq = t // rows
    wq = q.shape[1]
    wkv = kcur.shape[1]
    assert t % rows == 0
    if prev_3d:
        pspec = pl.BlockSpec((None, n_front, wkv), lambda b, j: (b, 0, 0))
    else:
        assert rows % n_front == 0 and t % n_front == 0
        per_seq, per_step = t // n_front, rows // n_front
        pspec = pl.BlockSpec((n_front, wkv), lambda b, j: (jnp.maximum(b * per_seq + j * per_step - 1, 0), 0))
    cspec = pl.BlockSpec((rows, wkv), lambda b, j: (b * nq + j, 0))
    qspec = pl.BlockSpec((rows, wq), lambda b, j: (b * nq + j, 0))
    return pl.pallas_call(
        kern,
        grid=(n_batch, nq),
        in_specs=[qspec, pspec, cspec, pspec, cspec] + [_const_spec(c.shape) for c in consts],
        out_specs=qspec,
        out_shape=jax.ShapeDtypeStruct((n_batch * t, wq), BF16),
        scratch_shapes=[pltpu.VMEM((n_front + rows + CHUNK, wkv), BF16)] * 2,
        compiler_params=_params(2),
        name=name,
    )(q, kprev, kcur, vprev, vcur, *consts)


def _band_b_kernel(q_ref, kp_ref, kc_ref, vp_ref, vc_ref, bias_ref, o_ref, kbuf, vbuf,
                   *, n_chunks, n_front, front_valid):
    j = pl.program_id(1)
    rows = n_chunks * CHUNK
    bwp = n_front + 2 * CHUNK
    _fill_band(kbuf, kp_ref, kc_ref, n_front, rows)
    _fill_band(vbuf, vp_ref, vc_ref, n_front, rows)
    half0 = lax.broadcasted_iota(jnp.int32, (CHUNK, LANES), 1) < HEAD_DIM
    ucol = lax.broadcasted_iota(jnp.int32, (1, bwp), 1)
    ones = jnp.ones((bwp, LANES), BF16)
    for ci in range(n_chunks):
        r0 = ci * CHUNK
        if not front_valid:
            front_mask = jnp.where(ucol >= jnp.where(j > 0, 0, n_front - r0), 0.0, NEG)
        for cp in range(B_HEADS // 2):
            cols = slice(cp * LANES, (cp + 1) * LANES)
            qq = q_ref[r0:r0 + CHUNK, cols]
            lhs = jnp.concatenate([jnp.where(half0, qq, jnp.zeros_like(qq)),
                                   jnp.where(half0, jnp.zeros_like(qq), qq)], axis=0)
            s = _dot_nt(lhs, kbuf[r0:r0 + bwp, cols]) + bias_ref[cp]
            if not front_valid:
                s = s + front_mask
            m = jnp.max(s, axis=-1, keepdims=True)
            e_ = jnp.exp(s - m).astype(BF16)
            o = _dot(e_, jnp.concatenate([vbuf[r0:r0 + bwp, cols], ones], axis=1))
            o = o[:, :LANES] / o[:, LANES:]
            o_ref[r0:r0 + CHUNK, cols] = jnp.where(half0, o[:CHUNK], o[CHUNK:]).astype(BF16)


def _band_b(q, kprev, kcur, vprev, vcur, bias, *, n_batch, t, n_chunks, front_valid, prev_3d):
    n_front = B_LEFT_CHUNKS * CHUNK
    kern = functools.partial(_band_b_kernel, n_chunks=n_chunks, n_front=n_front, front_valid=front_valid)
    return _band_call(kern, "band_b_attention", q, kprev, kcur, vprev, vcur, [bias],
                      n_batch=n_batch, t=t, n_chunks=n_chunks, n_front=n_front, prev_3d=prev_3d)


def _band_c_kernel(q_ref, kp_ref, kc_ref, vp_ref, vc_ref, base_ref, o_ref, kbuf, vbuf,
                   *, n_chunks, n_front, front_valid):
    j = pl.program_id(1)
    rows = n_chunks * CHUNK
    bw = n_front + CHUNK
    n_slots = C_HEADS
    _fill_band(kbuf, kp_ref, kc_ref, n_front, rows)
    _fill_band(vbuf, vp_ref, vc_ref, n_front, rows)
    lane = lax.broadcasted_iota(jnp.int32, (CHUNK, LANES), 1)
    half0 = lane < HEAD_DIM
    in_band = lax.broadcasted_iota(jnp.int32, (n_slots * CHUNK, MXU_COLS), 1) < bw
    ucol = lax.broadcasted_iota(jnp.int32, (1, MXU_COLS), 1)
    ones_band = jnp.ones((bw, LANES), BF16)
    tail = jnp.concatenate([jnp.zeros((MXU_COLS - bw, LANES), BF16), jnp.ones((MXU_COLS - bw, LANES), BF16)], axis=1)
    for ci in range(n_chunks):
        r0 = ci * CHUNK
        kb = kbuf[r0:r0 + MXU_COLS, :]
        vb = jnp.concatenate([jnp.concatenate([vbuf[r0:r0 + bw, :], ones_band], axis=1), tail], axis=0)
        parts = []
        for cb in range(n_slots // 2):
            qb = q_ref[r0:r0 + CHUNK, cb * LANES:(cb + 1) * LANES]
            parts.append(jnp.where(half0, qb, jnp.zeros_like(qb)))
            parts.append(jnp.where(half0, jnp.zeros_like(qb), qb))
        s = _dot_nt(jnp.concatenate(parts, axis=0), kb)
        if not front_valid:
            s = s + jnp.where(ucol >= jnp.where(j > 0, 0, n_front - r0), 0.0, NEG)
        s = jnp.where(in_band, s, base_ref[...])
        m = jnp.max(s, axis=-1, keepdims=True)
        e_ = jnp.exp(s - m)
        o = _dot(e_.astype(BF16), vb)
        o = o[:, :LANES] / o[:, LANES:]
        for cb in range(n_slots // 2):
            o0 = o[(2 * cb) * CHUNK:(2 * cb + 1) * CHUNK]
            o1 = o[(2 * cb + 1) * CHUNK:(2 * cb + 2) * CHUNK]
            o_ref[r0:r0 + CHUNK, cb * LANES:(cb + 1) * LANES] = jnp.where(half0, o0, o1).astype(BF16)


def _band_c(q, kprev, kcur, vprev, vcur, base, *, n_batch, t, n_chunks, front_valid, prev_3d):
    n_front = C_LEFT_CHUNKS * CHUNK
    kern = functools.partial(_band_c_kernel, n_chunks=n_chunks, n_front=n_front, front_valid=front_valid)
    return _band_call(kern, "band_c_attention", q, kprev, kcur, vprev, vcur, [base],
                      n_batch=n_batch, t=t, n_chunks=n_chunks, n_front=n_front, prev_3d=prev_3d)


def _rope_tables(pos):
    half = HEAD_DIM // 2
    inv_freq = ROPE_THETA ** (-jnp.arange(half, dtype=F32) / half)
    ang = pos.astype(F32)[:, None] * inv_freq[None, :]
    cos = jnp.cos(ang)
    sin = jnp.sin(ang)
    return (jnp.concatenate([cos, cos, cos, cos], axis=1), jnp.concatenate([-sin, sin, -sin, sin], axis=1))


def _perm_heads_cols(w, perm):
    d = w.shape[0]
    return w.reshape(d, len(perm), HEAD_DIM)[:, np.asarray(perm)].reshape(d, len(perm) * HEAD_DIM)


def _perm_heads_rows(w, perm):
    d = w.shape[1]
    return w.reshape(len(perm), HEAD_DIM, d)[np.asarray(perm)].reshape(len(perm) * HEAD_DIM, d)


def _even_weights(w_in, a_qn, a_kn, i_kn, b_qn, b_kn):
    d = w_in.shape[0]
    sizes = (A_HEADS * HEAD_DIM, A_KV_HEADS * HEAD_DIM, A_KV_HEADS * HEAD_DIM, IDX_HEADS * IDX_DIM, IDX_DIM,
             IDX_HEADS, B_HEADS * HEAD_DIM, B_HEADS * HEAD_DIM, B_HEADS * HEAD_DIM)
    aq, ak, av, iq, ik, iw, bq, bk, bv = jnp.split(w_in, np.cumsum(sizes)[:-1].tolist(), axis=1)
    w = jnp.concatenate([_perm_heads_cols(aq, A_PERM), iq, bq, bk, bv, ak, av, ik, ik,
                         iw, jnp.zeros((d, LANES - IDX_HEADS), w_in.dtype)], axis=1).astype(BF16)
    one = lambda n: jnp.ones((n,), F32)
    gains = jnp.concatenate([jnp.tile(a_qn, A_HEADS), one(512), jnp.tile(b_qn, B_HEADS), jnp.tile(b_kn, B_HEADS),
                             one(512), jnp.tile(a_kn, A_KV_HEADS), one(128), jnp.tile(i_kn, 2), one(128)])
    return w, gains[None, :].astype(F32)


def _odd_weights(w_in, c_qn, c_kn):
    q, k, v = jnp.split(w_in, [C_HEADS * HEAD_DIM, (C_HEADS + C_KV_HEADS) * HEAD_DIM], axis=1)
    w = jnp.concatenate([_perm_heads_cols(q, C_PERM), k, v], axis=1).astype(BF16)
    gains = jnp.concatenate([jnp.tile(c_qn, C_HEADS), jnp.tile(c_kn, C_KV_HEADS), jnp.ones((128,), F32)])
    return w, gains[None, :].astype(F32)


def _pad_rows(x, n, front):
    pad = [(0, 0)] * x.ndim
    pad[1] = (n, 0) if front else (0, n)
    return jnp.pad(x, pad)


def _values_transposed(v, tk):
    nbt, l_pad, _ = v.shape
    vt = jnp.swapaxes(v.reshape(nbt, l_pad // tk, tk, LANES), 2, 3)
    return jnp.concatenate([vt, jnp.ones((nbt, l_pad // tk, BF16_ROWS, tk), v.dtype)], axis=2)


def _band_b_bias(rb):
    n_front = B_LEFT_CHUNKS * CHUNK
    bw = n_front + CHUNK
    n_flat = n_front - B_MAX_REL + CHUNK
    ext = jnp.concatenate([jnp.broadcast_to(rb[:, :1], (rb.shape[0], n_flat)), rb[:, 1:B_MAX_REL + CHUNK]], axis=1)
    period = ext.shape[1]
    rolled = jnp.roll(ext, -(CHUNK - 1), axis=1)
    bias = jnp.tile(rolled, (1, CHUNK))[:, :CHUNK * (period - 1)].reshape(rb.shape[0], CHUNK, period - 1)[:, :, :bw]
    bias = jnp.pad(bias, ((0, 0), (0, 0), (0, CHUNK)), constant_values=NEG)
    return bias.reshape(B_HEADS // 2, 2 * CHUNK, bw + CHUNK)


def kernel(x_prompt, x_sample, cache_a_k, cache_a_v, cache_a_kidx, cache_b_k, cache_b_v, cache_c_k, cache_c_v, norm_mix, norm_ffn, w_in_even, w_out_even, a_q_norm, a_k_norm, idx_k_norm, b_q_norm, b_k_norm, b_rel_bias, w_in_odd, w_out_odd, c_q_norm, c_k_norm, c_sinks, w_ffn_in, w_ffn_out):
    nb, seq, d = x_prompt.shape
    ns, t_new, _ = x_sample.shape
    past = cache_a_k.shape[2]
    depth = norm_mix.shape[0]
    assert seq % LANES == 0 and t_new == CHUNK and past % CHUNK == 0
    assert cache_b_k.shape[2] == B_LEFT_CHUNKS * CHUNK and cache_c_k.shape[2] == C_LEFT_CHUNKS * CHUNK
    topk_p = min(TOPK_MAX, seq // 4)
    topk_s = min(TOPK_MAX, (past + t_new) // 4)
    keep_b = min(B_LEFT_CHUNKS * CHUNK, seq)
    keep_c = min(C_LEFT_CHUNKS * CHUNK, seq)
    n_front_b = B_LEFT_CHUNKS * CHUNK
    n_front_c = C_LEFT_CHUNKS * CHUNK

    hp = x_prompt.reshape(nb * seq, d)
    hs = x_sample.reshape(ns * t_new, d)
    tm_p = _row_tile(nb * seq)
    bps_p = seq // tm_p if (seq % tm_p == 0 and tm_p >= max(keep_b, keep_c)) else 1
    tm_s = _row_tile(ns * t_new)
    pos_p = jnp.tile(jnp.arange(seq, dtype=jnp.int32), max(1, tm_p // seq))
    pos_s = jnp.tile(past + jnp.arange(t_new, dtype=jnp.int32), max(1, tm_s // t_new))
    cos_p, sin_p = _rope_tables(pos_p)
    cos_s, sin_s = _rope_tables(pos_s)
    gid = np.arange(MXU_COLS) // HEAD_DIM
    bd = jnp.asarray((gid[:, None] == gid[None, :]).astype(np.float32) / HEAD_DIM, BF16)

    tk_p = 512 if seq % 512 == 0 else (256 if seq % 256 == 0 else 128)
    tk_s = 512
    l_s = past + t_new
    l_s_pad = -(-l_s // tk_s) * tk_s
    assert seq % n_front_b == 0
    ch_p = n_front_b // CHUNK

    outs = {k: [] for k in ("pa_k", "pa_v", "pa_i", "pb_k", "pb_v", "pc_k", "pc_v",
                            "sa_k", "sa_v", "sa_i", "sb_k", "sb_v", "sc_k", "sc_v")}

    def last_rows(x, keep, width):
        rows = x.shape[0] // nb
        return x.reshape(nb, rows, width)[:, rows - keep:]

    def pad_queries(x):
        return _pad_rows(x.reshape(ns, t_new, x.shape[1]), LANES - t_new, False).reshape(ns * LANES, x.shape[1])

    wgu = w_ffn_in.astype(BF16)
    wdn = w_ffn_out.astype(BF16)
    for layer in range(depth):
        li = layer // 2
        g_mix = norm_mix[layer][None, :]
        g_ffn = norm_ffn[layer][None, :]
        if layer % 2 == 0:
            w, gains = _even_weights(w_in_even[li], a_q_norm[li], a_k_norm[li], idx_k_norm[li],
                                     b_q_norm[li], b_k_norm[li])
            wo = w_out_even[li]
            wo_a = _perm_heads_rows(wo[:A_HEADS * HEAD_DIM], A_PERM).astype(BF16)
            wo_b = wo[A_HEADS * HEAD_DIM:].astype(BF16)
            bias = _band_b_bias(b_rel_bias[li].astype(F32))

            (aq, iq, bq, bk, bv, bk16, bv16, ak, av, ak16, av16, ik, ik16, iw) = _even_in(
                hp, g_mix, w, gains, bd, cos_p, sin_p, bps_p)
            vt = _values_transposed(av16.reshape(nb, seq, LANES), tk_p)
            out_a = _dsa(aq, iq, iw, ik16, ak16, vt, n_batch=nb, nq=seq // LANES, tk=tk_p,
                         q_off=0, l_valid=seq, topk=topk_p, keys_3d=False)
            out_b = _band_b(bq, bk16, bk16, bv16, bv16, bias, n_batch=nb, t=seq, n_chunks=ch_p,
                            front_valid=False, prev_3d=False)
            hp = _out_ffn(hp, [out_a, out_b], [wo_a, wo_b], g_ffn, wgu, wdn, layer)
            outs["pa_k"].append(ak.reshape(nb, seq, A_KV_HEADS, HEAD_DIM))
            outs["pa_v"].append(av.reshape(nb, seq, A_KV_HEADS, HEAD_DIM))
            outs["pa_i"].append(ik.reshape(nb, seq, IDX_DIM))
            outs["pb_k"].append(last_rows(bk, keep_b, 512).reshape(nb, keep_b, B_HEADS, HEAD_DIM))
            outs["pb_v"].append(last_rows(bv, keep_b, 512).reshape(nb, keep_b, B_HEADS, HEAD_DIM))

            (aq, iq, bq, bk, bv, bk16, bv16, ak, av, ak16, av16, ik, ik16, iw) = _even_in(
                hs, g_mix, w, gains, bd, cos_s, sin_s, 1)
            ci = cache_a_kidx[li].astype(BF16)
            ik_all = jnp.concatenate([jnp.concatenate([ci, ci], axis=-1), ik16.reshape(ns, t_new, LANES)], axis=1)
            ak_all = jnp.concatenate([cache_a_k[li].reshape(ns, past, LANES).astype(BF16),
                                      ak16.reshape(ns, t_new, LANES)], axis=1)
            av_all = jnp.concatenate([cache_a_v[li].reshape(ns, past, LANES).astype(BF16),
                                      av16.reshape(ns, t_new, LANES)], axis=1)
            ik_all, ak_all, av_all = (_pad_rows(x, l_s_pad - l_s, False) for x in (ik_all, ak_all, av_all))
            out_a = _dsa(pad_queries(aq), pad_queries(iq), pad_queries(iw), ik_all, ak_all,
                         _values_transposed(av_all, tk_s), n_batch=ns, nq=1, tk=tk_s,
                         q_off=past, l_valid=l_s, topk=topk_s, keys_3d=True)
            out_a = out_a.reshape(ns, LANES, 512)[:, :t_new].reshape(ns * t_new, 512)
            out_b = _band_b(bq, cache_b_k[li].reshape(ns, n_front_b, 512), bk16,
                            cache_b_v[li].reshape(ns, n_front_b, 512), bv16, bias, n_batch=ns, t=t_new,
                            n_chunks=1, front_valid=True, prev_3d=True)
            hs = _out_ffn(hs, [out_a, out_b], [wo_a, wo_b], g_ffn, wgu, wdn, layer)
            outs["sa_k"].append(ak.reshape(ns, t_new, A_KV_HEADS, HEAD_DIM))
            outs["sa_v"].append(av.reshape(ns, t_new, A_KV_HEADS, HEAD_DIM))
            outs["sa_i"].append(ik.reshape(ns, t_new, IDX_DIM))
            outs["sb_k"].append(bk.reshape(ns, t_new, B_HEADS, HEAD_DIM))
            outs["sb_v"].append(bv.reshape(ns, t_new, B_HEADS, HEAD_DIM))
        else:
            w, gains = _odd_weights(w_in_odd[li], c_q_norm[li], c_k_norm[li])
            wo = _perm_heads_rows(w_out_odd[li], C_PERM).astype(BF16)
            sinks = c_sinks[li].astype(F32)[np.asarray(C_PERM)]
            base = jnp.full((C_HEADS, 1, MXU_COLS), NEG, F32).at[:, 0, n_front_c + CHUNK].set(sinks)
            base = jnp.broadcast_to(base, (C_HEADS, CHUNK, MXU_COLS)).reshape(C_HEADS * CHUNK, MXU_COLS)

            q, k, v, k16, v16 = _odd_in(hp, g_mix, w, gains, bd, cos_p, sin_p, bps_p)
            out_c = _band_c(q, k16, k16, v16, v16, base, n_batch=nb, t=seq, n_chunks=ch_p,
                            front_valid=False, prev_3d=False)
            hp = _out_ffn(hp, [out_c], [wo], g_ffn, wgu, wdn, layer)
            outs["pc_k"].append(last_rows(k, keep_c, LANES).reshape(nb, keep_c, C_KV_HEADS, HEAD_DIM))
            outs["pc_v"].append(last_rows(v, keep_c, LANES).reshape(nb, keep_c, C_KV_HEADS, HEAD_DIM))

            q, k, v, k16, v16 = _odd_in(hs, g_mix, w, gains, bd, cos_s, sin_s, 1)
            out_c = _band_c(q, cache_c_k[li].reshape(ns, n_front_c, LANES), k16,
                            cache_c_v[li].reshape(ns, n_front_c, LANES), v16, base, n_batch=ns, t=t_new,
                            n_chunks=1, front_valid=True, prev_3d=True)
            hs = _out_ffn(hs, [out_c], [wo], g_ffn, wgu, wdn, layer)
            outs["sc_k"].append(k.reshape(ns, t_new, C_KV_HEADS, HEAD_DIM))
            outs["sc_v"].append(v.reshape(ns, t_new, C_KV_HEADS, HEAD_DIM))

    st = lambda name: jnp.stack(outs[name])
    return (hp.reshape(nb, seq, d), hs.reshape(ns, t_new, d),
            st("pa_k"), st("pa_v"), st("pa_i"), st("pb_k"), st("pb_v"), st("pc_k"), st("pc_v"),
            st("sa_k"), st("sa_v"), st("sa_i"), st("sb_k"), st("sb_v"), st("sc_k"), st("sc_v"))
```

```python
import functools
import math

import numpy as np
import jax
import jax.numpy as jnp
from jax import lax
from jax.experimental import pallas as pl
from jax.experimental.pallas import tpu as pltpu

CHUNK = 64
HEAD_DIM = 64
EPS = 1e-6
ROPE_THETA = 10000.0
A_HEADS = 8
A_KV_HEADS = 2
IDX_HEADS = 8
IDX_DIM = 64
IDX_W_SCALE = (IDX_HEADS * IDX_DIM) ** -0.5
TOPK_MAX = 256
B_HEADS = 8
B_LEFT_CHUNKS = 8
B_MAX_REL = 128
C_HEADS = 16
C_KV_HEADS = 2
C_LEFT_CHUNKS = 2
QK_SCALE = HEAD_DIM ** -0.5
QK_SCALE_LOG2 = QK_SCALE * math.log2(math.e)

LANES = 128
SUBLANES = 8
BF16_ROWS = 16
MXU_COLS = 256
VMEM_LIMIT = 56 * 1024 * 1024

NEG = -1e30
INT_MIN = np.int32(-2 ** 31)
F32 = jnp.float32
BF16 = jnp.bfloat16
V_ROWS = 2 * HEAD_DIM + BF16_ROWS

A_PERM = tuple(c + (A_HEADS // 2) * p for c in range(A_HEADS // 2) for p in range(2))
C_PERM = tuple(c + (C_HEADS // 2) * p for c in range(C_HEADS // 2) for p in range(2))


def _dot(a, b):
    return jnp.dot(a, b, preferred_element_type=F32)


def _dot_nt(a, b):
    return lax.dot_general(a, b, (((1,), (1,)), ((), ())), preferred_element_type=F32)


def _row_tile(n):
    for t in (512, 256, 128, 64):
        if n % t == 0:
            return t
    raise ValueError(f"row count {n} is not a multiple of {CHUNK}")


def _const_spec(shape):
    nd = len(shape)
    return pl.BlockSpec(shape, lambda *_: (0,) * nd)


def _params(n_axes):
    return pltpu.CompilerParams(dimension_semantics=("arbitrary",) * n_axes,
                                vmem_limit_bytes=VMEM_LIMIT)


def _group_ms(hb, bd):
    sq = (hb * hb).astype(BF16)
    w = hb.shape[1]
    parts = [_dot(sq[:, i:i + MXU_COLS], bd) for i in range(0, w, MXU_COLS)]
    return parts[0] if len(parts) == 1 else jnp.concatenate(parts, axis=1)


def _rope_blocks(y, cos, sin, first_half):
    out = []
    for i in range(0, y.shape[1], LANES):
        yb = y[:, i:i + LANES]
        sw = jnp.where(first_half, pltpu.roll(yb, LANES - 32, 1), pltpu.roll(yb, 32, 1))
        out.append(yb * cos + sw * sin)
    return out[0] if len(out) == 1 else jnp.concatenate(out, axis=1)


def _normed_input(x_ref, g_ref):
    x = x_ref[...]
    ms = jnp.mean(x * x, axis=-1, keepdims=True)
    return (x * lax.rsqrt(ms + EPS) * g_ref[...]).astype(BF16)


def _even_in_kernel(x_ref, g_ref, w_ref, gain_ref, bd_ref, cos_ref, sin_ref,
                    aq_ref, iq_ref, bq_ref, bk_ref, bv_ref, bk16_ref, bv16_ref,
                    ak_ref, av_ref, ak16_ref, av16_ref, ik_ref, ik16_ref, iw_ref, vt_ref):
    tm = x_ref.shape[0]
    xn = _normed_input(x_ref, g_ref)
    bd = bd_ref[...]
    cos = cos_ref[...]
    sin = sin_ref[...]
    first_half = (lax.broadcasted_iota(jnp.int32, (tm, LANES), 1) % HEAD_DIM) < HEAD_DIM // 2

    def proj(c0, width):
        return _dot(xn, w_ref[:, c0:c0 + width])

    def normed(h, c0):
        return h * lax.rsqrt(_group_ms(h, bd) + EPS) * gain_ref[:, c0:c0 + h.shape[1]]

    h = proj(0, 512)
    aq_ref[...] = (_rope_blocks(normed(h, 0), cos, sin, first_half) * QK_SCALE_LOG2).astype(BF16)
    h = proj(512, 512)
    iq_ref[...] = _rope_blocks(h, cos, sin, first_half).astype(BF16)
    h = proj(1024, 512)
    bq_ref[...] = (normed(h, 1024) * QK_SCALE).astype(BF16)
    h = normed(proj(1536, 512), 1536)
    bk_ref[...] = h
    bk16_ref[...] = h.astype(BF16)
    h = proj(2048, 512)
    bv_ref[...] = h
    bv16_ref[...] = h.astype(BF16)
    h = proj(2560, 256)
    k = _rope_blocks(normed(h, 2560)[:, :LANES], cos, sin, first_half)
    ak_ref[...] = k
    ak16_ref[...] = k.astype(BF16)
    v = h[:, LANES:]
    av_ref[...] = v
    av16_ref[...] = v.astype(BF16)
    vt_ref[0:2 * HEAD_DIM, :] = v.T.astype(BF16)
    vt_ref[2 * HEAD_DIM:, :] = jnp.ones((BF16_ROWS, tm), BF16)
    h = proj(2816, 256)
    k = _rope_blocks(normed(h, 2816)[:, :LANES], cos, sin, first_half)
    ik_ref[...] = k[:, :IDX_DIM]
    ik16_ref[...] = k.astype(BF16)
    iw_ref[...] = h[:, LANES:] * IDX_W_SCALE


def _in_proj_call(kern, name, widths_dtypes, x, g, w, gains, bd, cos, sin, tail_only, blocks_per_seq,
                  values_transposed=False):
    n, d = x.shape
    tm = _row_tile(n)
    assert cos.shape[0] % tm == 0 and (n // tm) % blocks_per_seq == 0
    n_tab = cos.shape[0] // tm
    row = lambda width: pl.BlockSpec((tm, width), lambda i: (i, 0))
    tail = lambda width: pl.BlockSpec((tm, width), lambda i: (i // blocks_per_seq, 0))
    tab = pl.BlockSpec((tm, LANES), lambda i: (i % n_tab, 0))
    out_rows = [n // blocks_per_seq if o in tail_only else n for o in range(len(widths_dtypes))]
    out_specs = [tail(wd) if o in tail_only else row(wd) for o, (wd, _) in enumerate(widths_dtypes)]
    out_shape = [jax.ShapeDtypeStruct((r, wd), dt) for r, (wd, dt) in zip(out_rows, widths_dtypes)]
    if values_transposed:
        out_specs.append(pl.BlockSpec((None, V_ROWS, tm), lambda i: (i, 0, 0)))
        out_shape.append(jax.ShapeDtypeStruct((n // tm, V_ROWS, tm), BF16))
    return pl.pallas_call(
        kern,
        grid=(n // tm,),
        in_specs=[row(d), _const_spec((1, d)), _const_spec(w.shape), _const_spec(gains.shape),
                  _const_spec(bd.shape), tab, tab],
        out_specs=out_specs,
        out_shape=out_shape,
        compiler_params=_params(1),
        name=name,
    )(x, g, w, gains, bd, cos, sin)


def _even_in(x, g, w, gains, bd, cos, sin, blocks_per_seq):
    widths_dtypes = [(512, BF16), (512, BF16), (512, BF16), (512, F32), (512, F32), (512, BF16), (512, BF16),
                     (128, F32), (128, F32), (128, BF16), (128, BF16), (IDX_DIM, F32), (128, BF16), (128, F32)]
    return _in_proj_call(_even_in_kernel, "even_in_proj", widths_dtypes, x, g, w, gains, bd, cos, sin,
                         tail_only=(3, 4), blocks_per_seq=blocks_per_seq, values_transposed=True)


def _odd_in_kernel(x_ref, g_ref, w_ref, gain_ref, bd_ref, cos_ref, sin_ref,
                   q_ref, k_ref, v_ref, k16_ref, v16_ref):
    tm = x_ref.shape[0]
    xn = _normed_input(x_ref, g_ref)
    bd = bd_ref[...]
    cos = cos_ref[...]
    sin = sin_ref[...]
    first_half = (lax.broadcasted_iota(jnp.int32, (tm, LANES), 1) % HEAD_DIM) < HEAD_DIM // 2
    for c0 in (0, 512):
        h = _dot(xn, w_ref[:, c0:c0 + 512])
        h = h * lax.rsqrt(_group_ms(h, bd) + EPS) * gain_ref[:, c0:c0 + 512]
        q_ref[:, c0:c0 + 512] = (_rope_blocks(h, cos, sin, first_half) * QK_SCALE).astype(BF16)
    h = _dot(xn, w_ref[:, 1024:1280])
    hn = h * lax.rsqrt(_group_ms(h, bd) + EPS) * gain_ref[:, 1024:1280]
    k = _rope_blocks(hn[:, :LANES], cos, sin, first_half)
    k_ref[...] = k
    k16_ref[...] = k.astype(BF16)
    v = h[:, LANES:]
    v_ref[...] = v
    v16_ref[...] = v.astype(BF16)


def _odd_in(x, g, w, gains, bd, cos, sin, blocks_per_seq):
    widths_dtypes = [(1024, BF16), (128, F32), (128, F32), (128, BF16), (128, BF16)]
    return _in_proj_call(_odd_in_kernel, "odd_in_proj", widths_dtypes, x, g, w, gains, bd, cos, sin,
                         tail_only=(1, 2), blocks_per_seq=blocks_per_seq)


def _out_ffn_kernel(*refs, n_attn, d_ff, ff_chunk):
    h_ref = refs[0]
    attn_refs = refs[1:1 + n_attn]
    wo_refs = refs[1 + n_attn:1 + 2 * n_attn]
    g_ref, wgu_ref, wdn_ref, o_ref, yn_ref = refs[1 + 2 * n_attn:]
    o_ref[...] = h_ref[...]
    for a_ref, wo_ref in zip(attn_refs, wo_refs):
        o_ref[...] += _dot(a_ref[...], wo_ref[...])
    y = o_ref[...]
    ms = jnp.mean(y * y, axis=-1, keepdims=True)
    yn_ref[...] = (y * lax.rsqrt(ms + EPS) * g_ref[...]).astype(BF16)
    for c0 in range(0, d_ff, ff_chunk):
        gate = _dot(yn_ref[...], wgu_ref[:, c0:c0 + ff_chunk])
        up = _dot(yn_ref[...], wgu_ref[:, d_ff + c0:d_ff + c0 + ff_chunk])
        act = (gate * (1.0 / (1.0 + jnp.exp(-gate))) * up).astype(BF16)
        o_ref[...] += _dot(act, wdn_ref[c0:c0 + ff_chunk, :])


def _out_ffn(h, attns, wos, g, wgu, wdn, layer):
    n, d = h.shape
    tm = _row_tile(n)
    d_ff = wdn.shape[1]
    row = lambda width: pl.BlockSpec((tm, width), lambda i: (i, 0))
    layer_spec = lambda w: pl.BlockSpec((None,) + w.shape[1:], lambda i: (layer, 0, 0))
    kern = functools.partial(_out_ffn_kernel, n_attn=len(attns), d_ff=d_ff, ff_chunk=MXU_COLS)
    return pl.pallas_call(
        kern,
        grid=(n // tm,),
        in_specs=[row(d)] + [row(a.shape[1]) for a in attns] + [_const_spec(w.shape) for w in wos]
                 + [_const_spec((1, d)), layer_spec(wgu), layer_spec(wdn)],
        out_specs=row(d),
        out_shape=jax.ShapeDtypeStruct((n, d), F32),
        scratch_shapes=[pltpu.VMEM((tm, d), BF16)],
        compiler_params=_params(1),
        name="out_proj_ffn",
    )(h, *attns, *wos, g, wgu, wdn)


GROUP_KEYS = 32 * SUBLANES


def _bit_transpose_32(rows):
    a = list(rows)
    j = 16
    m = 0x0000FFFF
    while j:
        k = 0
        while k < 32:
            t = (a[k] ^ lax.shift_right_logical(a[k + j], jnp.int32(j))) & jnp.int32(m)
            a[k] = a[k] ^ t
            a[k + j] = a[k + j] ^ lax.shift_left(t, jnp.int32(j))
            k = (k + j + 1) & ~j
        j >>= 1
        if j:
            m = (m ^ (m << j)) & 0xFFFFFFFF
    return a


def _dsa_kernel(aq_ref, iq_ref, iw_ref, ik_ref, ak_ref, vt_ref, o_ref,
                iqs_ref, aqs_ref, w_ref, keys_ref, planes_ref, s_ref, acc_ref,
                *, tk, q_off, l_valid, topk, idx_bits, n_sub):
    tq = LANES
    n_slots = IDX_HEADS
    n_pairs = n_slots // 2
    n_chains = 4
    nkt_max = keys_ref.shape[1]
    groups_per_tile = tk // GROUP_KEYS
    n_groups = nkt_max * groups_per_tile
    jj = pl.program_id(1)
    krow = lax.broadcasted_iota(jnp.int32, (tk, tq), 0)
    group0 = lax.broadcasted_iota(jnp.int32, (2 * HEAD_DIM, tq), 0) < HEAD_DIM
    idx_all = jnp.int32(2 ** idx_bits - 1)

    def tile_info(t):
        qpos0 = q_off + (jj * n_sub + t) * tq
        n_adm_max = jnp.minimum(((qpos0 + tq - 1) // CHUNK + 1) * CHUNK, l_valid)
        qcol = qpos0 + lax.broadcasted_iota(jnp.int32, (1, tq), 1)
        return (n_adm_max + tk - 1) // tk, jnp.minimum((qcol // CHUNK + 1) * CHUNK, l_valid)

    def build_stacks(t):
        rows = slice(t * tq, (t + 1) * tq)
        for c in range(n_pairs):
            iqt = iq_ref[rows, c * LANES:(c + 1) * LANES].astype(F32).T
            aqt = aq_ref[rows, c * LANES:(c + 1) * LANES].astype(F32).T
            for p in range(2):
                n = 2 * c + p
                keep = group0 if p == 0 else jnp.logical_not(group0)
                iqs_ref[t, :, n * tq:(n + 1) * tq] = jnp.where(keep, iqt, 0.0).astype(BF16)
                aqs_ref[t, :, n * tq:(n + 1) * tq] = jnp.where(keep, aqt, 0.0).astype(BF16)
        w_ref[t] = iw_ref[rows, :].T

    def score_tile(t, klim, kt):
        par = t % 2
        k0 = pl.multiple_of(kt * tk, tk)
        ikt = ik_ref[pl.ds(k0, tk), :]
        score = jnp.zeros((tk, tq), F32)
        for c in range(n_pairs):
            s = _dot(ikt, iqs_ref[t, :, 2 * c * tq:2 * (c + 1) * tq])
            for p in range(2):
                score = score + jnp.maximum(s[:, p * tq:(p + 1) * tq], 0.0) * w_ref[t, 2 * c + p:2 * c + p + 1, :]
        bits = lax.bitcast_convert_type(score, jnp.int32)
        key = bits ^ ((bits >> 31) & jnp.int32(0x7FFFFFFF))
        key = jnp.where(key == -1, 0, key)
        key = jnp.where(krow + k0 < klim, key, INT_MIN)
        keys_ref[par, kt] = key
        ukey = key ^ INT_MIN
        for g in range(groups_per_tile):
            base = g * GROUP_KEYS
            planes = _bit_transpose_32([ukey[base + SUBLANES * i:base + SUBLANES * (i + 1), :] for i in range(32)])
            for b in range(32):
                planes_ref[b, par, kt * groups_per_tile + g] = planes[b]

    def clear_unused_planes(t, nkt):
        zero = jnp.zeros((SUBLANES, tq), jnp.int32)

        def body(kt, carry):
            for g in range(groups_per_tile):
                for b in range(32):
                    planes_ref[b, t % 2, kt * groups_per_tile + g] = zero
            return carry
        lax.fori_loop(nkt, nkt_max, body, 0)

    def popcount_rows(words):
        pcs = [lax.population_count(w) for w in words]
        chains = [sum(pcs[c::n_chains][1:], pcs[c]) for c in range(min(n_chains, len(pcs)))]
        return jnp.sum(sum(chains[1:], chains[0]).astype(F32), axis=0, keepdims=True)

    def search(t, nkt):
        par = t % 2

        def count(indicator):
            def body(kt, acc):
                ind = indicator(keys_ref[par, kt], kt)
                return acc + jnp.sum(ind.reshape(tk // (n_chains * SUBLANES), n_chains, SUBLANES, tq), axis=0)
            acc = lax.fori_loop(0, nkt, body, jnp.zeros((n_chains, SUBLANES, tq), F32))
            return jnp.sum(jnp.sum(acc, axis=0), axis=0, keepdims=True)

        def search_bit(i, carry):
            ans, n_gt, und = carry
            plane = planes_ref[i, par]
            hit = [und[g] & plane[g] for g in range(n_groups)]
            total = n_gt + popcount_rows(hit)
            take = total >= topk
            takem = jnp.where(take, jnp.int32(-1), jnp.int32(0))
            ans = jnp.where(take, ans | jnp.left_shift(jnp.int32(1), 31 - i), ans)
            n_gt = jnp.where(take, n_gt, total)
            und = tuple((und[g] ^ hit[g]) ^ (und[g] & takem) for g in range(n_groups))
            return ans, n_gt, und

        live_groups = nkt * groups_per_tile
        und0 = tuple(jnp.full((SUBLANES, tq), jnp.where(g < live_groups, jnp.int32(-1), jnp.int32(0)), jnp.int32)
                     for g in range(n_groups))
        ans, c_gt, und = lax.fori_loop(
            0, 32, search_bit, (jnp.zeros((1, tq), jnp.int32), jnp.zeros((1, tq), F32), und0))
        vstar = ans ^ INT_MIN
        c_ge = c_gt + popcount_rows(und)
        need = topk - c_gt
        has_tie = jnp.where(c_ge > topk, jnp.where(vstar > INT_MIN, 1.0, 0.0), 0.0)

        def tie_search():
            def tie_bit(i, jmax):
                cand = jmax | jnp.left_shift(jnp.int32(1), idx_bits - 1 - i)
                cnt = count(lambda k, kt: jnp.where(k == vstar, jnp.where(krow + kt * tk < cand, 1.0, 0.0), 0.0))
                return jnp.where(cnt <= need, cand, jmax)
            return lax.fori_loop(0, idx_bits, tie_bit, jnp.zeros((1, tq), jnp.int32))

        jmax = lax.cond(jnp.max(has_tie) > 0.0, tie_search, lambda: jnp.full((1, tq), idx_all, jnp.int32))
        return vstar, jnp.where(vstar == INT_MIN, 0, jmax)

    def attend_tile(t, vstar, jmax, kt, m_run):
        k0 = pl.multiple_of(kt * tk, tk)
        key = keys_ref[t % 2, kt]
        tie_ok = jnp.where(krow + k0 < jmax, 0.0, NEG)
        mb = jnp.where(key > vstar, 0.0, jnp.where(key == vstar, tie_ok, NEG))
        mb2 = jnp.concatenate([mb, mb], axis=1)
        akt = ak_ref[pl.ds(k0, tk), :]
        vt = vt_ref[kt]
        m_next = []
        for c in range(n_pairs):
            s = _dot(akt, aqs_ref[t, :, 2 * c * tq:2 * (c + 1) * tq]) + mb2
            s_ref[c] = s
            m_next.append(jnp.maximum(m_run[c], jnp.max(s, axis=0, keepdims=True)))
        for c in range(n_pairs):
            alpha = jnp.exp2(m_run[c] - m_next[c])
            pr = jnp.exp2(s_ref[c] - m_next[c])
            acc_ref[c] = alpha * acc_ref[c] + _dot(vt, pr.astype(BF16))
        return tuple(m_next)

    def finalize(t):
        for c in range(n_pairs):
            a = acc_ref[c]
            o = a[:2 * HEAD_DIM] / a[2 * HEAD_DIM:2 * HEAD_DIM + 1]
            o_ref[t * tq:(t + 1) * tq, c * LANES:(c + 1) * LANES] = (
                jnp.where(group0, o[:, :tq], o[:, tq:]).T.astype(BF16))

    info = [tile_info(t) for t in range(n_sub)]
    for t in range(n_sub):
        build_stacks(t)

    def score_only(t, lo, hi):
        def body(kt, carry):
            score_tile(t, info[t][1], kt)
            return carry
        lax.fori_loop(lo, hi, body, 0)

    score_only(0, 0, info[0][0])
    for t in range(n_sub):
        nkt = info[t][0]
        clear_unused_planes(t, nkt)
        vstar, jmax = search(t, nkt)
        acc_ref[...] = jnp.zeros(acc_ref.shape, F32)
        m0 = tuple(jnp.full((1, 2 * tq), NEG, F32) for _ in range(n_pairs))
        if t + 1 < n_sub:
            def fused(kt, m_run, t=t, vstar=vstar, jmax=jmax):
                score_tile(t + 1, info[t + 1][1], kt)
                return attend_tile(t, vstar, jmax, kt, m_run)
            lax.fori_loop(0, nkt, fused, m0)
            score_only(t + 1, nkt, info[t + 1][0])
        else:
            lax.fori_loop(0, nkt, lambda kt, m_run, t=t, vstar=vstar, jmax=jmax:
                          attend_tile(t, vstar, jmax, kt, m_run), m0)
        finalize(t)


def _dsa(aq, iq, iw, ik, ak, vt, *, n_batch, nq, tk, q_off, l_valid, topk, keys_3d):
    tq = LANES
    nkt_max = vt.shape[1]
    l_pad = nkt_max * tk
    assert l_valid <= l_pad and vt.shape[2:] == (V_ROWS, tk) and tk % GROUP_KEYS == 0
    n_sub = 4 if nq % 4 == 0 else (2 if nq % 2 == 0 else 1)
    steps = nq // n_sub
    qspec = lambda width: pl.BlockSpec((n_sub * tq, width), lambda b, j: (b * steps + j, 0))
    if keys_3d:
        kspec = pl.BlockSpec((None, l_pad, LANES), lambda b, j: (b, 0, 0))
    else:
        kspec = pl.BlockSpec((l_pad, LANES), lambda b, j: (b, 0))
    kern = functools.partial(_dsa_kernel, tk=tk, q_off=q_off, l_valid=l_valid, topk=topk,
                             idx_bits=int(l_pad).bit_length(), n_sub=n_sub)
    n_slots = IDX_HEADS
    return pl.pallas_call(
        kern,
        grid=(n_batch, steps),
        in_specs=[qspec(512), qspec(512), qspec(LANES), kspec, kspec,
                  pl.BlockSpec((None, nkt_max, V_ROWS, tk), lambda b, j: (b, 0, 0, 0))],
        out_specs=qspec(512),
        out_shape=jax.ShapeDtypeStruct((n_batch * nq * tq, 512), BF16),
        scratch_shapes=[
            pltpu.VMEM((n_sub, LANES, n_slots * tq), BF16),
            pltpu.VMEM((n_sub, LANES, n_slots * tq), BF16),
            pltpu.VMEM((n_sub, LANES, tq), F32),
            pltpu.VMEM((2, nkt_max, tk, tq), jnp.int32),
            pltpu.VMEM((32, 2, l_pad // GROUP_KEYS, SUBLANES, tq), jnp.int32),
            pltpu.VMEM((n_slots // 2, tk, 2 * tq), F32),
            pltpu.VMEM((n_slots // 2, V_ROWS, 2 * tq), F32),
        ],
        compiler_params=_params(2),
        name="dsa_attention",
    )(aq, iq, iw, ik, ak, vt)


def _fill_band(buf, prev_ref, cur_ref, n_front, rows):
    buf[0:n_front, :] = prev_ref[...].astype(BF16)
    buf[n_front:n_front + rows, :] = cur_ref[...]
    buf[n_front + rows:, :] = jnp.zeros((buf.shape[0] - n_front - rows, buf.shape[1]), BF16)


def _band_call(kern, name, q, kprev, kcur, vprev, vcur, consts, *, n_batch, t, n_chunks, n_front, prev_3d):
    rows = n_chunks * CHUNK
    nq = t // rows
    wq = q.shape[1]
    wkv = kcur.shape[1]
    assert t % rows == 0
    if prev_3d:
        pspec = pl.BlockSpec((None, n_front, wkv), lambda b, j: (b, 0, 0))
    else:
        assert rows % n_front == 0 and t % n_front == 0
        per_seq, per_step = t // n_front, rows // n_front
        pspec = pl.BlockSpec((n_front, wkv), lambda b, j: (jnp.maximum(b * per_seq + j * per_step - 1, 0), 0))
    cspec = pl.BlockSpec((rows, wkv), lambda b, j: (b * nq + j, 0))
    qspec = pl.BlockSpec((rows, wq), lambda b, j: (b * nq + j, 0))
    return pl.pallas_call(
        kern,
        grid=(n_batch, nq),
        in_specs=[qspec, pspec, cspec, pspec, cspec] + [_const_spec(c.shape) for c in consts],
        out_specs=qspec,
        out_shape=jax.ShapeDtypeStruct((n_batch * t, wq), BF16),
        scratch_shapes=[pltpu.VMEM((n_front + rows + CHUNK, wkv), BF16)] * 2,
        compiler_params=_params(2),
        name=name,
    )(q, kprev, kcur, vprev, vcur, *consts)


def _band_b_kernel(q_ref, kp_ref, kc_ref, vp_ref, vc_ref, bias_ref, o_ref, kbuf, vbuf,
                   *, n_chunks, n_front, front_valid):
    j = pl.program_id(1)
    rows = n_chunks * CHUNK
    bwp = n_front + 2 * CHUNK
    _fill_band(kbuf, kp_ref, kc_ref, n_front, rows)
    _fill_band(vbuf, vp_ref, vc_ref, n_front, rows)
    half0 = lax.broadcasted_iota(jnp.int32, (CHUNK, LANES), 1) < HEAD_DIM
    ucol = lax.broadcasted_iota(jnp.int32, (1, bwp), 1)
    ones = jnp.ones((bwp, LANES), BF16)
    for ci in range(n_chunks):
        r0 = ci * CHUNK
        if not front_valid:
            front_mask = jnp.where(ucol >= jnp.where(j > 0, 0, n_front - r0), 0.0, NEG)
        for cp in range(B_HEADS // 2):
            cols = slice(cp * LANES, (cp + 1) * LANES)
            qq = q_ref[r0:r0 + CHUNK, cols]
            lhs = jnp.concatenate([jnp.where(half0, qq, jnp.zeros_like(qq)),
                                   jnp.where(half0, jnp.zeros_like(qq), qq)], axis=0)
            s = _dot_nt(lhs, kbuf[r0:r0 + bwp, cols]) + bias_ref[cp]
            if not front_valid:
                s = s + front_mask
            m = jnp.max(s, axis=-1, keepdims=True)
            e_ = jnp.exp(s - m).astype(BF16)
            o = _dot(e_, jnp.concatenate([vbuf[r0:r0 + bwp, cols], ones], axis=1))
            o = o[:, :LANES] / o[:, LANES:]
            o_ref[r0:r0 + CHUNK, cols] = jnp.where(half0, o[:CHUNK], o[CHUNK:]).astype(BF16)


def _band_b(q, kprev, kcur, vprev, vcur, bias, *, n_batch, t, n_chunks, front_valid, prev_3d):
    n_front = B_LEFT_CHUNKS * CHUNK
    kern = functools.partial(_band_b_kernel, n_chunks=n_chunks, n_front=n_front, front_valid=front_valid)
    return _band_call(kern, "band_b_attention", q, kprev, kcur, vprev, vcur, [bias],
                      n_batch=n_batch, t=t, n_chunks=n_chunks, n_front=n_front, prev_3d=prev_3d)


def _band_c_kernel(q_ref, kp_ref, kc_ref, vp_ref, vc_ref, base_ref, o_ref, kbuf, vbuf,
                   *, n_chunks, n_front, front_valid):
    j = pl.program_id(1)
    rows = n_chunks * CHUNK
    bw = n_front + CHUNK
    n_slots = C_HEADS
    _fill_band(kbuf, kp_ref, kc_ref, n_front, rows)
    _fill_band(vbuf, vp_ref, vc_ref, n_front, rows)
    lane = lax.broadcasted_iota(jnp.int32, (CHUNK, LANES), 1)
    half0 = lane < HEAD_DIM
    in_band = lax.broadcasted_iota(jnp.int32, (n_slots * CHUNK, MXU_COLS), 1) < bw
    ucol = lax.broadcasted_iota(jnp.int32, (1, MXU_COLS), 1)
    ones_band = jnp.ones((bw, LANES), BF16)
    tail = jnp.concatenate([jnp.zeros((MXU_COLS - bw, LANES), BF16), jnp.ones((MXU_COLS - bw, LANES), BF16)], axis=1)
    for ci in range(n_chunks):
        r0 = ci * CHUNK
        kb = kbuf[r0:r0 + MXU_COLS, :]
        vb = jnp.concatenate([jnp.concatenate([vbuf[r0:r0 + bw, :], ones_band], axis=1), tail], axis=0)
        parts = []
        for cb in range(n_slots // 2):
            qb = q_ref[r0:r0 + CHUNK, cb * LANES:(cb + 1) * LANES]
            parts.append(jnp.where(half0, qb, jnp.zeros_like(qb)))
            parts.append(jnp.where(half0, jnp.zeros_like(qb), qb))
        s = _dot_nt(jnp.concatenate(parts, axis=0), kb)
        if not front_valid:
            s = s + jnp.where(ucol >= jnp.where(j > 0, 0, n_front - r0), 0.0, NEG)
        s = jnp.where(in_band, s, base_ref[...])
        m = jnp.max(s, axis=-1, keepdims=True)
        e_ = jnp.exp(s - m)
        o = _dot(e_.astype(BF16), vb)
        o = o[:, :LANES] / o[:, LANES:]
        for cb in range(n_slots // 2):
            o0 = o[(2 * cb) * CHUNK:(2 * cb + 1) * CHUNK]
            o1 = o[(2 * cb + 1) * CHUNK:(2 * cb + 2) * CHUNK]
            o_ref[r0:r0 + CHUNK, cb * LANES:(cb + 1) * LANES] = jnp.where(half0, o0, o1).astype(BF16)


def _band_c(q, kprev, kcur, vprev, vcur, base, *, n_batch, t, n_chunks, front_valid, prev_3d):
    n_front = C_LEFT_CHUNKS * CHUNK
    kern = functools.partial(_band_c_kernel, n_chunks=n_chunks, n_front=n_front, front_valid=front_valid)
    return _band_call(kern, "band_c_attention", q, kprev, kcur, vprev, vcur, [base],
                      n_batch=n_batch, t=t, n_chunks=n_chunks, n_front=n_front, prev_3d=prev_3d)


def _rope_tables(pos):
    half = HEAD_DIM // 2
    inv_freq = ROPE_THETA ** (-jnp.arange(half, dtype=F32) / half)
    ang = pos.astype(F32)[:, None] * inv_freq[None, :]
    cos = jnp.cos(ang)
    sin = jnp.sin(ang)
    return (jnp.concatenate([cos, cos, cos, cos], axis=1), jnp.concatenate([-sin, sin, -sin, sin], axis=1))


def _perm_heads_cols(w, perm):
    d = w.shape[0]
    return w.reshape(d, len(perm), HEAD_DIM)[:, np.asarray(perm)].reshape(d, len(perm) * HEAD_DIM)


def _perm_heads_rows(w, perm):
    d = w.shape[1]
    return w.reshape(len(perm), HEAD_DIM, d)[np.asarray(perm)].reshape(len(perm) * HEAD_DIM, d)


def _even_weights(w_in, a_qn, a_kn, i_kn, b_qn, b_kn):
    d = w_in.shape[0]
    sizes = (A_HEADS * HEAD_DIM, A_KV_HEADS * HEAD_DIM, A_KV_HEADS * HEAD_DIM, IDX_HEADS * IDX_DIM, IDX_DIM,
             IDX_HEADS, B_HEADS * HEAD_DIM, B_HEADS * HEAD_DIM, B_HEADS * HEAD_DIM)
    aq, ak, av, iq, ik, iw, bq, bk, bv = jnp.split(w_in, np.cumsum(sizes)[:-1].tolist(), axis=1)
    w = jnp.concatenate([_perm_heads_cols(aq, A_PERM), iq, bq, bk, bv, ak, av, ik, ik,
                         iw, jnp.zeros((d, LANES - IDX_HEADS), w_in.dtype)], axis=1).astype(BF16)
    one = lambda n: jnp.ones((n,), F32)
    gains = jnp.concatenate([jnp.tile(a_qn, A_HEADS), one(512), jnp.tile(b_qn, B_HEADS), jnp.tile(b_kn, B_HEADS),
                             one(512), jnp.tile(a_kn, A_KV_HEADS), one(128), jnp.tile(i_kn, 2), one(128)])
    return w, gains[None, :].astype(F32)


def _odd_weights(w_in, c_qn, c_kn):
    q, k, v = jnp.split(w_in, [C_HEADS * HEAD_DIM, (C_HEADS + C_KV_HEADS) * HEAD_DIM], axis=1)
    w = jnp.concatenate([_perm_heads_cols(q, C_PERM), k, v], axis=1).astype(BF16)
    gains = jnp.concatenate([jnp.tile(c_qn, C_HEADS), jnp.tile(c_kn, C_KV_HEADS), jnp.ones((128,), F32)])
    return w, gains[None, :].astype(F32)


def _pad_rows(x, n, front):
    pad = [(0, 0)] * x.ndim
    pad[1] = (n, 0) if front else (0, n)
    return jnp.pad(x, pad)


def _values_transposed(v, tk):
    nbt, l_pad, _ = v.shape
    vt = jnp.swapaxes(v.reshape(nbt, l_pad // tk, tk, LANES), 2, 3)
    return jnp.concatenate([vt, jnp.ones((nbt, l_pad // tk, BF16_ROWS, tk), v.dtype)], axis=2)


def _band_b_bias(rb):
    n_front = B_LEFT_CHUNKS * CHUNK
    bw = n_front + CHUNK
    n_flat = n_front - B_MAX_REL + CHUNK
    ext = jnp.concatenate([jnp.broadcast_to(rb[:, :1], (rb.shape[0], n_flat)), rb[:, 1:B_MAX_REL + CHUNK]], axis=1)
    period = ext.shape[1]
    rolled = jnp.roll(ext, -(CHUNK - 1), axis=1)
    bias = jnp.tile(rolled, (1, CHUNK))[:, :CHUNK * (period - 1)].reshape(rb.shape[0], CHUNK, period - 1)[:, :, :bw]
    bias = jnp.pad(bias, ((0, 0), (0, 0), (0, CHUNK)), constant_values=NEG)
    return bias.reshape(B_HEADS // 2, 2 * CHUNK, bw + CHUNK)


def kernel(x_prompt, x_sample, cache_a_k, cache_a_v, cache_a_kidx, cache_b_k, cache_b_v, cache_c_k, cache_c_v, norm_mix, norm_ffn, w_in_even, w_out_even, a_q_norm, a_k_norm, idx_k_norm, b_q_norm, b_k_norm, b_rel_bias, w_in_odd, w_out_odd, c_q_norm, c_k_norm, c_sinks, w_ffn_in, w_ffn_out):
    nb, seq, d = x_prompt.shape
    ns, t_new, _ = x_sample.shape
    past = cache_a_k.shape[2]
    depth = norm_mix.shape[0]
    assert seq % LANES == 0 and t_new == CHUNK and past % CHUNK == 0
    assert cache_b_k.shape[2] == B_LEFT_CHUNKS * CHUNK and cache_c_k.shape[2] == C_LEFT_CHUNKS * CHUNK
    topk_p = min(TOPK_MAX, seq // 4)
    topk_s = min(TOPK_MAX, (past + t_new) // 4)
    keep_b = min(B_LEFT_CHUNKS * CHUNK, seq)
    keep_c = min(C_LEFT_CHUNKS * CHUNK, seq)
    n_front_b = B_LEFT_CHUNKS * CHUNK
    n_front_c = C_LEFT_CHUNKS * CHUNK

    hp = x_prompt.reshape(nb * seq, d)
    hs = x_sample.reshape(ns * t_new, d)
    tm_p = _row_tile(nb * seq)
    bps_p = seq // tm_p if (seq % tm_p == 0 and tm_p >= max(keep_b, keep_c)) else 1
    tm_s = _row_tile(ns * t_new)
    pos_p = jnp.tile(jnp.arange(seq, dtype=jnp.int32), max(1, tm_p // seq))
    pos_s = jnp.tile(past + jnp.arange(t_new, dtype=jnp.int32), max(1, tm_s // t_new))
    cos_p, sin_p = _rope_tables(pos_p)
    cos_s, sin_s = _rope_tables(pos_s)
    gid = np.arange(MXU_COLS) // HEAD_DIM
    bd = jnp.asarray((gid[:, None] == gid[None, :]).astype(np.float32) / HEAD_DIM, BF16)

    tk_p = tm_p
    assert seq % tk_p == 0 and tk_p % GROUP_KEYS == 0
    tk_s = 512
    l_s = past + t_new
    l_s_pad = -(-l_s // tk_s) * tk_s
    assert seq % n_front_b == 0
    ch_p = n_front_b // CHUNK

    outs = {k: [] for k in ("pa_k", "pa_v", "pa_i", "pb_k", "pb_v", "pc_k", "pc_v",
                            "sa_k", "sa_v", "sa_i", "sb_k", "sb_v", "sc_k", "sc_v")}

    def last_rows(x, keep, width):
        rows = x.shape[0] // nb
        return x.reshape(nb, rows, width)[:, rows - keep:]

    def pad_queries(x):
        return _pad_rows(x.reshape(ns, t_new, x.shape[1]), LANES - t_new, False).reshape(ns * LANES, x.shape[1])

    wgu = w_ffn_in.astype(BF16)
    wdn = w_ffn_out.astype(BF16)
    for layer in range(depth):
        li = layer // 2
        g_mix = norm_mix[layer][None, :]
        g_ffn = norm_ffn[layer][None, :]
        if layer % 2 == 0:
            w, gains = _even_weights(w_in_even[li], a_q_norm[li], a_k_norm[li], idx_k_norm[li],
                                     b_q_norm[li], b_k_norm[li])
            wo = w_out_even[li]
            wo_a = _perm_heads_rows(wo[:A_HEADS * HEAD_DIM], A_PERM).astype(BF16)
            wo_b = wo[A_HEADS * HEAD_DIM:].astype(BF16)
            bias = _band_b_bias(b_rel_bias[li].astype(F32))

            (aq, iq, bq, bk, bv, bk16, bv16, ak, av, ak16, av16, ik, ik16, iw, vt) = _even_in(
                hp, g_mix, w, gains, bd, cos_p, sin_p, bps_p)
            vt = vt.reshape(nb, seq // tk_p, V_ROWS, tk_p)
            out_a = _dsa(aq, iq, iw, ik16, ak16, vt, n_batch=nb, nq=seq // LANES, tk=tk_p,
                         q_off=0, l_valid=seq, topk=topk_p, keys_3d=False)
            out_b = _band_b(bq, bk16, bk16, bv16, bv16, bias, n_batch=nb, t=seq, n_chunks=ch_p,
                            front_valid=False, prev_3d=False)
            hp = _out_ffn(hp, [out_a, out_b], [wo_a, wo_b], g_ffn, wgu, wdn, layer)
            outs["pa_k"].append(ak.reshape(nb, seq, A_KV_HEADS, HEAD_DIM))
            outs["pa_v"].append(av.reshape(nb, seq, A_KV_HEADS, HEAD_DIM))
            outs["pa_i"].append(ik.reshape(nb, seq, IDX_DIM))
            outs["pb_k"].append(last_rows(bk, keep_b, 512).reshape(nb, keep_b, B_HEADS, HEAD_DIM))
            outs["pb_v"].append(last_rows(bv, keep_b, 512).reshape(nb, keep_b, B_HEADS, HEAD_DIM))

            (aq, iq, bq, bk, bv, bk16, bv16, ak, av, ak16, av16, ik, ik16, iw, _) = _even_in(
                hs, g_mix, w, gains, bd, cos_s, sin_s, 1)
            ci = cache_a_kidx[li].astype(BF16)
            ik_all = jnp.concatenate([jnp.concatenate([ci, ci], axis=-1), ik16.reshape(ns, t_new, LANES)], axis=1)
            ak_all = jnp.concatenate([cache_a_k[li].reshape(ns, past, LANES).astype(BF16),
                                      ak16.reshape(ns, t_new, LANES)], axis=1)
            av_all = jnp.concatenate([cache_a_v[li].reshape(ns, past, LANES).astype(BF16),
                                      av16.reshape(ns, t_new, LANES)], axis=1)
            ik_all, ak_all, av_all = (_pad_rows(x, l_s_pad - l_s, False) for x in (ik_all, ak_all, av_all))
            out_a = _dsa(pad_queries(aq), pad_queries(iq), pad_queries(iw), ik_all, ak_all,
                         _values_transposed(av_all, tk_s), n_batch=ns, nq=1, tk=tk_s,
                         q_off=past, l_valid=l_s, topk=topk_s, keys_3d=True)
            out_a = out_a.reshape(ns, LANES, 512)[:, :t_new].reshape(ns * t_new, 512)
            out_b = _band_b(bq, cache_b_k[li].reshape(ns, n_front_b, 512), bk16,
                            cache_b_v[li].reshape(ns, n_front_b, 512), bv16, bias, n_batch=ns, t=t_new,
                            n_chunks=1, front_valid=True, prev_3d=True)
            hs = _out_ffn(hs, [out_a, out_b], [wo_a, wo_b], g_ffn, wgu, wdn, layer)
            outs["sa_k"].append(ak.reshape(ns, t_new, A_KV_HEADS, HEAD_DIM))
            outs["sa_v"].append(av.reshape(ns, t_new, A_KV_HEADS, HEAD_DIM))
            outs["sa_i"].append(ik.reshape(ns, t_new, IDX_DIM))
            outs["sb_k"].append(bk.reshape(ns, t_new, B_HEADS, HEAD_DIM))
            outs["sb_v"].append(bv.reshape(ns, t_new, B_HEADS, HEAD_DIM))
        else:
            w, gains = _odd_weights(w_in_odd[li], c_q_norm[li], c_k_norm[li])
            wo = _perm_heads_rows(w_out_odd[li], C_PERM).astype(BF16)
            sinks = c_sinks[li].astype(F32)[np.asarray(C_PERM)]
            base = jnp.full((C_HEADS, 1, MXU_COLS), NEG, F32).at[:, 0, n_front_c + CHUNK].set(sinks)
            base = jnp.broadcast_to(base, (C_HEADS, CHUNK, MXU_COLS)).reshape(C_HEADS * CHUNK, MXU_COLS)

            q, k, v, k16, v16 = _odd_in(hp, g_mix, w, gains, bd, cos_p, sin_p, bps_p)
            out_c = _band_c(q, k16, k16, v16, v16, base, n_batch=nb, t=seq, n_chunks=ch_p,
                            front_valid=False, prev_3d=False)
            hp = _out_ffn(hp, [out_c], [wo], g_ffn, wgu, wdn, layer)
            outs["pc_k"].append(last_rows(k, keep_c, LANES).reshape(nb, keep_c, C_KV_HEADS, HEAD_DIM))
            outs["pc_v"].append(last_rows(v, keep_c, LANES).reshape(nb, keep_c, C_KV_HEADS, HEAD_DIM))

            q, k, v, k16, v16 = _odd_in(hs, g_mix, w, gains, bd, cos_s, sin_s, 1)
            out_c = _band_c(q, cache_c_k[li].reshape(ns, n_front_c, LANES), k16,
                            cache_c_v[li].reshape(ns, n_front_c, LANES), v16, base, n_batch=ns, t=t_new,
                            n_chunks=1, front_valid=True, prev_3d=True)
            hs = _out_ffn(hs, [out_c], [wo], g_ffn, wgu, wdn, layer)
            outs["sc_k"].append(k.reshape(ns, t_new, C_KV_HEADS, HEAD_DIM))
            outs["sc_v"].append(v.reshape(ns, t_new, C_KV_HEADS, HEAD_DIM))

    st = lambda name: jnp.stack(outs[name])
    return (hp.reshape(nb, seq, d), hs.reshape(ns, t_new, d),
            st("pa_k"), st("pa_v"), st("pa_i"), st("pb_k"), st("pb_v"), st("pc_k"), st("pc_v"),
            st("sa_k"), st("sa_v"), st("sa_i"), st("sb_k"), st("sb_v"), st("sc_k"), st("sc_v"))
```

```python
import functools
import math

import numpy as np
import jax
import jax.numpy as jnp
from jax import lax
from jax.experimental import pallas as pl
from jax.experimental.pallas import tpu as pltpu

CHUNK = 64
HEAD_DIM = 64
EPS = 1e-6
ROPE_THETA = 10000.0
A_HEADS = 8
A_KV_HEADS = 2
IDX_HEADS = 8
IDX_DIM = 64
IDX_W_SCALE = (IDX_HEADS * IDX_DIM) ** -0.5
TOPK_MAX = 256
B_HEADS = 8
B_LEFT_CHUNKS = 8
B_MAX_REL = 128
C_HEADS = 16
C_KV_HEADS = 2
C_LEFT_CHUNKS = 2
QK_SCALE = HEAD_DIM ** -0.5
QK_SCALE_LOG2 = QK_SCALE * math.log2(math.e)

LANES = 128
SUBLANES = 8
BF16_ROWS = 16
MXU_COLS = 256
VMEM_LIMIT = 56 * 1024 * 1024

NEG = -1e30
INT_MIN = np.int32(-2 ** 31)
F32 = jnp.float32
BF16 = jnp.bfloat16
V_ROWS = 2 * HEAD_DIM + BF16_ROWS

A_PERM = tuple(c + (A_HEADS // 2) * p for c in range(A_HEADS // 2) for p in range(2))
C_PERM = tuple(c + (C_HEADS // 2) * p for c in range(C_HEADS // 2) for p in range(2))


def _dot(a, b):
    return jnp.dot(a, b, preferred_element_type=F32)


def _dot_nt(a, b):
    return lax.dot_general(a, b, (((1,), (1,)), ((), ())), preferred_element_type=F32)


def _row_tile(n):
    for t in (512, 256, 128, 64):
        if n % t == 0:
            return t
    raise ValueError(f"row count {n} is not a multiple of {CHUNK}")


def _const_spec(shape):
    nd = len(shape)
    return pl.BlockSpec(shape, lambda *_: (0,) * nd)


def _params(n_axes):
    return pltpu.CompilerParams(dimension_semantics=("arbitrary",) * n_axes,
                                vmem_limit_bytes=VMEM_LIMIT)


def _group_ms(hb, bd):
    sq = (hb * hb).astype(BF16)
    w = hb.shape[1]
    parts = [_dot(sq[:, i:i + MXU_COLS], bd) for i in range(0, w, MXU_COLS)]
    return parts[0] if len(parts) == 1 else jnp.concatenate(parts, axis=1)


def _rope_blocks(y, cos, sin, first_half):
    out = []
    for i in range(0, y.shape[1], LANES):
        yb = y[:, i:i + LANES]
        sw = jnp.where(first_half, pltpu.roll(yb, LANES - 32, 1), pltpu.roll(yb, 32, 1))
        out.append(yb * cos + sw * sin)
    return out[0] if len(out) == 1 else jnp.concatenate(out, axis=1)


def _normed_input(x_ref, g_ref):
    x = x_ref[...]
    ms = jnp.mean(x * x, axis=-1, keepdims=True)
    return (x * lax.rsqrt(ms + EPS) * g_ref[...]).astype(BF16)


def _even_in_kernel(x_ref, g_ref, w_ref, gain_ref, bd_ref, cos_ref, sin_ref,
                    aq_ref, iq_ref, bq_ref, bk_ref, bv_ref, bk16_ref, bv16_ref,
                    ak_ref, av_ref, ak16_ref, av16_ref, ik_ref, ik16_ref, iw_ref, vt_ref):
    tm = x_ref.shape[0]
    xn = _normed_input(x_ref, g_ref)
    bd = bd_ref[...]
    cos = cos_ref[...]
    sin = sin_ref[...]
    first_half = (lax.broadcasted_iota(jnp.int32, (tm, LANES), 1) % HEAD_DIM) < HEAD_DIM // 2

    def proj(c0, width):
        return _dot(xn, w_ref[:, c0:c0 + width])

    def normed(h, c0):
        return h * lax.rsqrt(_group_ms(h, bd) + EPS) * gain_ref[:, c0:c0 + h.shape[1]]

    h = proj(0, 512)
    aq_ref[...] = (_rope_blocks(normed(h, 0), cos, sin, first_half) * QK_SCALE_LOG2).astype(BF16)
    h = proj(512, 512)
    iq_ref[...] = _rope_blocks(h, cos, sin, first_half).astype(BF16)
    h = proj(1024, 512)
    bq_ref[...] = (normed(h, 1024) * QK_SCALE).astype(BF16)
    h = normed(proj(1536, 512), 1536)
    bk_ref[...] = h
    bk16_ref[...] = h.astype(BF16)
    h = proj(2048, 512)
    bv_ref[...] = h
    bv16_ref[...] = h.astype(BF16)
    h = proj(2560, 256)
    k = _rope_blocks(normed(h, 2560)[:, :LANES], cos, sin, first_half)
    ak_ref[...] = k
    ak16_ref[...] = k.astype(BF16)
    v = h[:, LANES:]
    av_ref[...] = v
    av16_ref[...] = v.astype(BF16)
    vt_ref[0:2 * HEAD_DIM, :] = v.T.astype(BF16)
    vt_ref[2 * HEAD_DIM:, :] = jnp.ones((BF16_ROWS, tm), BF16)
    h = proj(2816, 256)
    k = _rope_blocks(normed(h, 2816)[:, :LANES], cos, sin, first_half)
    ik_ref[...] = k[:, :IDX_DIM]
    ik16_ref[...] = k.astype(BF16)
    iw_ref[...] = h[:, LANES:] * IDX_W_SCALE


def _in_proj_call(kern, name, widths_dtypes, x, g, w, gains, bd, cos, sin, tail_only, blocks_per_seq,
                  values_transposed=False):
    n, d = x.shape
    tm = _row_tile(n)
    assert cos.shape[0] % tm == 0 and (n // tm) % blocks_per_seq == 0
    n_tab = cos.shape[0] // tm
    row = lambda width: pl.BlockSpec((tm, width), lambda i: (i, 0))
    tail = lambda width: pl.BlockSpec((tm, width), lambda i: (i // blocks_per_seq, 0))
    tab = pl.BlockSpec((tm, LANES), lambda i: (i % n_tab, 0))
    out_rows = [n // blocks_per_seq if o in tail_only else n for o in range(len(widths_dtypes))]
    out_specs = [tail(wd) if o in tail_only else row(wd) for o, (wd, _) in enumerate(widths_dtypes)]
    out_shape = [jax.ShapeDtypeStruct((r, wd), dt) for r, (wd, dt) in zip(out_rows, widths_dtypes)]
    if values_transposed:
        out_specs.append(pl.BlockSpec((None, V_ROWS, tm), lambda i: (i, 0, 0)))
        out_shape.append(jax.ShapeDtypeStruct((n // tm, V_ROWS, tm), BF16))
    return pl.pallas_call(
        kern,
        grid=(n // tm,),
        in_specs=[row(d), _const_spec((1, d)), _const_spec(w.shape), _const_spec(gains.shape),
                  _const_spec(bd.shape), tab, tab],
        out_specs=out_specs,
        out_shape=out_shape,
        compiler_params=_params(1),
        name=name,
    )(x, g, w, gains, bd, cos, sin)


def _even_in(x, g, w, gains, bd, cos, sin, blocks_per_seq):
    widths_dtypes = [(512, BF16), (512, BF16), (512, BF16), (512, F32), (512, F32), (512, BF16), (512, BF16),
                     (128, F32), (128, F32), (128, BF16), (128, BF16), (IDX_DIM, F32), (128, BF16), (128, F32)]
    return _in_proj_call(_even_in_kernel, "even_in_proj", widths_dtypes, x, g, w, gains, bd, cos, sin,
                         tail_only=(3, 4), blocks_per_seq=blocks_per_seq, values_transposed=True)


def _odd_in_kernel(x_ref, g_ref, w_ref, gain_ref, bd_ref, cos_ref, sin_ref,
                   q_ref, k_ref, v_ref, k16_ref, v16_ref):
    tm = x_ref.shape[0]
    xn = _normed_input(x_ref, g_ref)
    bd = bd_ref[...]
    cos = cos_ref[...]
    sin = sin_ref[...]
    first_half = (lax.broadcasted_iota(jnp.int32, (tm, LANES), 1) % HEAD_DIM) < HEAD_DIM // 2
    for c0 in (0, 512):
        h = _dot(xn, w_ref[:, c0:c0 + 512])
        h = h * lax.rsqrt(_group_ms(h, bd) + EPS) * gain_ref[:, c0:c0 + 512]
        q_ref[:, c0:c0 + 512] = (_rope_blocks(h, cos, sin, first_half) * QK_SCALE).astype(BF16)
    h = _dot(xn, w_ref[:, 1024:1280])
    hn = h * lax.rsqrt(_group_ms(h, bd) + EPS) * gain_ref[:, 1024:1280]
    k = _rope_blocks(hn[:, :LANES], cos, sin, first_half)
    k_ref[...] = k
    k16_ref[...] = k.astype(BF16)
    v = h[:, LANES:]
    v_ref[...] = v
    v16_ref[...] = v.astype(BF16)


def _odd_in(x, g, w, gains, bd, cos, sin, blocks_per_seq):
    widths_dtypes = [(1024, BF16), (128, F32), (128, F32), (128, BF16), (128, BF16)]
    return _in_proj_call(_odd_in_kernel, "odd_in_proj", widths_dtypes, x, g, w, gains, bd, cos, sin,
                         tail_only=(1, 2), blocks_per_seq=blocks_per_seq)


def _out_ffn_kernel(*refs, n_attn, d_ff, ff_chunk):
    h_ref = refs[0]
    attn_refs = refs[1:1 + n_attn]
    wo_refs = refs[1 + n_attn:1 + 2 * n_attn]
    g_ref, wgu_ref, wdn_ref, o_ref, yn_ref = refs[1 + 2 * n_attn:]
    o_ref[...] = h_ref[...]
    for a_ref, wo_ref in zip(attn_refs, wo_refs):
        o_ref[...] += _dot(a_ref[...], wo_ref[...])
    y = o_ref[...]
    ms = jnp.mean(y * y, axis=-1, keepdims=True)
    yn_ref[...] = (y * lax.rsqrt(ms + EPS) * g_ref[...]).astype(BF16)
    for c0 in range(0, d_ff, ff_chunk):
        gate = _dot(yn_ref[...], wgu_ref[:, c0:c0 + ff_chunk])
        up = _dot(yn_ref[...], wgu_ref[:, d_ff + c0:d_ff + c0 + ff_chunk])
        act = (gate * (1.0 / (1.0 + jnp.exp(-gate))) * up).astype(BF16)
        o_ref[...] += _dot(act, wdn_ref[c0:c0 + ff_chunk, :])


def _out_ffn(h, attns, wos, g, wgu, wdn, layer):
    n, d = h.shape
    tm = _row_tile(n)
    d_ff = wdn.shape[1]
    row = lambda width: pl.BlockSpec((tm, width), lambda i: (i, 0))
    layer_spec = lambda w: pl.BlockSpec((None,) + w.shape[1:], lambda i: (layer, 0, 0))
    kern = functools.partial(_out_ffn_kernel, n_attn=len(attns), d_ff=d_ff, ff_chunk=MXU_COLS)
    return pl.pallas_call(
        kern,
        grid=(n // tm,),
        in_specs=[row(d)] + [row(a.shape[1]) for a in attns] + [_const_spec(w.shape) for w in wos]
                 + [_const_spec((1, d)), layer_spec(wgu), layer_spec(wdn)],
        out_specs=row(d),
        out_shape=jax.ShapeDtypeStruct((n, d), F32),
        scratch_shapes=[pltpu.VMEM((tm, d), BF16)],
        compiler_params=_params(1),
        name="out_proj_ffn",
    )(h, *attns, *wos, g, wgu, wdn)


GROUP_KEYS = 32 * SUBLANES


def _bit_transpose_32(rows):
    a = list(rows)
    j = 16
    m = 0x0000FFFF
    while j:
        k = 0
        while k < 32:
            t = (a[k] ^ lax.shift_right_logical(a[k + j], jnp.int32(j))) & jnp.int32(m)
            a[k] = a[k] ^ t
            a[k + j] = a[k + j] ^ lax.shift_left(t, jnp.int32(j))
            k = (k + j + 1) & ~j
        j >>= 1
        if j:
            m = (m ^ (m << j)) & 0xFFFFFFFF
    return a


def _dsa_kernel(aq_ref, iq_ref, iw_ref, ik_ref, ak_ref, vt_ref, o_ref,
                iqs_ref, aqs_ref, w_ref, keys_ref, planes_ref, s_ref, acc_ref,
                *, tk, q_off, l_valid, topk, idx_bits, n_sub):
    tq = LANES
    n_slots = IDX_HEADS
    n_pairs = n_slots // 2
    n_chains = 4
    nkt_max = keys_ref.shape[1]
    groups_per_tile = tk // GROUP_KEYS
    n_groups = nkt_max * groups_per_tile
    jj = pl.program_id(1)
    krow = lax.broadcasted_iota(jnp.int32, (tk, tq), 0)
    group0 = lax.broadcasted_iota(jnp.int32, (2 * HEAD_DIM, tq), 0) < HEAD_DIM
    idx_all = jnp.int32(2 ** idx_bits - 1)

    def tile_info(t):
        qpos0 = q_off + (jj * n_sub + t) * tq
        n_adm_max = jnp.minimum(((qpos0 + tq - 1) // CHUNK + 1) * CHUNK, l_valid)
        qcol = qpos0 + lax.broadcasted_iota(jnp.int32, (1, tq), 1)
        return (n_adm_max + tk - 1) // tk, jnp.minimum((qcol // CHUNK + 1) * CHUNK, l_valid)

    def build_stacks(t):
        rows = slice(t * tq, (t + 1) * tq)
        for c in range(n_pairs):
            iqt = iq_ref[rows, c * LANES:(c + 1) * LANES].astype(F32).T
            aqt = aq_ref[rows, c * LANES:(c + 1) * LANES].astype(F32).T
            for p in range(2):
                n = 2 * c + p
                keep = group0 if p == 0 else jnp.logical_not(group0)
                iqs_ref[t, :, n * tq:(n + 1) * tq] = jnp.where(keep, iqt, 0.0).astype(BF16)
                aqs_ref[t, :, n * tq:(n + 1) * tq] = jnp.where(keep, aqt, 0.0).astype(BF16)
        w_ref[t] = iw_ref[rows, :].T

    def score_tile(t, klim, kt):
        par = t % 2
        k0 = pl.multiple_of(kt * tk, tk)
        ikt = ik_ref[pl.ds(k0, tk), :]
        score = jnp.zeros((tk, tq), F32)
        for c in range(n_pairs):
            s = _dot(ikt, iqs_ref[t, :, 2 * c * tq:2 * (c + 1) * tq])
            for p in range(2):
                score = score + jnp.maximum(s[:, p * tq:(p + 1) * tq], 0.0) * w_ref[t, 2 * c + p:2 * c + p + 1, :]
        bits = lax.bitcast_convert_type(score, jnp.int32)
        key = bits ^ ((bits >> 31) & jnp.int32(0x7FFFFFFF))
        key = jnp.where(key == -1, 0, key)
        key = jnp.where(krow + k0 < klim, key, INT_MIN)
        keys_ref[par, kt] = key
        ukey = key ^ INT_MIN
        for g in range(groups_per_tile):
            base = g * GROUP_KEYS
            planes = _bit_transpose_32([ukey[base + SUBLANES * i:base + SUBLANES * (i + 1), :] for i in range(32)])
            for b in range(32):
                planes_ref[b, par, kt * groups_per_tile + g] = planes[b]

    def clear_unused_planes(t, nkt):
        zero = jnp.zeros((SUBLANES, tq), jnp.int32)

        def body(kt, carry):
            for g in range(groups_per_tile):
                for b in range(32):
                    planes_ref[b, t % 2, kt * groups_per_tile + g] = zero
            return carry
        lax.fori_loop(nkt, nkt_max, body, 0)

    def popcount_rows(words):
        pcs = [lax.population_count(w) for w in words]
        chains = [sum(pcs[c::n_chains][1:], pcs[c]) for c in range(min(n_chains, len(pcs)))]
        return jnp.sum(sum(chains[1:], chains[0]).astype(F32), axis=0, keepdims=True)

    def search(t, nkt):
        par = t % 2

        def count(indicator):
            def body(kt, acc):
                ind = indicator(keys_ref[par, kt], kt)
                return acc + jnp.sum(ind.reshape(tk // (n_chains * SUBLANES), n_chains, SUBLANES, tq), axis=0)
            acc = lax.fori_loop(0, nkt, body, jnp.zeros((n_chains, SUBLANES, tq), F32))
            return jnp.sum(jnp.sum(acc, axis=0), axis=0, keepdims=True)

        def search_bit(i, carry):
            ans, n_gt, und = carry
            plane = planes_ref[i, par]
            hit = [und[g] & plane[g] for g in range(n_groups)]
            total = n_gt + popcount_rows(hit)
            take = total >= topk
            takem = jnp.where(take, jnp.int32(-1), jnp.int32(0))
            ans = jnp.where(take, ans | jnp.left_shift(jnp.int32(1), 31 - i), ans)
            n_gt = jnp.where(take, n_gt, total)
            und = tuple((und[g] ^ hit[g]) ^ (und[g] & takem) for g in range(n_groups))
            return ans, n_gt, und

        live_groups = nkt * groups_per_tile
        und0 = tuple(jnp.full((SUBLANES, tq), jnp.where(g < live_groups, jnp.int32(-1), jnp.int32(0)), jnp.int32)
                     for g in range(n_groups))
        ans, c_gt, und = lax.fori_loop(
            0, 32, search_bit, (jnp.zeros((1, tq), jnp.int32), jnp.zeros((1, tq), F32), und0))
        vstar = ans ^ INT_MIN
        c_ge = c_gt + popcount_rows(und)
        need = topk - c_gt
        has_tie = jnp.where(c_ge > topk, jnp.where(vstar > INT_MIN, 1.0, 0.0), 0.0)

        def tie_search():
            def tie_bit(i, jmax):
                cand = jmax | jnp.left_shift(jnp.int32(1), idx_bits - 1 - i)
                cnt = count(lambda k, kt: jnp.where(k == vstar, jnp.where(krow + kt * tk < cand, 1.0, 0.0), 0.0))
                return jnp.where(cnt <= need, cand, jmax)
            return lax.fori_loop(0, idx_bits, tie_bit, jnp.zeros((1, tq), jnp.int32))

        jmax = lax.cond(jnp.max(has_tie) > 0.0, tie_search, lambda: jnp.full((1, tq), idx_all, jnp.int32))
        return vstar, jnp.where(vstar == INT_MIN, 0, jmax)

    def attend_tile(t, vstar, jmax, kt, m_run):
        k0 = pl.multiple_of(kt * tk, tk)
        key = keys_ref[t % 2, kt]
        tie_ok = jnp.where(krow + k0 < jmax, 0.0, NEG)
        mb = jnp.where(key > vstar, 0.0, jnp.where(key == vstar, tie_ok, NEG))
        mb2 = jnp.concatenate([mb, mb], axis=1)
        akt = ak_ref[pl.ds(k0, tk), :]
        vt = vt_ref[kt]
        m_next = []
        for c in range(n_pairs):
            s = _dot(akt, aqs_ref[t, :, 2 * c * tq:2 * (c + 1) * tq]) + mb2
            s_ref[c] = s
            m_next.append(jnp.maximum(m_run[c], jnp.max(s, axis=0, keepdims=True)))
        for c in range(n_pairs):
            alpha = jnp.exp2(m_run[c] - m_next[c])
            pr = jnp.exp2(s_ref[c] - m_next[c])
            acc_ref[c] = alpha * acc_ref[c] + _dot(vt, pr.astype(BF16))
        return tuple(m_next)

    def finalize(t):
        for c in range(n_pairs):
            a = acc_ref[c]
            o = a[:2 * HEAD_DIM] / a[2 * HEAD_DIM:2 * HEAD_DIM + 1]
            o_ref[t * tq:(t + 1) * tq, c * LANES:(c + 1) * LANES] = (
                jnp.where(group0, o[:, :tq], o[:, tq:]).T.astype(BF16))

    info = [tile_info(t) for t in range(n_sub)]
    for t in range(n_sub):
        build_stacks(t)

    def score_only(t, lo, hi):
        def body(kt, carry):
            score_tile(t, info[t][1], kt)
            return carry
        lax.fori_loop(lo, hi, body, 0)

    score_only(0, 0, info[0][0])
    for t in range(n_sub):
        nkt = info[t][0]
        clear_unused_planes(t, nkt)
        vstar, jmax = search(t, nkt)
        acc_ref[...] = jnp.zeros(acc_ref.shape, F32)
        m0 = tuple(jnp.full((1, 2 * tq), NEG, F32) for _ in range(n_pairs))
        if t + 1 < n_sub:
            def fused(kt, m_run, t=t, vstar=vstar, jmax=jmax):
                score_tile(t + 1, info[t + 1][1], kt)
                return attend_tile(t, vstar, jmax, kt, m_run)
            lax.fori_loop(0, nkt, fused, m0)
            score_only(t + 1, nkt, info[t + 1][0])
        else:
            lax.fori_loop(0, nkt, lambda kt, m_run, t=t, vstar=vstar, jmax=jmax:
                          attend_tile(t, vstar, jmax, kt, m_run), m0)
        finalize(t)


def _dsa(aq, iq, iw, ik, ak, vt, *, n_batch, nq, tk, q_off, l_valid, topk, keys_3d):
    tq = LANES
    nkt_max = vt.shape[1]
    l_pad = nkt_max * tk
    assert l_valid <= l_pad and vt.shape[2:] == (V_ROWS, tk) and tk % GROUP_KEYS == 0
    n_sub = 4 if nq % 4 == 0 else (2 if nq % 2 == 0 else 1)
    steps = nq // n_sub
    qspec = lambda width: pl.BlockSpec((n_sub * tq, width), lambda b, j: (b * steps + j, 0))
    if keys_3d:
        kspec = pl.BlockSpec((None, l_pad, LANES), lambda b, j: (b, 0, 0))
    else:
        kspec = pl.BlockSpec((l_pad, LANES), lambda b, j: (b, 0))
    kern = functools.partial(_dsa_kernel, tk=tk, q_off=q_off, l_valid=l_valid, topk=topk,
                             idx_bits=int(l_pad).bit_length(), n_sub=n_sub)
    n_slots = IDX_HEADS
    return pl.pallas_call(
        kern,
        grid=(n_batch, steps),
        in_specs=[qspec(512), qspec(512), qspec(LANES), kspec, kspec,
                  pl.BlockSpec((None, nkt_max, V_ROWS, tk), lambda b, j: (b, 0, 0, 0))],
        out_specs=qspec(512),
        out_shape=jax.ShapeDtypeStruct((n_batch * nq * tq, 512), BF16),
        scratch_shapes=[
            pltpu.VMEM((n_sub, LANES, n_slots * tq), BF16),
            pltpu.VMEM((n_sub, LANES, n_slots * tq), BF16),
            pltpu.VMEM((n_sub, LANES, tq), F32),
            pltpu.VMEM((2, nkt_max, tk, tq), jnp.int32),
            pltpu.VMEM((32, 2, l_pad // GROUP_KEYS, SUBLANES, tq), jnp.int32),
            pltpu.VMEM((n_slots // 2, tk, 2 * tq), F32),
            pltpu.VMEM((n_slots // 2, V_ROWS, 2 * tq), F32),
        ],
        compiler_params=_params(2),
        name="dsa_attention",
    )(aq, iq, iw, ik, ak, vt)


def _fill_band(buf, prev_ref, cur_ref, n_front, rows):
    buf[0:n_front, :] = prev_ref[...].astype(BF16)
    buf[n_front:n_front + rows, :] = cur_ref[...]
    buf[n_front + rows:, :] = jnp.zeros((buf.shape[0] - n_front - rows, buf.shape[1]), BF16)


def _band_call(kern, name, q, kprev, kcur, vprev, vcur, consts, *, n_batch, t, n_chunks, n_front, prev_3d):
    rows = n_chunks * CHUNK
    nq = t // rows
    wq = q.shape[1]
    wkv = kcur.shape[1]
    assert t % rows == 0
    if prev_3d:
        pspec = pl.BlockSpec((None, n_front, wkv), lambda b, j: (b, 0, 0))
    else:
        assert rows % n_front == 0 and t % n_front == 0
        per_seq, per_step = t // n_front, rows // n_front
        pspec = pl.BlockSpec((n_front, wkv), lambda b, j: (jnp.maximum(b * per_seq + j * per_step - 1, 0), 0))
    cspec = pl.BlockSpec((rows, wkv), lambda b, j: (b * nq + j, 0))
    qspec = pl.BlockSpec((rows, wq), lambda b, j: (b * nq + j, 0))
    return pl.pallas_call(
        kern,
        grid=(n_batch, nq),
        in_specs=[qspec, pspec, cspec, pspec, cspec] + [_const_spec(c.shape) for c in consts],
        out_specs=qspec,
        out_shape=jax.ShapeDtypeStruct((n_batch * t, wq), BF16),
        scratch_shapes=[pltpu.VMEM((n_front + rows + CHUNK, wkv), BF16)] * 2,
        compiler_params=_params(2),
        name=name,
    )(q, kprev, kcur, vprev, vcur, *consts)


def _band_b_kernel(q_ref, kp_ref, kc_ref, vp_ref, vc_ref, bias_ref, o_ref, kbuf, vbuf,
                   *, n_chunks, n_front, front_valid):
    j = pl.program_id(1)
    rows = n_chunks * CHUNK
    bwp = n_front + 2 * CHUNK
    _fill_band(kbuf, kp_ref, kc_ref, n_front, rows)
    _fill_band(vbuf, vp_ref, vc_ref, n_front, rows)
    half0 = lax.broadcasted_iota(jnp.int32, (CHUNK, LANES), 1) < HEAD_DIM
    ucol = lax.broadcasted_iota(jnp.int32, (1, bwp), 1)
    ones = jnp.ones((bwp, LANES), BF16)
    for ci in range(n_chunks):
        r0 = ci * CHUNK
        if not front_valid:
            front_mask = jnp.where(ucol >= jnp.where(j > 0, 0, n_front - r0), 0.0, NEG)
        for cp in range(B_HEADS // 2):
            cols = slice(cp * LANES, (cp + 1) * LANES)
            qq = q_ref[r0:r0 + CHUNK, cols]
            lhs = jnp.concatenate([jnp.where(half0, qq, jnp.zeros_like(qq)),
                                   jnp.where(half0, jnp.zeros_like(qq), qq)], axis=0)
            s = _dot_nt(lhs, kbuf[r0:r0 + bwp, cols]) + bias_ref[cp]
            if not front_valid:
                s = s + front_mask
            m = jnp.max(s, axis=-1, keepdims=True)
            e_ = jnp.exp(s - m).astype(BF16)
            o = _dot(e_, jnp.concatenate([vbuf[r0:r0 + bwp, cols], ones], axis=1))
            o = o[:, :LANES] / o[:, LANES:]
            o_ref[r0:r0 + CHUNK, cols] = jnp.where(half0, o[:CHUNK], o[CHUNK:]).astype(BF16)


def _band_b(q, kprev, kcur, vprev, vcur, bias, *, n_batch, t, n_chunks, front_valid, prev_3d):
    n_front = B_LEFT_CHUNKS * CHUNK
    kern = functools.partial(_band_b_kernel, n_chunks=n_chunks, n_front=n_front, front_valid=front_valid)
    return _band_call(kern, "band_b_attention", q, kprev, kcur, vprev, vcur, [bias],
                      n_batch=n_batch, t=t, n_chunks=n_chunks, n_front=n_front, prev_3d=prev_3d)


def _band_c_kernel(q_ref, kp_ref, kc_ref, vp_ref, vc_ref, base_ref, o_ref, kbuf, vbuf,
                   *, n_chunks, n_front, front_valid):
    j = pl.program_id(1)
    rows = n_chunks * CHUNK
    bw = n_front + CHUNK
    n_slots = C_HEADS
    _fill_band(kbuf, kp_ref, kc_ref, n_front, rows)
    _fill_band(vbuf, vp_ref, vc_ref, n_front, rows)
    lane = lax.broadcasted_iota(jnp.int32, (CHUNK, LANES), 1)
    half0 = lane < HEAD_DIM
    in_band = lax.broadcasted_iota(jnp.int32, (n_slots * CHUNK, MXU_COLS), 1) < bw
    ucol = lax.broadcasted_iota(jnp.int32, (1, MXU_COLS), 1)
    ones_band = jnp.ones((bw, LANES), BF16)
    tail = jnp.concatenate([jnp.zeros((MXU_COLS - bw, LANES), BF16), jnp.ones((MXU_COLS - bw, LANES), BF16)], axis=1)
    for ci in range(n_chunks):
        r0 = ci * CHUNK
        kb = kbuf[r0:r0 + MXU_COLS, :]
        vb = jnp.concatenate([jnp.concatenate([vbuf[r0:r0 + bw, :], ones_band], axis=1), tail], axis=0)
        parts = []
        for cb in range(n_slots // 2):
            qb = q_ref[r0:r0 + CHUNK, cb * LANES:(cb + 1) * LANES]
            parts.append(jnp.where(half0, qb, jnp.zeros_like(qb)))
            parts.append(jnp.where(half0, jnp.zeros_like(qb), qb))
        s = _dot_nt(jnp.concatenate(parts, axis=0), kb)
        if not front_valid:
            s = s + jnp.where(ucol >= jnp.where(j > 0, 0, n_front - r0), 0.0, NEG)
        s = jnp.where(in_band, s, base_ref[...])
        m = jnp.max(s, axis=-1, keepdims=True)
        e_ = jnp.exp(s - m)
        o = _dot(e_.astype(BF16), vb)
        o = o[:, :LANES] / o[:, LANES:]
        for cb in range(n_slots // 2):
            o0 = o[(2 * cb) * CHUNK:(2 * cb + 1) * CHUNK]
            o1 = o[(2 * cb + 1) * CHUNK:(2 * cb + 2) * CHUNK]
            o_ref[r0:r0 + CHUNK, cb * LANES:(cb + 1) * LANES] = jnp.where(half0, o0, o1).astype(BF16)


def _band_c(q, kprev, kcur, vprev, vcur, base, *, n_batch, t, n_chunks, front_valid, prev_3d):
    n_front = C_LEFT_CHUNKS * CHUNK
    kern = functools.partial(_band_c_kernel, n_chunks=n_chunks, n_front=n_front, front_valid=front_valid)
    return _band_call(kern, "band_c_attention", q, kprev, kcur, vprev, vcur, [base],
                      n_batch=n_batch, t=t, n_chunks=n_chunks, n_front=n_front, prev_3d=prev_3d)


def _rope_tables(pos):
    half = HEAD_DIM // 2
    inv_freq = ROPE_THETA ** (-jnp.arange(half, dtype=F32) / half)
    ang = pos.astype(F32)[:, None] * inv_freq[None, :]
    cos = jnp.cos(ang)
    sin = jnp.sin(ang)
    return (jnp.concatenate([cos, cos, cos, cos], axis=1), jnp.concatenate([-sin, sin, -sin, sin], axis=1))


def _perm_heads_cols(w, perm):
    d = w.shape[0]
    return w.reshape(d, len(perm), HEAD_DIM)[:, np.asarray(perm)].reshape(d, len(perm) * HEAD_DIM)


def _perm_heads_rows(w, perm):
    d = w.shape[1]
    return w.reshape(len(perm), HEAD_DIM, d)[np.asarray(perm)].reshape(len(perm) * HEAD_DIM, d)


def _even_weights(w_in, a_qn, a_kn, i_kn, b_qn, b_kn):
    d = w_in.shape[0]
    sizes = (A_HEADS * HEAD_DIM, A_KV_HEADS * HEAD_DIM, A_KV_HEADS * HEAD_DIM, IDX_HEADS * IDX_DIM, IDX_DIM,
             IDX_HEADS, B_HEADS * HEAD_DIM, B_HEADS * HEAD_DIM, B_HEADS * HEAD_DIM)
    aq, ak, av, iq, ik, iw, bq, bk, bv = jnp.split(w_in, np.cumsum(sizes)[:-1].tolist(), axis=1)
    w = jnp.concatenate([_perm_heads_cols(aq, A_PERM), iq, bq, bk, bv, ak, av, ik, ik,
                         iw, jnp.zeros((d, LANES - IDX_HEADS), w_in.dtype)], axis=1).astype(BF16)
    one = lambda n: jnp.ones((n,), F32)
    gains = jnp.concatenate([jnp.tile(a_qn, A_HEADS), one(512), jnp.tile(b_qn, B_HEADS), jnp.tile(b_kn, B_HEADS),
                             one(512), jnp.tile(a_kn, A_KV_HEADS), one(128), jnp.tile(i_kn, 2), one(128)])
    return w, gains[None, :].astype(F32)


def _odd_weights(w_in, c_qn, c_kn):
    q, k, v = jnp.split(w_in, [C_HEADS * HEAD_DIM, (C_HEADS + C_KV_HEADS) * HEAD_DIM], axis=1)
    w = jnp.concatenate([_perm_heads_cols(q, C_PERM), k, v], axis=1).astype(BF16)
    gains = jnp.concatenate([jnp.tile(c_qn, C_HEADS), jnp.tile(c_kn, C_KV_HEADS), jnp.ones((128,), F32)])
    return w, gains[None, :].astype(F32)


def _pad_rows(x, n, front):
    pad = [(0, 0)] * x.ndim
    pad[1] = (n, 0) if front else (0, n)
    return jnp.pad(x, pad)


def _values_transposed(v, tk):
    nbt, l_pad, _ = v.shape
    vt = jnp.swapaxes(v.reshape(nbt, l_pad // tk, tk, LANES), 2, 3)
    return jnp.concatenate([vt, jnp.ones((nbt, l_pad // tk, BF16_ROWS, tk), v.dtype)], axis=2)


def _band_b_bias(rb):
    n_front = B_LEFT_CHUNKS * CHUNK
    bw = n_front + CHUNK
    n_flat = n_front - B_MAX_REL + CHUNK
    ext = jnp.concatenate([jnp.broadcast_to(rb[:, :1], (rb.shape[0], n_flat)), rb[:, 1:B_MAX_REL + CHUNK]], axis=1)
    period = ext.shape[1]
    rolled = jnp.roll(ext, -(CHUNK - 1), axis=1)
    bias = jnp.tile(rolled, (1, CHUNK))[:, :CHUNK * (period - 1)].reshape(rb.shape[0], CHUNK, period - 1)[:, :, :bw]
    bias = jnp.pad(bias, ((0, 0), (0, 0), (0, CHUNK)), constant_values=NEG)
    return bias.reshape(B_HEADS // 2, 2 * CHUNK, bw + CHUNK)


def kernel(x_prompt, x_sample, cache_a_k, cache_a_v, cache_a_kidx, cache_b_k, cache_b_v, cache_c_k, cache_c_v, norm_mix, norm_ffn, w_in_even, w_out_even, a_q_norm, a_k_norm, idx_k_norm, b_q_norm, b_k_norm, b_rel_bias, w_in_odd, w_out_odd, c_q_norm, c_k_norm, c_sinks, w_ffn_in, w_ffn_out):
    nb, seq, d = x_prompt.shape
    ns, t_new, _ = x_sample.shape
    past = cache_a_k.shape[2]
    depth = norm_mix.shape[0]
    assert seq % LANES == 0 and t_new == CHUNK and past % CHUNK == 0
    assert cache_b_k.shape[2] == B_LEFT_CHUNKS * CHUNK and cache_c_k.shape[2] == C_LEFT_CHUNKS * CHUNK
    topk_p = min(TOPK_MAX, seq // 4)
    topk_s = min(TOPK_MAX, (past + t_new) // 4)
    keep_b = min(B_LEFT_CHUNKS * CHUNK, seq)
    keep_c = min(C_LEFT_CHUNKS * CHUNK, seq)
    n_front_b = B_LEFT_CHUNKS * CHUNK
    n_front_c = C_LEFT_CHUNKS * CHUNK

    hp = x_prompt.reshape(nb * seq, d)
    hs = x_sample.reshape(ns * t_new, d)
    tm_p = _row_tile(nb * seq)
    bps_p = seq // tm_p if (seq % tm_p == 0 and tm_p >= max(keep_b, keep_c)) else 1
    tm_s = _row_tile(ns * t_new)
    pos_p = jnp.tile(jnp.arange(seq, dtype=jnp.int32), max(1, tm_p // seq))
    pos_s = jnp.tile(past + jnp.arange(t_new, dtype=jnp.int32), max(1, tm_s // t_new))
    cos_p, sin_p = _rope_tables(pos_p)
    cos_s, sin_s = _rope_tables(pos_s)
    gid = np.arange(MXU_COLS) // HEAD_DIM
    bd = jnp.asarray((gid[:, None] == gid[None, :]).astype(np.float32) / HEAD_DIM, BF16)

    tk_p = tm_p
    assert seq % tk_p == 0 and tk_p % GROUP_KEYS == 0
    tk_s = 512
    l_s = past + t_new
    l_s_pad = -(-l_s // tk_s) * tk_s
    assert seq % n_front_b == 0
    ch_p = n_front_b // CHUNK

    outs = {k: [] for k in ("pa_k", "pa_v", "pa_i", "pb_k", "pb_v", "pc_k", "pc_v",
                            "sa_k", "sa_v", "sa_i", "sb_k", "sb_v", "sc_k", "sc_v")}

    def last_rows(x, keep, width):
        rows = x.shape[0] // nb
        return x.reshape(nb, rows, width)[:, rows - keep:]

    def pad_queries(x):
        x = x.reshape(ns, t_new, x.shape[1])
        return jnp.concatenate([x] * (LANES // t_new), axis=1).reshape(ns * LANES, x.shape[2])

    wgu = w_ffn_in.astype(BF16)
    wdn = w_ffn_out.astype(BF16)
    for layer in range(depth):
        li = layer // 2
        g_mix = norm_mix[layer][None, :]
        g_ffn = norm_ffn[layer][None, :]
        if layer % 2 == 0:
            w, gains = _even_weights(w_in_even[li], a_q_norm[li], a_k_norm[li], idx_k_norm[li],
                                     b_q_norm[li], b_k_norm[li])
            wo = w_out_even[li]
            wo_a = _perm_heads_rows(wo[:A_HEADS * HEAD_DIM], A_PERM).astype(BF16)
            wo_b = wo[A_HEADS * HEAD_DIM:].astype(BF16)
            bias = _band_b_bias(b_rel_bias[li].astype(F32))

            (aq, iq, bq, bk, bv, bk16, bv16, ak, av, ak16, av16, ik, ik16, iw, vt) = _even_in(
                hp, g_mix, w, gains, bd, cos_p, sin_p, bps_p)
            vt = vt.reshape(nb, seq // tk_p, V_ROWS, tk_p)
            out_a = _dsa(aq, iq, iw, ik16, ak16, vt, n_batch=nb, nq=seq // LANES, tk=tk_p,
                         q_off=0, l_valid=seq, topk=topk_p, keys_3d=False)
            out_b = _band_b(bq, bk16, bk16, bv16, bv16, bias, n_batch=nb, t=seq, n_chunks=ch_p,
                            front_valid=False, prev_3d=False)
            hp = _out_ffn(hp, [out_a, out_b], [wo_a, wo_b], g_ffn, wgu, wdn, layer)
            outs["pa_k"].append(ak.reshape(nb, seq, A_KV_HEADS, HEAD_DIM))
            outs["pa_v"].append(av.reshape(nb, seq, A_KV_HEADS, HEAD_DIM))
            outs["pa_i"].append(ik.reshape(nb, seq, IDX_DIM))
            outs["pb_k"].append(last_rows(bk, keep_b, 512).reshape(nb, keep_b, B_HEADS, HEAD_DIM))
            outs["pb_v"].append(last_rows(bv, keep_b, 512).reshape(nb, keep_b, B_HEADS, HEAD_DIM))

            (aq, iq, bq, bk, bv, bk16, bv16, ak, av, ak16, av16, ik, ik16, iw, _) = _even_in(
                hs, g_mix, w, gains, bd, cos_s, sin_s, 1)
            ci = cache_a_kidx[li].astype(BF16)
            ik_all = jnp.concatenate([jnp.concatenate([ci, ci], axis=-1), ik16.reshape(ns, t_new, LANES)], axis=1)
            ak_all = jnp.concatenate([cache_a_k[li].reshape(ns, past, LANES).astype(BF16),
                                      ak16.reshape(ns, t_new, LANES)], axis=1)
            av_all = jnp.concatenate([cache_a_v[li].reshape(ns, past, LANES).astype(BF16),
                                      av16.reshape(ns, t_new, LANES)], axis=1)
            ik_all, ak_all, av_all = (_pad_rows(x, l_s_pad - l_s, False) for x in (ik_all, ak_all, av_all))
            out_a = _dsa(pad_queries(aq), pad_queries(iq), pad_queries(iw), ik_all, ak_all,
                         _values_transposed(av_all, tk_s), n_batch=ns, nq=1, tk=tk_s,
                         q_off=past, l_valid=l_s, topk=topk_s, keys_3d=True)
            out_a = out_a.reshape(ns, LANES, 512)[:, :t_new].reshape(ns * t_new, 512)
            out_b = _band_b(bq, cache_b_k[li].reshape(ns, n_front_b, 512), bk16,
                            cache_b_v[li].reshape(ns, n_front_b, 512), bv16, bias, n_batch=ns, t=t_new,
                            n_chunks=1, front_valid=True, prev_3d=True)
            hs = _out_ffn(hs, [out_a, out_b], [wo_a, wo_b], g_ffn, wgu, wdn, layer)
            outs["sa_k"].append(ak.reshape(ns, t_new, A_KV_HEADS, HEAD_DIM))
            outs["sa_v"].append(av.reshape(ns, t_new, A_KV_HEADS, HEAD_DIM))
            outs["sa_i"].append(ik.reshape(ns, t_new, IDX_DIM))
            outs["sb_k"].append(bk.reshape(ns, t_new, B_HEADS, HEAD_DIM))
            outs["sb_v"].append(bv.reshape(ns, t_new, B_HEADS, HEAD_DIM))
        else:
            w, gains = _odd_weights(w_in_odd[li], c_q_norm[li], c_k_norm[li])
            wo = _perm_heads_rows(w_out_odd[li], C_PERM).astype(BF16)
            sinks = c_sinks[li].astype(F32)[np.asarray(C_PERM)]
            base = jnp.full((C_HEADS, 1, MXU_COLS), NEG, F32).at[:, 0, n_front_c + CHUNK].set(sinks)
            base = jnp.broadcast_to(base, (C_HEADS, CHUNK, MXU_COLS)).reshape(C_HEADS * CHUNK, MXU_COLS)

            q, k, v, k16, v16 = _odd_in(hp, g_mix, w, gains, bd, cos_p, sin_p, bps_p)
            out_c = _band_c(q, k16, k16, v16, v16, base, n_batch=nb, t=seq, n_chunks=ch_p,
                            front_valid=False, prev_3d=False)
            hp = _out_ffn(hp, [out_c], [wo], g_ffn, wgu, wdn, layer)
            outs["pc_k"].append(last_rows(k, keep_c, LANES).reshape(nb, keep_c, C_KV_HEADS, HEAD_DIM))
            outs["pc_v"].append(last_rows(v, keep_c, LANES).reshape(nb, keep_c, C_KV_HEADS, HEAD_DIM))

            q, k, v, k16, v16 = _odd_in(hs, g_mix, w, gains, bd, cos_s, sin_s, 1)
            out_c = _band_c(q, cache_c_k[li].reshape(ns, n_front_c, LANES), k16,
                            cache_c_v[li].reshape(ns, n_front_c, LANES), v16, base, n_batch=ns, t=t_new,
                            n_chunks=1, front_valid=True, prev_3d=True)
            hs = _out_ffn(hs, [out_c], [wo], g_ffn, wgu, wdn, layer)
            outs["sc_k"].append(k.reshape(ns, t_new, C_KV_HEADS, HEAD_DIM))
            outs["sc_v"].append(v.reshape(ns, t_new, C_KV_HEADS, HEAD_DIM))

    st = lambda name: jnp.stack(outs[name])
    return (hp.reshape(nb, seq, d), hs.reshape(ns, t_new, d),
            st("pa_k"), st("pa_v"), st("pa_i"), st("pb_k"), st("pb_v"), st("pc_k"), st("pc_v"),
            st("sa_k"), st("sa_v"), st("sa_i"), st("sb_k"), st("sb_v"), st("sc_k"), st("sc_v"))
```

```python
import functools
import math

import numpy as np
import jax
import jax.numpy as jnp
from jax import lax
from jax.experimental import pallas as pl
from jax.experimental.pallas import tpu as pltpu

CHUNK = 64
HEAD_DIM = 64
EPS = 1e-6
ROPE_THETA = 10000.0
A_HEADS = 8
A_KV_HEADS = 2
IDX_HEADS = 8
IDX_DIM = 64
IDX_W_SCALE = (IDX_HEADS * IDX_DIM) ** -0.5
TOPK_MAX = 256
B_HEADS = 8
B_LEFT_CHUNKS = 8
B_MAX_REL = 128
C_HEADS = 16
C_KV_HEADS = 2
C_LEFT_CHUNKS = 2
QK_SCALE = HEAD_DIM ** -0.5
QK_SCALE_LOG2 = QK_SCALE * math.log2(math.e)

LANES = 128
SUBLANES = 8
BF16_ROWS = 16
MXU_COLS = 256
VMEM_LIMIT = 56 * 1024 * 1024

NEG = -1e30
INT_MIN = np.int32(-2 ** 31)
F32 = jnp.float32
BF16 = jnp.bfloat16
V_ROWS = 2 * HEAD_DIM + BF16_ROWS

A_PERM = tuple(c + (A_HEADS // 2) * p for c in range(A_HEADS // 2) for p in range(2))
C_PERM = tuple(c + (C_HEADS // 2) * p for c in range(C_HEADS // 2) for p in range(2))


def _dot(a, b):
    return jnp.dot(a, b, preferred_element_type=F32)


def _dot_nt(a, b):
    return lax.dot_general(a, b, (((1,), (1,)), ((), ())), preferred_element_type=F32)


def _row_tile(n):
    for t in (512, 256, 128, 64):
        if n % t == 0:
            return t
    raise ValueError(f"row count {n} is not a multiple of {CHUNK}")


def _const_spec(shape):
    nd = len(shape)
    return pl.BlockSpec(shape, lambda *_: (0,) * nd)


def _params(n_axes):
    return pltpu.CompilerParams(dimension_semantics=("arbitrary",) * n_axes,
                                vmem_limit_bytes=VMEM_LIMIT)


def _group_ms(hb, bd):
    sq = (hb * hb).astype(BF16)
    w = hb.shape[1]
    parts = [_dot(sq[:, i:i + MXU_COLS], bd) for i in range(0, w, MXU_COLS)]
    return parts[0] if len(parts) == 1 else jnp.concatenate(parts, axis=1)


def _rope_blocks(y, cos, sin, first_half):
    out = []
    for i in range(0, y.shape[1], LANES):
        yb = y[:, i:i + LANES]
        sw = jnp.where(first_half, pltpu.roll(yb, LANES - 32, 1), pltpu.roll(yb, 32, 1))
        out.append(yb * cos + sw * sin)
    return out[0] if len(out) == 1 else jnp.concatenate(out, axis=1)


def _normed_input(x_ref, g_ref):
    x = x_ref[...]
    ms = jnp.mean(x * x, axis=-1, keepdims=True)
    return (x * lax.rsqrt(ms + EPS) * g_ref[...]).astype(BF16)


def _even_in_kernel(x_ref, g_ref, w_ref, gain_ref, bd_ref, cos_ref, sin_ref,
                    aq_ref, iq_ref, bq_ref, bk_ref, bv_ref, bk16_ref, bv16_ref,
                    ak_ref, av_ref, ak16_ref, av16_ref, ik_ref, ik16_ref, iw_ref, vt_ref):
    tm = x_ref.shape[0]
    xn = _normed_input(x_ref, g_ref)
    bd = bd_ref[...]
    cos = cos_ref[...]
    sin = sin_ref[...]
    first_half = (lax.broadcasted_iota(jnp.int32, (tm, LANES), 1) % HEAD_DIM) < HEAD_DIM // 2

    def proj(c0, width):
        return _dot(xn, w_ref[:, c0:c0 + width])

    def normed(h, c0):
        return h * lax.rsqrt(_group_ms(h, bd) + EPS) * gain_ref[:, c0:c0 + h.shape[1]]

    h = proj(0, 512)
    aq_ref[...] = (_rope_blocks(normed(h, 0), cos, sin, first_half) * QK_SCALE_LOG2).astype(BF16)
    h = proj(512, 512)
    iq_ref[...] = _rope_blocks(h, cos, sin, first_half).astype(BF16)
    h = proj(1024, 512)
    bq_ref[...] = (normed(h, 1024) * QK_SCALE).astype(BF16)
    h = normed(proj(1536, 512), 1536)
    bk_ref[...] = h
    bk16_ref[...] = h.astype(BF16)
    h = proj(2048, 512)
    bv_ref[...] = h
    bv16_ref[...] = h.astype(BF16)
    h = proj(2560, 256)
    k = _rope_blocks(normed(h, 2560)[:, :LANES], cos, sin, first_half)
    ak_ref[...] = k
    ak16_ref[...] = k.astype(BF16)
    v = h[:, LANES:]
    av_ref[...] = v
    av16_ref[...] = v.astype(BF16)
    vt_ref[0:2 * HEAD_DIM, :] = v.T.astype(BF16)
    vt_ref[2 * HEAD_DIM:, :] = jnp.ones((BF16_ROWS, tm), BF16)
    h = proj(2816, 256)
    k = _rope_blocks(normed(h, 2816)[:, :LANES], cos, sin, first_half)
    ik_ref[...] = k[:, :IDX_DIM]
    ik16_ref[...] = k.astype(BF16)
    iw_ref[...] = h[:, LANES:] * IDX_W_SCALE


def _in_proj_call(kern, name, widths_dtypes, x, g, w, gains, bd, cos, sin, tail_only, blocks_per_seq,
                  values_transposed=False):
    n, d = x.shape
    tm = _row_tile(n)
    assert cos.shape[0] % tm == 0 and (n // tm) % blocks_per_seq == 0
    n_tab = cos.shape[0] // tm
    row = lambda width: pl.BlockSpec((tm, width), lambda i: (i, 0))
    tail = lambda width: pl.BlockSpec((tm, width), lambda i: (i // blocks_per_seq, 0))
    tab = pl.BlockSpec((tm, LANES), lambda i: (i % n_tab, 0))
    out_rows = [n // blocks_per_seq if o in tail_only else n for o in range(len(widths_dtypes))]
    out_specs = [tail(wd) if o in tail_only else row(wd) for o, (wd, _) in enumerate(widths_dtypes)]
    out_shape = [jax.ShapeDtypeStruct((r, wd), dt) for r, (wd, dt) in zip(out_rows, widths_dtypes)]
    if values_transposed:
        out_specs.append(pl.BlockSpec((None, V_ROWS, tm), lambda i: (i, 0, 0)))
        out_shape.append(jax.ShapeDtypeStruct((n // tm, V_ROWS, tm), BF16))
    return pl.pallas_call(
        kern,
        grid=(n // tm,),
        in_specs=[row(d), _const_spec((1, d)), _const_spec(w.shape), _const_spec(gains.shape),
                  _const_spec(bd.shape), tab, tab],
        out_specs=out_specs,
        out_shape=out_shape,
        compiler_params=_params(1),
        name=name,
    )(x, g, w, gains, bd, cos, sin)


def _even_in(x, g, w, gains, bd, cos, sin, blocks_per_seq):
    widths_dtypes = [(512, BF16), (512, BF16), (512, BF16), (512, F32), (512, F32), (512, BF16), (512, BF16),
                     (128, F32), (128, F32), (128, BF16), (128, BF16), (IDX_DIM, F32), (128, BF16), (128, F32)]
    return _in_proj_call(_even_in_kernel, "even_in_proj", widths_dtypes, x, g, w, gains, bd, cos, sin,
                         tail_only=(3, 4), blocks_per_seq=blocks_per_seq, values_transposed=True)


def _odd_in_kernel(x_ref, g_ref, w_ref, gain_ref, bd_ref, cos_ref, sin_ref,
                   q_ref, k_ref, v_ref, k16_ref, v16_ref):
    tm = x_ref.shape[0]
    xn = _normed_input(x_ref, g_ref)
    bd = bd_ref[...]
    cos = cos_ref[...]
    sin = sin_ref[...]
    first_half = (lax.broadcasted_iota(jnp.int32, (tm, LANES), 1) % HEAD_DIM) < HEAD_DIM // 2
    for c0 in (0, 512):
        h = _dot(xn, w_ref[:, c0:c0 + 512])
        h = h * lax.rsqrt(_group_ms(h, bd) + EPS) * gain_ref[:, c0:c0 + 512]
        q_ref[:, c0:c0 + 512] = (_rope_blocks(h, cos, sin, first_half) * QK_SCALE).astype(BF16)
    h = _dot(xn, w_ref[:, 1024:1280])
    hn = h * lax.rsqrt(_group_ms(h, bd) + EPS) * gain_ref[:, 1024:1280]
    k = _rope_blocks(hn[:, :LANES], cos, sin, first_half)
    k_ref[...] = k
    k16_ref[...] = k.astype(BF16)
    v = h[:, LANES:]
    v_ref[...] = v
    v16_ref[...] = v.astype(BF16)


def _odd_in(x, g, w, gains, bd, cos, sin, blocks_per_seq):
    widths_dtypes = [(1024, BF16), (128, F32), (128, F32), (128, BF16), (128, BF16)]
    return _in_proj_call(_odd_in_kernel, "odd_in_proj", widths_dtypes, x, g, w, gains, bd, cos, sin,
                         tail_only=(1, 2), blocks_per_seq=blocks_per_seq)


def _out_ffn_kernel(*refs, n_attn, d_ff, ff_chunk):
    h_ref = refs[0]
    attn_refs = refs[1:1 + n_attn]
    wo_refs = refs[1 + n_attn:1 + 2 * n_attn]
    g_ref, wgu_ref, wdn_ref, o_ref, yn_ref = refs[1 + 2 * n_attn:]
    o_ref[...] = h_ref[...]
    for a_ref, wo_ref in zip(attn_refs, wo_refs):
        o_ref[...] += _dot(a_ref[...], wo_ref[...])
    y = o_ref[...]
    ms = jnp.mean(y * y, axis=-1, keepdims=True)
    yn_ref[...] = (y * lax.rsqrt(ms + EPS) * g_ref[...]).astype(BF16)
    for c0 in range(0, d_ff, ff_chunk):
        gate = _dot(yn_ref[...], wgu_ref[:, c0:c0 + ff_chunk])
        up = _dot(yn_ref[...], wgu_ref[:, d_ff + c0:d_ff + c0 + ff_chunk])
        act = (gate * (1.0 / (1.0 + jnp.exp(-gate))) * up).astype(BF16)
        o_ref[...] += _dot(act, wdn_ref[c0:c0 + ff_chunk, :])


def _out_ffn(h, attns, wos, g, wgu, wdn, layer):
    n, d = h.shape
    tm = _row_tile(n)
    d_ff = wdn.shape[1]
    row = lambda width: pl.BlockSpec((tm, width), lambda i: (i, 0))
    layer_spec = lambda w: pl.BlockSpec((None,) + w.shape[1:], lambda i: (layer, 0, 0))
    kern = functools.partial(_out_ffn_kernel, n_attn=len(attns), d_ff=d_ff, ff_chunk=MXU_COLS)
    return pl.pallas_call(
        kern,
        grid=(n // tm,),
        in_specs=[row(d)] + [row(a.shape[1]) for a in attns] + [_const_spec(w.shape) for w in wos]
                 + [_const_spec((1, d)), layer_spec(wgu), layer_spec(wdn)],
        out_specs=row(d),
        out_shape=jax.ShapeDtypeStruct((n, d), F32),
        scratch_shapes=[pltpu.VMEM((tm, d), BF16)],
        compiler_params=_params(1),
        name="out_proj_ffn",
    )(h, *attns, *wos, g, wgu, wdn)


GROUP_KEYS = 32 * SUBLANES


def _bit_transpose_32(rows):
    a = list(rows)
    j = 16
    m = 0x0000FFFF
    while j:
        k = 0
        while k < 32:
            t = (a[k] ^ lax.shift_right_logical(a[k + j], jnp.int32(j))) & jnp.int32(m)
            a[k] = a[k] ^ t
            a[k + j] = a[k + j] ^ lax.shift_left(t, jnp.int32(j))
            k = (k + j + 1) & ~j
        j >>= 1
        if j:
            m = (m ^ (m << j)) & 0xFFFFFFFF
    return a


def _dsa_kernel(aq_ref, iq_ref, iw_ref, ik_ref, ak_ref, vt_ref, o_ref,
                iqs_ref, aqs_ref, w_ref, keys_ref, planes_ref, s_ref, acc_ref,
                *, tk, q_off, l_valid, topk, idx_bits, n_sub):
    tq = LANES
    n_slots = IDX_HEADS
    n_pairs = n_slots // 2
    n_chains = 4
    nkt_max = keys_ref.shape[1]
    groups_per_tile = tk // GROUP_KEYS
    n_groups = nkt_max * groups_per_tile
    jj = pl.program_id(1)
    krow = lax.broadcasted_iota(jnp.int32, (tk, tq), 0)
    group0 = lax.broadcasted_iota(jnp.int32, (2 * HEAD_DIM, tq), 0) < HEAD_DIM
    idx_all = jnp.int32(2 ** idx_bits - 1)

    def tile_info(t):
        qpos0 = q_off + (jj * n_sub + t) * tq
        n_adm_max = jnp.minimum(((qpos0 + tq - 1) // CHUNK + 1) * CHUNK, l_valid)
        qcol = qpos0 + lax.broadcasted_iota(jnp.int32, (1, tq), 1)
        return (n_adm_max + tk - 1) // tk, jnp.minimum((qcol // CHUNK + 1) * CHUNK, l_valid)

    def build_stacks(t):
        rows = slice(t * tq, (t + 1) * tq)
        for c in range(n_pairs):
            iqt = iq_ref[rows, c * LANES:(c + 1) * LANES].astype(F32).T
            aqt = aq_ref[rows, c * LANES:(c + 1) * LANES].astype(F32).T
            for p in range(2):
                n = 2 * c + p
                keep = group0 if p == 0 else jnp.logical_not(group0)
                iqs_ref[t, :, n * tq:(n + 1) * tq] = jnp.where(keep, iqt, 0.0).astype(BF16)
                aqs_ref[t, :, n * tq:(n + 1) * tq] = jnp.where(keep, aqt, 0.0).astype(BF16)
        w_ref[t] = iw_ref[rows, :].T

    def score_tile(t, klim, kt):
        par = t % 2
        k0 = pl.multiple_of(kt * tk, tk)
        ikt = ik_ref[pl.ds(k0, tk), :]
        score = jnp.zeros((tk, tq), F32)
        for c in range(n_pairs):
            s = _dot(ikt, iqs_ref[t, :, 2 * c * tq:2 * (c + 1) * tq])
            for p in range(2):
                score = score + jnp.maximum(s[:, p * tq:(p + 1) * tq], 0.0) * w_ref[t, 2 * c + p:2 * c + p + 1, :]
        bits = lax.bitcast_convert_type(score, jnp.int32)
        key = bits ^ ((bits >> 31) & jnp.int32(0x7FFFFFFF))
        key = jnp.where(key == -1, 0, key)
        key = jnp.where(krow + k0 < klim, key, INT_MIN)
        keys_ref[par, kt] = key
        ukey = key ^ INT_MIN
        for g in range(groups_per_tile):
            base = g * GROUP_KEYS
            planes = _bit_transpose_32([ukey[base + SUBLANES * i:base + SUBLANES * (i + 1), :] for i in range(32)])
            for b in range(32):
                planes_ref[b, par, kt * groups_per_tile + g] = planes[b]

    def clear_unused_planes(t, nkt):
        zero = jnp.zeros((SUBLANES, tq), jnp.int32)

        def body(kt, carry):
            for g in range(groups_per_tile):
                for b in range(32):
                    planes_ref[b, t % 2, kt * groups_per_tile + g] = zero
            return carry
        lax.fori_loop(nkt, nkt_max, body, 0)

    def popcount_rows(words):
        pcs = [lax.population_count(w) for w in words]
        chains = [sum(pcs[c::n_chains][1:], pcs[c]) for c in range(min(n_chains, len(pcs)))]
        return jnp.sum(sum(chains[1:], chains[0]).astype(F32), axis=0, keepdims=True)

    def search(t, nkt):
        par = t % 2

        def count(indicator):
            def body(kt, acc):
                ind = indicator(keys_ref[par, kt], kt)
                return acc + jnp.sum(ind.reshape(tk // (n_chains * SUBLANES), n_chains, SUBLANES, tq), axis=0)
            acc = lax.fori_loop(0, nkt, body, jnp.zeros((n_chains, SUBLANES, tq), F32))
            return jnp.sum(jnp.sum(acc, axis=0), axis=0, keepdims=True)

        def search_bit(i, carry):
            ans, n_gt, und = carry
            plane = planes_ref[i, par]
            hit = [und[g] & plane[g] for g in range(n_groups)]
            total = n_gt + popcount_rows(hit)
            take = total >= topk
            takem = jnp.where(take, jnp.int32(-1), jnp.int32(0))
            ans = jnp.where(take, ans | jnp.left_shift(jnp.int32(1), 31 - i), ans)
            n_gt = jnp.where(take, n_gt, total)
            und = tuple((und[g] ^ hit[g]) ^ (und[g] & takem) for g in range(n_groups))
            return ans, n_gt, und

        live_groups = nkt * groups_per_tile
        und0 = tuple(jnp.full((SUBLANES, tq), jnp.where(g < live_groups, jnp.int32(-1), jnp.int32(0)), jnp.int32)
                     for g in range(n_groups))
        ans, c_gt, und = lax.fori_loop(
            0, 32, search_bit, (jnp.zeros((1, tq), jnp.int32), jnp.zeros((1, tq), F32), und0))
        vstar = ans ^ INT_MIN
        c_ge = c_gt + popcount_rows(und)
        need = topk - c_gt
        has_tie = jnp.where(c_ge > topk, jnp.where(vstar > INT_MIN, 1.0, 0.0), 0.0)

        def tie_search():
            def tie_bit(i, jmax):
                cand = jmax | jnp.left_shift(jnp.int32(1), idx_bits - 1 - i)
                cnt = count(lambda k, kt: jnp.where(k == vstar, jnp.where(krow + kt * tk < cand, 1.0, 0.0), 0.0))
                return jnp.where(cnt <= need, cand, jmax)
            return lax.fori_loop(0, idx_bits, tie_bit, jnp.zeros((1, tq), jnp.int32))

        jmax = lax.cond(jnp.max(has_tie) > 0.0, tie_search, lambda: jnp.full((1, tq), idx_all, jnp.int32))
        return vstar, jnp.where(vstar == INT_MIN, 0, jmax)

    def attend_tile(t, vstar, jmax, kt, m_run):
        k0 = pl.multiple_of(kt * tk, tk)
        key = keys_ref[t % 2, kt]
        tie_ok = jnp.where(krow + k0 < jmax, 0.0, NEG)
        mb = jnp.where(key > vstar, 0.0, jnp.where(key == vstar, tie_ok, NEG))
        mb2 = jnp.concatenate([mb, mb], axis=1)
        akt = ak_ref[pl.ds(k0, tk), :]
        vt = vt_ref[kt]
        m_next = []
        for c in range(n_pairs):
            s = _dot(akt, aqs_ref[t, :, 2 * c * tq:2 * (c + 1) * tq]) + mb2
            s_ref[c] = s
            m_next.append(jnp.maximum(m_run[c], jnp.max(s, axis=0, keepdims=True)))
        for c in range(n_pairs):
            alpha = jnp.exp2(m_run[c] - m_next[c])
            pr = jnp.exp2(s_ref[c] - m_next[c])
            acc_ref[c] = alpha * acc_ref[c] + _dot(vt, pr.astype(BF16))
        return tuple(m_next)

    def finalize(t):
        for c in range(n_pairs):
            a = acc_ref[c]
            o = a[:2 * HEAD_DIM] / a[2 * HEAD_DIM:2 * HEAD_DIM + 1]
            o_ref[t * tq:(t + 1) * tq, c * LANES:(c + 1) * LANES] = (
                jnp.where(group0, o[:, :tq], o[:, tq:]).T.astype(BF16))

    info = [tile_info(t) for t in range(n_sub)]
    for t in range(n_sub):
        build_stacks(t)

    def score_only(t, lo, hi):
        def body(kt, carry):
            score_tile(t, info[t][1], kt)
            return carry
        lax.fori_loop(lo, hi, body, 0)

    score_only(0, 0, info[0][0])
    for t in range(n_sub):
        nkt = info[t][0]
        clear_unused_planes(t, nkt)
        vstar, jmax = search(t, nkt)
        acc_ref[...] = jnp.zeros(acc_ref.shape, F32)
        m0 = tuple(jnp.full((1, 2 * tq), NEG, F32) for _ in range(n_pairs))
        if t + 1 < n_sub:
            def fused(kt, m_run, t=t, vstar=vstar, jmax=jmax):
                score_tile(t + 1, info[t + 1][1], kt)
                return attend_tile(t, vstar, jmax, kt, m_run)
            lax.fori_loop(0, nkt, fused, m0)
            score_only(t + 1, nkt, info[t + 1][0])
        else:
            lax.fori_loop(0, nkt, lambda kt, m_run, t=t, vstar=vstar, jmax=jmax:
                          attend_tile(t, vstar, jmax, kt, m_run), m0)
        finalize(t)


def _dsa(aq, iq, iw, ik, ak, vt, *, n_batch, nq, tk, q_off, l_valid, topk, keys_3d):
    tq = LANES
    nkt_max = vt.shape[1]
    l_pad = nkt_max * tk
    assert l_valid <= l_pad and vt.shape[2:] == (V_ROWS, tk) and tk % GROUP_KEYS == 0
    n_sub = next(c for c in (8, 4, 2, 1) if nq % c == 0)
    steps = nq // n_sub
    qspec = lambda width: pl.BlockSpec((n_sub * tq, width), lambda b, j: (b * steps + j, 0))
    if keys_3d:
        kspec = pl.BlockSpec((None, l_pad, LANES), lambda b, j: (b, 0, 0))
    else:
        kspec = pl.BlockSpec((l_pad, LANES), lambda b, j: (b, 0))
    kern = functools.partial(_dsa_kernel, tk=tk, q_off=q_off, l_valid=l_valid, topk=topk,
                             idx_bits=int(l_pad).bit_length(), n_sub=n_sub)
    n_slots = IDX_HEADS
    return pl.pallas_call(
        kern,
        grid=(n_batch, steps),
        in_specs=[qspec(512), qspec(512), qspec(LANES), kspec, kspec,
                  pl.BlockSpec((None, nkt_max, V_ROWS, tk), lambda b, j: (b, 0, 0, 0))],
        out_specs=qspec(512),
        out_shape=jax.ShapeDtypeStruct((n_batch * nq * tq, 512), BF16),
        scratch_shapes=[
            pltpu.VMEM((n_sub, LANES, n_slots * tq), BF16),
            pltpu.VMEM((n_sub, LANES, n_slots * tq), BF16),
            pltpu.VMEM((n_sub, LANES, tq), F32),
            pltpu.VMEM((2, nkt_max, tk, tq), jnp.int32),
            pltpu.VMEM((32, 2, l_pad // GROUP_KEYS, SUBLANES, tq), jnp.int32),
            pltpu.VMEM((n_slots // 2, tk, 2 * tq), F32),
            pltpu.VMEM((n_slots // 2, V_ROWS, 2 * tq), F32),
        ],
        compiler_params=_params(2),
        name="dsa_attention",
    )(aq, iq, iw, ik, ak, vt)


def _fill_band(buf, prev_ref, cur_ref, n_front, rows):
    buf[0:n_front, :] = prev_ref[...].astype(BF16)
    buf[n_front:n_front + rows, :] = cur_ref[...]
    buf[n_front + rows:, :] = jnp.zeros((buf.shape[0] - n_front - rows, buf.shape[1]), BF16)


def _band_call(kern, name, q, kprev, kcur, vprev, vcur, consts, *, n_batch, t, n_chunks, n_front, prev_3d):
    rows = n_chunks * CHUNK
    nq = t // rows
    wq = q.shape[1]
    wkv = kcur.shape[1]
    assert t % rows == 0
    if prev_3d:
        pspec = pl.BlockSpec((None, n_front, wkv), lambda b, j: (b, 0, 0))
    else:
        assert rows % n_front == 0 and t % n_front == 0
        per_seq, per_step = t // n_front, rows // n_front
        pspec = pl.BlockSpec((n_front, wkv), lambda b, j: (jnp.maximum(b * per_seq + j * per_step - 1, 0), 0))
    cspec = pl.BlockSpec((rows, wkv), lambda b, j: (b * nq + j, 0))
    qspec = pl.BlockSpec((rows, wq), lambda b, j: (b * nq + j, 0))
    return pl.pallas_call(
        kern,
        grid=(n_batch, nq),
        in_specs=[qspec, pspec, cspec, pspec, cspec] + [_const_spec(c.shape) for c in consts],
        out_specs=qspec,
        out_shape=jax.ShapeDtypeStruct((n_batch * t, wq), BF16),
        scratch_shapes=[pltpu.VMEM((n_front + rows + CHUNK, wkv), BF16)] * 2,
        compiler_params=_params(2),
        name=name,
    )(q, kprev, kcur, vprev, vcur, *consts)


def _band_b_kernel(q_ref, kp_ref, kc_ref, vp_ref, vc_ref, bias_ref, o_ref, kbuf, vbuf,
                   *, n_chunks, n_front, front_valid):
    j = pl.program_id(1)
    rows = n_chunks * CHUNK
    bwp = n_front + 2 * CHUNK
    _fill_band(kbuf, kp_ref, kc_ref, n_front, rows)
    _fill_band(vbuf, vp_ref, vc_ref, n_front, rows)
    half0 = lax.broadcasted_iota(jnp.int32, (CHUNK, LANES), 1) < HEAD_DIM
    ucol = lax.broadcasted_iota(jnp.int32, (1, bwp), 1)
    ones = jnp.ones((bwp, LANES), BF16)
    for ci in range(n_chunks):
        r0 = ci * CHUNK
        if not front_valid:
            front_mask = jnp.where(ucol >= jnp.where(j > 0, 0, n_front - r0), 0.0, NEG)
        for cp in range(B_HEADS // 2):
            cols = slice(cp * LANES, (cp + 1) * LANES)
            qq = q_ref[r0:r0 + CHUNK, cols]
            lhs = jnp.concatenate([jnp.where(half0, qq, jnp.zeros_like(qq)),
                                   jnp.where(half0, jnp.zeros_like(qq), qq)], axis=0)
            s = _dot_nt(lhs, kbuf[r0:r0 + bwp, cols]) + bias_ref[cp]
            if not front_valid:
                s = s + front_mask
            m = jnp.max(s, axis=-1, keepdims=True)
            e_ = jnp.exp(s - m).astype(BF16)
            o = _dot(e_, jnp.concatenate([vbuf[r0:r0 + bwp, cols], ones], axis=1))
            o = o[:, :LANES] / o[:, LANES:]
            o_ref[r0:r0 + CHUNK, cols] = jnp.where(half0, o[:CHUNK], o[CHUNK:]).astype(BF16)


def _band_b(q, kprev, kcur, vprev, vcur, bias, *, n_batch, t, n_chunks, front_valid, prev_3d):
    n_front = B_LEFT_CHUNKS * CHUNK
    kern = functools.partial(_band_b_kernel, n_chunks=n_chunks, n_front=n_front, front_valid=front_valid)
    return _band_call(kern, "band_b_attention", q, kprev, kcur, vprev, vcur, [bias],
                      n_batch=n_batch, t=t, n_chunks=n_chunks, n_front=n_front, prev_3d=prev_3d)


def _band_c_kernel(q_ref, kp_ref, kc_ref, vp_ref, vc_ref, base_ref, o_ref, kbuf, vbuf,
                   *, n_chunks, n_front, front_valid):
    j = pl.program_id(1)
    rows = n_chunks * CHUNK
    bw = n_front + CHUNK
    n_slots = C_HEADS
    _fill_band(kbuf, kp_ref, kc_ref, n_front, rows)
    _fill_band(vbuf, vp_ref, vc_ref, n_front, rows)
    lane = lax.broadcasted_iota(jnp.int32, (CHUNK, LANES), 1)
    half0 = lane < HEAD_DIM
    in_band = lax.broadcasted_iota(jnp.int32, (n_slots * CHUNK, MXU_COLS), 1) < bw
    ucol = lax.broadcasted_iota(jnp.int32, (1, MXU_COLS), 1)
    ones_band = jnp.ones((bw, LANES), BF16)
    tail = jnp.concatenate([jnp.zeros((MXU_COLS - bw, LANES), BF16), jnp.ones((MXU_COLS - bw, LANES), BF16)], axis=1)
    for ci in range(n_chunks):
        r0 = ci * CHUNK
        kb = kbuf[r0:r0 + MXU_COLS, :]
        vb = jnp.concatenate([jnp.concatenate([vbuf[r0:r0 + bw, :], ones_band], axis=1), tail], axis=0)
        parts = []
        for cb in range(n_slots // 2):
            qb = q_ref[r0:r0 + CHUNK, cb * LANES:(cb + 1) * LANES]
            parts.append(jnp.where(half0, qb, jnp.zeros_like(qb)))
            parts.append(jnp.where(half0, jnp.zeros_like(qb), qb))
        s = _dot_nt(jnp.concatenate(parts, axis=0), kb)
        if not front_valid:
            s = s + jnp.where(ucol >= jnp.where(j > 0, 0, n_front - r0), 0.0, NEG)
        s = jnp.where(in_band, s, base_ref[...])
        m = jnp.max(s, axis=-1, keepdims=True)
        e_ = jnp.exp(s - m)
        o = _dot(e_.astype(BF16), vb)
        o = o[:, :LANES] / o[:, LANES:]
        for cb in range(n_slots // 2):
            o0 = o[(2 * cb) * CHUNK:(2 * cb + 1) * CHUNK]
            o1 = o[(2 * cb + 1) * CHUNK:(2 * cb + 2) * CHUNK]
            o_ref[r0:r0 + CHUNK, cb * LANES:(cb + 1) * LANES] = jnp.where(half0, o0, o1).astype(BF16)


def _band_c(q, kprev, kcur, vprev, vcur, base, *, n_batch, t, n_chunks, front_valid, prev_3d):
    n_front = C_LEFT_CHUNKS * CHUNK
    kern = functools.partial(_band_c_kernel, n_chunks=n_chunks, n_front=n_front, front_valid=front_valid)
    return _band_call(kern, "band_c_attention", q, kprev, kcur, vprev, vcur, [base],
                      n_batch=n_batch, t=t, n_chunks=n_chunks, n_front=n_front, prev_3d=prev_3d)


def _rope_tables(pos):
    half = HEAD_DIM // 2
    inv_freq = ROPE_THETA ** (-jnp.arange(half, dtype=F32) / half)
    ang = pos.astype(F32)[:, None] * inv_freq[None, :]
    cos = jnp.cos(ang)
    sin = jnp.sin(ang)
    return (jnp.concatenate([cos, cos, cos, cos], axis=1), jnp.concatenate([-sin, sin, -sin, sin], axis=1))


def _perm_heads_cols(w, perm):
    d = w.shape[0]
    return w.reshape(d, len(perm), HEAD_DIM)[:, np.asarray(perm)].reshape(d, len(perm) * HEAD_DIM)


def _perm_heads_rows(w, perm):
    d = w.shape[1]
    return w.reshape(len(perm), HEAD_DIM, d)[np.asarray(perm)].reshape(len(perm) * HEAD_DIM, d)


def _even_weights(w_in, a_qn, a_kn, i_kn, b_qn, b_kn):
    d = w_in.shape[0]
    sizes = (A_HEADS * HEAD_DIM, A_KV_HEADS * HEAD_DIM, A_KV_HEADS * HEAD_DIM, IDX_HEADS * IDX_DIM, IDX_DIM,
             IDX_HEADS, B_HEADS * HEAD_DIM, B_HEADS * HEAD_DIM, B_HEADS * HEAD_DIM)
    aq, ak, av, iq, ik, iw, bq, bk, bv = jnp.split(w_in, np.cumsum(sizes)[:-1].tolist(), axis=1)
    w = jnp.concatenate([_perm_heads_cols(aq, A_PERM), iq, bq, bk, bv, ak, av, ik, ik,
                         iw, jnp.zeros((d, LANES - IDX_HEADS), w_in.dtype)], axis=1).astype(BF16)
    one = lambda n: jnp.ones((n,), F32)
    gains = jnp.concatenate([jnp.tile(a_qn, A_HEADS), one(512), jnp.tile(b_qn, B_HEADS), jnp.tile(b_kn, B_HEADS),
                             one(512), jnp.tile(a_kn, A_KV_HEADS), one(128), jnp.tile(i_kn, 2), one(128)])
    return w, gains[None, :].astype(F32)


def _odd_weights(w_in, c_qn, c_kn):
    q, k, v = jnp.split(w_in, [C_HEADS * HEAD_DIM, (C_HEADS + C_KV_HEADS) * HEAD_DIM], axis=1)
    w = jnp.concatenate([_perm_heads_cols(q, C_PERM), k, v], axis=1).astype(BF16)
    gains = jnp.concatenate([jnp.tile(c_qn, C_HEADS), jnp.tile(c_kn, C_KV_HEADS), jnp.ones((128,), F32)])
    return w, gains[None, :].astype(F32)


def _pad_rows(x, n, front):
    pad = [(0, 0)] * x.ndim
    pad[1] = (n, 0) if front else (0, n)
    return jnp.pad(x, pad)


def _values_transposed(v, tk):
    nbt, l_pad, _ = v.shape
    vt = jnp.swapaxes(v.reshape(nbt, l_pad // tk, tk, LANES), 2, 3)
    return jnp.concatenate([vt, jnp.ones((nbt, l_pad // tk, BF16_ROWS, tk), v.dtype)], axis=2)


def _band_b_bias(rb):
    n_front = B_LEFT_CHUNKS * CHUNK
    bw = n_front + CHUNK
    n_flat = n_front - B_MAX_REL + CHUNK
    ext = jnp.concatenate([jnp.broadcast_to(rb[:, :1], (rb.shape[0], n_flat)), rb[:, 1:B_MAX_REL + CHUNK]], axis=1)
    period = ext.shape[1]
    rolled = jnp.roll(ext, -(CHUNK - 1), axis=1)
    bias = jnp.tile(rolled, (1, CHUNK))[:, :CHUNK * (period - 1)].reshape(rb.shape[0], CHUNK, period - 1)[:, :, :bw]
    bias = jnp.pad(bias, ((0, 0), (0, 0), (0, CHUNK)), constant_values=NEG)
    return bias.reshape(B_HEADS // 2, 2 * CHUNK, bw + CHUNK)


def kernel(x_prompt, x_sample, cache_a_k, cache_a_v, cache_a_kidx, cache_b_k, cache_b_v, cache_c_k, cache_c_v, norm_mix, norm_ffn, w_in_even, w_out_even, a_q_norm, a_k_norm, idx_k_norm, b_q_norm, b_k_norm, b_rel_bias, w_in_odd, w_out_odd, c_q_norm, c_k_norm, c_sinks, w_ffn_in, w_ffn_out):
    nb, seq, d = x_prompt.shape
    ns, t_new, _ = x_sample.shape
    past = cache_a_k.shape[2]
    depth = norm_mix.shape[0]
    assert seq % LANES == 0 and t_new == CHUNK and past % CHUNK == 0
    assert cache_b_k.shape[2] == B_LEFT_CHUNKS * CHUNK and cache_c_k.shape[2] == C_LEFT_CHUNKS * CHUNK
    topk_p = min(TOPK_MAX, seq // 4)
    topk_s = min(TOPK_MAX, (past + t_new) // 4)
    keep_b = min(B_LEFT_CHUNKS * CHUNK, seq)
    keep_c = min(C_LEFT_CHUNKS * CHUNK, seq)
    n_front_b = B_LEFT_CHUNKS * CHUNK
    n_front_c = C_LEFT_CHUNKS * CHUNK

    hp = x_prompt.reshape(nb * seq, d)
    hs = x_sample.reshape(ns * t_new, d)
    tm_p = _row_tile(nb * seq)
    bps_p = seq // tm_p if (seq % tm_p == 0 and tm_p >= max(keep_b, keep_c)) else 1
    tm_s = _row_tile(ns * t_new)
    pos_p = jnp.tile(jnp.arange(seq, dtype=jnp.int32), max(1, tm_p // seq))
    pos_s = jnp.tile(past + jnp.arange(t_new, dtype=jnp.int32), max(1, tm_s // t_new))
    cos_p, sin_p = _rope_tables(pos_p)
    cos_s, sin_s = _rope_tables(pos_s)
    gid = np.arange(MXU_COLS) // HEAD_DIM
    bd = jnp.asarray((gid[:, None] == gid[None, :]).astype(np.float32) / HEAD_DIM, BF16)

    tk_p = tm_p
    assert seq % tk_p == 0 and tk_p % GROUP_KEYS == 0
    tk_s = 512
    l_s = past + t_new
    l_s_pad = -(-l_s // tk_s) * tk_s
    assert seq % n_front_b == 0
    ch_p = n_front_b // CHUNK

    outs = {k: [] for k in ("pa_k", "pa_v", "pa_i", "pb_k", "pb_v", "pc_k", "pc_v",
                            "sa_k", "sa_v", "sa_i", "sb_k", "sb_v", "sc_k", "sc_v")}

    def last_rows(x, keep, width):
        rows = x.shape[0] // nb
        return x.reshape(nb, rows, width)[:, rows - keep:]

    def pad_queries(x):
        x = x.reshape(ns, t_new, x.shape[1])
        return jnp.concatenate([x] * (LANES // t_new), axis=1).reshape(ns * LANES, x.shape[2])

    wgu = w_ffn_in.astype(BF16)
    wdn = w_ffn_out.astype(BF16)
    for layer in range(depth):
        li = layer // 2
        g_mix = norm_mix[layer][None, :]
        g_ffn = norm_ffn[layer][None, :]
        if layer % 2 == 0:
            w, gains = _even_weights(w_in_even[li], a_q_norm[li], a_k_norm[li], idx_k_norm[li],
                                     b_q_norm[li], b_k_norm[li])
            wo = w_out_even[li]
            wo_a = _perm_heads_rows(wo[:A_HEADS * HEAD_DIM], A_PERM).astype(BF16)
            wo_b = wo[A_HEADS * HEAD_DIM:].astype(BF16)
            bias = _band_b_bias(b_rel_bias[li].astype(F32))

            (aq, iq, bq, bk, bv, bk16, bv16, ak, av, ak16, av16, ik, ik16, iw, vt) = _even_in(
                hp, g_mix, w, gains, bd, cos_p, sin_p, bps_p)
            vt = vt.reshape(nb, seq // tk_p, V_ROWS, tk_p)
            out_a = _dsa(aq, iq, iw, ik16, ak16, vt, n_batch=nb, nq=seq // LANES, tk=tk_p,
                         q_off=0, l_valid=seq, topk=topk_p, keys_3d=False)
            out_b = _band_b(bq, bk16, bk16, bv16, bv16, bias, n_batch=nb, t=seq, n_chunks=ch_p,
                            front_valid=False, prev_3d=False)
            hp = _out_ffn(hp, [out_a, out_b], [wo_a, wo_b], g_ffn, wgu, wdn, layer)
            outs["pa_k"].append(ak.reshape(nb, seq, A_KV_HEADS, HEAD_DIM))
            outs["pa_v"].append(av.reshape(nb, seq, A_KV_HEADS, HEAD_DIM))
            outs["pa_i"].append(ik.reshape(nb, seq, IDX_DIM))
            outs["pb_k"].append(last_rows(bk, keep_b, 512).reshape(nb, keep_b, B_HEADS, HEAD_DIM))
            outs["pb_v"].append(last_rows(bv, keep_b, 512).reshape(nb, keep_b, B_HEADS, HEAD_DIM))

            (aq, iq, bq, bk, bv, bk16, bv16, ak, av, ak16, av16, ik, ik16, iw, _) = _even_in(
                hs, g_mix, w, gains, bd, cos_s, sin_s, 1)
            ci = cache_a_kidx[li].astype(BF16)
            ik_all = jnp.concatenate([jnp.concatenate([ci, ci], axis=-1), ik16.reshape(ns, t_new, LANES)], axis=1)
            ak_all = jnp.concatenate([cache_a_k[li].reshape(ns, past, LANES).astype(BF16),
                                      ak16.reshape(ns, t_new, LANES)], axis=1)
            av_all = jnp.concatenate([cache_a_v[li].reshape(ns, past, LANES).astype(BF16),
                                      av16.reshape(ns, t_new, LANES)], axis=1)
            ik_all, ak_all, av_all = (_pad_rows(x, l_s_pad - l_s, False) for x in (ik_all, ak_all, av_all))
            out_a = _dsa(pad_queries(aq), pad_queries(iq), pad_queries(iw), ik_all, ak_all,
                         _values_transposed(av_all, tk_s), n_batch=ns, nq=1, tk=tk_s,
                         q_off=past, l_valid=l_s, topk=topk_s, keys_3d=True)
            out_a = out_a.reshape(ns, LANES, 512)[:, :t_new].reshape(ns * t_new, 512)
            out_b = _band_b(bq, cache_b_k[li].reshape(ns, n_front_b, 512), bk16,
                            cache_b_v[li].reshape(ns, n_front_b, 512), bv16, bias, n_batch=ns, t=t_new,
                            n_chunks=1, front_valid=True, prev_3d=True)
            hs = _out_ffn(hs, [out_a, out_b], [wo_a, wo_b], g_ffn, wgu, wdn, layer)
            outs["sa_k"].append(ak.reshape(ns, t_new, A_KV_HEADS, HEAD_DIM))
            outs["sa_v"].append(av.reshape(ns, t_new, A_KV_HEADS, HEAD_DIM))
            outs["sa_i"].append(ik.reshape(ns, t_new, IDX_DIM))
            outs["sb_k"].append(bk.reshape(ns, t_new, B_HEADS, HEAD_DIM))
            outs["sb_v"].append(bv.reshape(ns, t_new, B_HEADS, HEAD_DIM))
        else:
            w, gains = _odd_weights(w_in_odd[li], c_q_norm[li], c_k_norm[li])
            wo = _perm_heads_rows(w_out_odd[li], C_PERM).astype(BF16)
            sinks = c_sinks[li].astype(F32)[np.asarray(C_PERM)]
            base = jnp.full((C_HEADS, 1, MXU_COLS), NEG, F32).at[:, 0, n_front_c + CHUNK].set(sinks)
            base = jnp.broadcast_to(base, (C_HEADS, CHUNK, MXU_COLS)).reshape(C_HEADS * CHUNK, MXU_COLS)

            q, k, v, k16, v16 = _odd_in(hp, g_mix, w, gains, bd, cos_p, sin_p, bps_p)
            out_c = _band_c(q, k16, k16, v16, v16, base, n_batch=nb, t=seq, n_chunks=ch_p,
                            front_valid=False, prev_3d=False)
            hp = _out_ffn(hp, [out_c], [wo], g_ffn, wgu, wdn, layer)
            outs["pc_k"].append(last_rows(k, keep_c, LANES).reshape(nb, keep_c, C_KV_HEADS, HEAD_DIM))
            outs["pc_v"].append(last_rows(v, keep_c, LANES).reshape(nb, keep_c, C_KV_HEADS, HEAD_DIM))

            q, k, v, k16, v16 = _odd_in(hs, g_mix, w, gains, bd, cos_s, sin_s, 1)
            out_c = _band_c(q, cache_c_k[li].reshape(ns, n_front_c, LANES), k16,
                            cache_c_v[li].reshape(ns, n_front_c, LANES), v16, base, n_batch=ns, t=t_new,
                            n_chunks=1, front_valid=True, prev_3d=True)
            hs = _out_ffn(hs, [out_c], [wo], g_ffn, wgu, wdn, layer)
            outs["sc_k"].append(k.reshape(ns, t_new, C_KV_HEADS, HEAD_DIM))
            outs["sc_v"].append(v.reshape(ns, t_new, C_KV_HEADS, HEAD_DIM))

    st = lambda name: jnp.stack(outs[name])
    return (hp.reshape(nb, seq, d), hs.reshape(ns, t_new, d),
            st("pa_k"), st("pa_v"), st("pa_i"), st("pb_k"), st("pb_v"), st("pc_k"), st("pc_v"),
            st("sa_k"), st("sa_v"), st("sa_i"), st("sb_k"), st("sb_v"), st("sc_k"), st("sc_v"))
```
